```python
import jax, jax.numpy as jnp
from jax import lax
import numpy as np

D_MODEL = 1024
BATCH = 8
SEQ = 2048
DEPTH = 2

GRID_W = 64
CTX_LEN = 256
D_MIX = D_MODEL
D_RET = D_MIX // 2
D_RWKV = D_MIX - D_RET
HEAD_DIM = 64
N_RET_HEADS = D_RET // HEAD_DIM
N_RWKV_HEADS = D_RWKV // HEAD_DIM
RET_CHUNK = 128
LORA_W = 64
LORA_A = 64
SHIFT_K = 3
D_SHIFT = 3 * D_RWKV + LORA_W + LORA_A
D_IN = 4 * D_RET + D_SHIFT + D_RWKV
ROPE_BASE = 10000.0
NORM_EPS = 1e-6
RWKV_GN_EPS = 64e-5
IN_SPLITS = [D_RET, 2 * D_RET, 3 * D_RET, 4 * D_RET, 4 * D_RET + D_SHIFT]
RWKV_SPLITS = [D_RWKV, 2 * D_RWKV, 3 * D_RWKV, 3 * D_RWKV + LORA_W]

kernel_name = 'hymba_retention_rwkv7_prefix_dit'


def rms_norm(x, w):
    xf = x.astype(jnp.float32)
    return xf * lax.rsqrt(jnp.mean(xf * xf, axis=-1, keepdims=True) + NORM_EPS) * w


def adaln(cond, w_mod, b_mod):
    m = jax.nn.silu(cond.astype(jnp.float32)) @ w_mod + b_mod
    return jnp.split(m, 3, axis=-1)


def to_heads(t, n_heads):
    b, l, _ = t.shape
    return t.astype(jnp.float32).reshape(b, l, n_heads, HEAD_DIM).transpose(0, 2, 1, 3)


def rope_1d(xh, pos):
    nf = xh.shape[-1] // 2
    inv = ROPE_BASE ** (-jnp.arange(nf, dtype=jnp.float32) / nf)
    ang = pos[:, None] * inv[None, :]
    cos, sin = jnp.cos(ang), jnp.sin(ang)
    x1, x2 = xh[..., :nf], xh[..., nf:]
    return jnp.concatenate([x1 * cos - x2 * sin, x1 * sin + x2 * cos], axis=-1)


def rope_2d(x, row_pos, col_pos):
    half = x.shape[-1] // 2
    return jnp.concatenate([rope_1d(x[..., :half], row_pos), rope_1d(x[..., half:], col_pos)], axis=-1)


def retention_scan(q, k, v, log_gamma, s0):
    b, h, l, dh = q.shape
    n_chunks = l // RET_CHUNK
    to_chunks = lambda t: t.reshape(b, h, n_chunks, RET_CHUNK, dh).transpose(2, 0, 1, 3, 4)
    lg = log_gamma.astype(jnp.float32)[:, None]
    idx = jnp.arange(RET_CHUNK, dtype=jnp.float32)
    rel = idx[:, None] - idx[None, :]
    decay_mask = jnp.where(rel >= 0, jnp.exp(lg[:, :, None] * jnp.maximum(rel, 0.0)), 0.0)
    q_decay = jnp.exp(lg * (idx + 1.0))[:, :, None]
    k_decay = jnp.exp(lg * (RET_CHUNK - 1.0 - idx))[:, :, None]
    chunk_decay = jnp.exp(lg * RET_CHUNK)[:, :, None]

    def step(s, qkv):
        qc, kc, vc = qkv
        scores = jnp.einsum('bhid,bhjd->bhij', qc, kc) * decay_mask
        inner = jnp.einsum('bhij,bhjd->bhid', scores, vc)
        cross = jnp.einsum('bhid,bhde->bhie', qc * q_decay, s)
        s_new = s * chunk_decay + jnp.einsum('bhjd,bhje->bhde', kc * k_decay, vc)
        return s_new, inner + cross

    s_fin, o = lax.scan(step, s0, (to_chunks(q), to_chunks(k), to_chunks(v)))
    return o.transpose(1, 2, 0, 3, 4).reshape(b, h, l, dh), s_fin


def head_norm_merge(o, norm_w):
    o = o * lax.rsqrt(jnp.mean(o * o, axis=-1, keepdims=True) + NORM_EPS)
    b, h, l, dh = o.shape
    return o.transpose(0, 2, 1, 3).reshape(b, l, h * dh) * norm_w


def retention_branch(ql, kl, vl, qc, kc, vc, row_pos, col_pos, log_gamma, norm_w):
    hd = lambda t: to_heads(t, N_RET_HEADS)
    scale = HEAD_DIM ** -0.5
    ql = rope_2d(hd(ql), row_pos, col_pos)
    kl = rope_2d(hd(kl), row_pos, col_pos) * scale
    vl = hd(vl)
    qc, kc, vc = hd(qc), hd(kc) * scale, hd(vc)
    s0 = jnp.zeros((ql.shape[0], N_RET_HEADS, HEAD_DIM, HEAD_DIM), jnp.float32)
    fl = lambda t: jnp.flip(t, axis=2)
    oc_f, sc_f = retention_scan(qc, kc, vc, log_gamma[0], s0)
    ol_f, _ = retention_scan(ql, kl, vl, log_gamma[0], sc_f)
    oc_b, sc_b = retention_scan(fl(qc), fl(kc), fl(vc), log_gamma[1], s0)
    ol_b, _ = retention_scan(fl(ql), fl(kl), fl(vl), log_gamma[1], sc_b)
    out_l = head_norm_merge(ol_f + fl(ol_b), norm_w)
    out_c = head_norm_merge(oc_f + fl(oc_b), norm_w)
    return out_l, out_c


def centred_conv(x, w):
    k_size = w.shape[0]
    pad = k_size // 2
    l = x.shape[1]
    xp = jnp.pad(x, ((0, 0), (pad, pad), (0, 0)))
    return sum(xp[:, i:i + l] * w[i] for i in range(k_size))


def rwkv7_scan(r, w, k, v, kap, a, s0, reverse):
    xs = tuple(jnp.swapaxes(t, 0, 1) for t in (r, w, k, v, kap, a))

    def step(s, inp):
        r_t, w_t, k_t, v_t, kap_t, a_t = inp
        s_kap = jnp.einsum('bhvk,bhk->bhv', s, kap_t)
        s = (s * w_t[:, :, None, :]
             - s_kap[..., None] * (kap_t * a_t)[:, :, None, :]
             + v_t[..., None] * k_t[:, :, None, :])
        return s, jnp.einsum('bhvk,bhk->bhv', s, r_t)

    s_fin, y = lax.scan(step, s0, xs, reverse=reverse)
    return jnp.swapaxes(y, 0, 1), s_fin


def rwkv_branch(pl, pc, shift_w, w0, w2, a0, a2, k_k, k_a, r_k, ln_w, ln_b):
    def features(p):
        h = centred_conv(p.astype(jnp.float32), shift_w)
        r, k, v, xw, xa = jnp.split(h, RWKV_SPLITS, axis=-1)
        b, l, _ = r.shape
        hs = lambda t: t.reshape(b, l, N_RWKV_HEADS, HEAD_DIM)
        kk = hs(k * k_k)
        kk = kk / jnp.maximum(jnp.sqrt(jnp.sum(kk * kk, axis=-1, keepdims=True)), 1e-12)
        dirs = []
        for d in range(2):
            w_log = -jax.nn.softplus(-(w0[d] + jnp.tanh(xw) @ w2[d])) - 0.5
            decay = jnp.exp(-jnp.exp(w_log))
            a = jax.nn.sigmoid(a0[d] + xa @ a2[d])
            k_mod = k * (1.0 + (a - 1.0) * k_a)
            dirs.append((hs(r), hs(decay), hs(k_mod), hs(v), kk, hs(a)))
        bonus = jnp.sum(hs(r) * hs(k) * r_k, axis=-1, keepdims=True) * hs(v)
        return dirs, bonus

    def readout(y, bonus):
        mu = jnp.mean(y, axis=-1, keepdims=True)
        var = jnp.mean(jnp.square(y - mu), axis=-1, keepdims=True)
        y = (y - mu) * lax.rsqrt(var + RWKV_GN_EPS) * ln_w.reshape(N_RWKV_HEADS, HEAD_DIM) \
            + ln_b.reshape(N_RWKV_HEADS, HEAD_DIM) + bonus
        b, l = y.shape[:2]
        return y.reshape(b, l, D_RWKV)

    dirs_c, bonus_c = features(pc)
    dirs_l, bonus_l = features(pl)
    s0 = jnp.zeros((pl.shape[0], N_RWKV_HEADS, HEAD_DIM, HEAD_DIM), jnp.float32)
    y_c, y_l = 0.0, 0.0
    for d in range(2):
        rev = d == 1
        yc, sc = rwkv7_scan(*dirs_c[d], s0, rev)
        yl, _ = rwkv7_scan(*dirs_l[d], sc, rev)
        y_c = y_c + yc
        y_l = y_l + yl
    return readout(y_l, bonus_l), readout(y_c, bonus_c)


def setup_inputs(seed: int = 0) -> dict:
    key = jax.random.key(seed)
    ks = jax.random.split(key, 24)
    f32 = jnp.float32
    nrm = lambda k, shape, s: jax.random.normal(k, shape, f32) * s
    x = nrm(ks[0], (BATCH, SEQ, D_MODEL), 1.0)
    c = nrm(ks[1], (BATCH, D_MODEL), 1.0)
    ctx = nrm(ks[2], (BATCH, CTX_LEN, D_MODEL), 1.0)
    c_ctx = nrm(ks[3], (D_MODEL,), 1.0)
    norm_w = 1.0 + nrm(ks[4], (DEPTH, D_MODEL), 0.05)
    w_mod = nrm(ks[5], (DEPTH, D_MODEL, 3 * D_MODEL), 0.5 * D_MODEL ** -0.5)
    b_mod = nrm(ks[6], (DEPTH, 3 * D_MODEL), 0.02)
    w_in = nrm(ks[7], (DEPTH, D_MODEL, D_IN), D_MODEL ** -0.5)
    base_lg = jnp.asarray(np.log(1.0 - 2.0 ** (-5.0 - np.arange(N_RET_HEADS))).astype(np.float32))
    ret_log_gamma = base_lg * jnp.exp(nrm(ks[8], (DEPTH, 2, N_RET_HEADS), 0.1))
    ret_norm_w = 1.0 + nrm(ks[9], (DEPTH, D_RET), 0.05)
    base_shift = jnp.array([0.25, 0.5, 0.25], f32)[:, None]
    rwkv_shift_w = base_shift + nrm(ks[10], (DEPTH, SHIFT_K, D_SHIFT), 0.1)
    rwkv_w0 = jnp.linspace(-4.0, 1.0, D_RWKV, dtype=f32) + nrm(ks[11], (DEPTH, 2, D_RWKV), 0.2)
    rwkv_w2 = nrm(ks[12], (DEPTH, 2, LORA_W, D_RWKV), 0.5 * LORA_W ** -0.5)
    rwkv_a0 = nrm(ks[13], (DEPTH, 2, D_RWKV), 0.1)
    rwkv_a2 = nrm(ks[14], (DEPTH, 2, LORA_A, D_RWKV), 0.5 * LORA_A ** -0.5)
    rwkv_k_k = 0.85 + nrm(ks[15], (DEPTH, D_RWKV), 0.05)
    rwkv_k_a = 1.0 + nrm(ks[16], (DEPTH, D_RWKV), 0.05)
    rwkv_r_k = nrm(ks[17], (DEPTH, N_RWKV_HEADS, HEAD_DIM), 0.1)
    rwkv_ln_w = 1.0 + nrm(ks[18], (DEPTH, D_RWKV), 0.05)
    rwkv_ln_b = nrm(ks[19], (DEPTH, D_RWKV), 0.02)
    w_out = nrm(ks[20], (DEPTH, D_MIX, D_MODEL), D_MIX ** -0.5)
    final_norm_w = 1.0 + nrm(ks[21], (D_MODEL,), 0.05)
    return {'x': x, 'c': c, 'ctx': ctx, 'c_ctx': c_ctx, 'norm_w': norm_w, 'w_mod': w_mod,
            'b_mod': b_mod, 'w_in': w_in, 'ret_log_gamma': ret_log_gamma, 'ret_norm_w': ret_norm_w,
            'rwkv_shift_w': rwkv_shift_w, 'rwkv_w0': rwkv_w0, 'rwkv_w2': rwkv_w2, 'rwkv_a0': rwkv_a0,
            'rwkv_a2': rwkv_a2, 'rwkv_k_k': rwkv_k_k, 'rwkv_k_a': rwkv_k_a, 'rwkv_r_k': rwkv_r_k,
            'rwkv_ln_w': rwkv_ln_w, 'rwkv_ln_b': rwkv_ln_b, 'w_out': w_out, 'final_norm_w': final_norm_w}


def reference(x, c, ctx, c_ctx, norm_w, w_mod, b_mod, w_in, ret_log_gamma, ret_norm_w,
              rwkv_shift_w, rwkv_w0, rwkv_w2, rwkv_a0, rwkv_a2, rwkv_k_k, rwkv_k_a, rwkv_r_k,
              rwkv_ln_w, rwkv_ln_b, w_out, final_norm_w):
    f32 = jnp.float32
    seq_len = x.shape[1]
    ROWS = seq_len // GRID_W
    row_pos = jnp.repeat(jnp.arange(ROWS, dtype=f32), GRID_W)
    col_pos = jnp.tile(jnp.arange(GRID_W, dtype=f32), ROWS)
    xl, xc = x, ctx
    for layer in range(DEPTH):
        sh_l, sc_l, g_l = adaln(c, w_mod[layer], b_mod[layer])
        sh_c, sc_c, g_c = adaln(c_ctx, w_mod[layer], b_mod[layer])
        hl = rms_norm(xl, norm_w[layer]) * (1.0 + sc_l[:, None]) + sh_l[:, None]
        hc = rms_norm(xc, norm_w[layer]) * (1.0 + sc_c) + sh_c
        pl = hl @ w_in[layer]
        pc = hc @ w_in[layer]
        ql, kl, vl, gr_l, rw_l, gw_l = jnp.split(pl, IN_SPLITS, axis=-1)
        qc, kc, vc, gr_c, rw_c, gw_c = jnp.split(pc, IN_SPLITS, axis=-1)
        ret_l, ret_c = retention_branch(ql, kl, vl, qc, kc, vc, row_pos, col_pos,
                                        ret_log_gamma[layer], ret_norm_w[layer])
        rwo_l, rwo_c = rwkv_branch(rw_l, rw_c, rwkv_shift_w[layer], rwkv_w0[layer], rwkv_w2[layer],
                                   rwkv_a0[layer], rwkv_a2[layer], rwkv_k_k[layer], rwkv_k_a[layer],
                                   rwkv_r_k[layer], rwkv_ln_w[layer], rwkv_ln_b[layer])
        mix_l = jnp.concatenate([ret_l * jax.nn.silu(gr_l), rwo_l * jax.nn.silu(gw_l)], axis=-1) @ w_out[layer]
        xl = xl + g_l[:, None] * mix_l
        if layer < DEPTH - 1:
            mix_c = jnp.concatenate([ret_c * jax.nn.silu(gr_c), rwo_c * jax.nn.silu(gw_c)], axis=-1) @ w_out[layer]
            xc = xc + g_c * mix_c
    return rms_norm(xl, final_norm_w)
```

```python
import functools
import math

import jax
import jax.numpy as jnp
from jax import lax
from jax.experimental import pallas as pl
from jax.experimental.pallas import tpu as pltpu

f32 = jnp.float32
bf16 = jnp.bfloat16

HEAD = 64
RET_C = 128
RW_C = 64
GRID_W = 64
ROPE_BASE = 10000.0
NORM_EPS = 1e-6
GN_EPS = 64e-5
ROW_TILE = 256
HALF = 256
NH = HALF // HEAD
VMEM_LIMIT = 56 * 1024 * 1024

NN = ((1,), (0,))
NT = ((1,), (1,))
TN = ((0,), (0,))

P_MOD = 3
P_IN = 3
P_OUT = 3
P_RET = 3
P_RW = 3


def _dg(a, b, dims):
    return lax.dot_general(a, b, (dims, ((), ())), preferred_element_type=f32)


def _split2(a):
    hi = a.astype(bf16)
    lo = (a - hi.astype(f32)).astype(bf16)
    return hi, lo


def _split3(a):
    hi = a.astype(bf16)
    r1 = a - hi.astype(f32)
    mid = r1.astype(bf16)
    lo = (r1 - mid.astype(f32)).astype(bf16)
    return hi, mid, lo


def _mm(a, b, dims=NN, passes=3):
    if passes == 1:
        return _dg(a.astype(bf16), b.astype(bf16), dims)
    ah, al = _split2(a)
    bh, bl = _split2(b)
    return _dg(ah, bh, dims) + (_dg(ah, bl, dims) + _dg(al, bh, dims))


def _mm_sel(sel, x):
    h, m, l = _split3(x)
    return _dg(sel, h, NN) + (_dg(sel, m, NN) + _dg(sel, l, NN))


def _group_sum(x, gmat):
    h, m, l = _split3(x)
    return _dg(h, gmat, NN) + (_dg(m, gmat, NN) + _dg(l, gmat, NN))


def _group_mat(n):
    ii = lax.broadcasted_iota(jnp.int32, (n, n), 0) // HEAD
    jj = lax.broadcasted_iota(jnp.int32, (n, n), 1) // HEAD
    return jnp.where(ii == jj, 1.0, 0.0).astype(bf16)


def _mod_kernel(c_ref, w_ref, b_ref, o_ref):
    cnd = c_ref[...]
    s = cnd * jax.nn.sigmoid(cnd)
    o_ref[0] = _mm(s, w_ref[0], NN, P_MOD) + b_ref[0]


def _modulation(cond, w_mod, b_mod):
    depth, d, d3 = w_mod.shape
    rows = cond.shape[0]
    tn = 1024
    return pl.pallas_call(
        _mod_kernel,
        grid=(depth, d3 // tn),
        in_specs=[pl.BlockSpec((rows, d), lambda l, j: (0, 0)),
                  pl.BlockSpec((1, d, tn), lambda l, j: (l, 0, j)),
                  pl.BlockSpec((1, 1, tn), lambda l, j: (l, 0, j))],
        out_specs=pl.BlockSpec((1, rows, tn), lambda l, j: (l, 0, j)),
        out_shape=jax.ShapeDtypeStruct((depth, rows, d3), f32),
        compiler_params=pltpu.CompilerParams(vmem_limit_bytes=VMEM_LIMIT),
        name="adaln_mod",
    )(cond, w_mod, b_mod.reshape(depth, 1, d3))


def _inproj_kernel(x_ref, mod_ref, nw_ref, wh_ref, wl_ref, cos_ref, sin_ref,
                   qkv_ref, rw_ref, g_ref, *, d_ret, d_shift):
    x = x_ref[0]
    ms = jnp.mean(x * x, axis=-1, keepdims=True)
    m = mod_ref[0, 0]
    h = x * lax.rsqrt(ms + NORM_EPS) * nw_ref[...]
    h = h * (1.0 + m[0:1]) + m[1:2]
    if P_IN == 1:
        hh = h.astype(bf16)
        hl = None
    else:
        hh, hl = _split2(h)

    def proj(lo, hi):
        wh = wh_ref[:, lo:hi]
        acc = _dg(hh, wh, NN)
        if hl is not None:
            acc = acc + (_dg(hh, wl_ref[:, lo:hi], NN) + _dg(hl, wh, NN))
        return acc

    cos = cos_ref[...]
    sin = sin_ref[...]
    lane = lax.broadcasted_iota(jnp.int32, cos.shape, 1)
    first = (lane % 32) < 16

    def rope(t):
        nl = t.shape[-1]
        swapped = jnp.where(first, pltpu.roll(t, nl - 16, 1), pltpu.roll(t, 16, 1))
        return t * cos + swapped * sin

    qkv_ref[0, :, 0:d_ret] = rope(proj(0, d_ret))
    qkv_ref[0, :, d_ret:2 * d_ret] = rope(proj(d_ret, 2 * d_ret)) * (HEAD ** -0.5)
    qkv_ref[0, :, 2 * d_ret:3 * d_ret] = proj(2 * d_ret, 3 * d_ret)
    g_ref[0, :, 0:d_ret] = proj(3 * d_ret, 4 * d_ret)
    o = 4 * d_ret
    rw_ref[0] = proj(o, o + d_shift)
    g_ref[0, :, d_ret:] = proj(o + d_shift, wh_ref.shape[1])


def _inproj(xall, modv, nw, wh, wl, cos_t, sin_t, d_ret, d_shift, d_rwkv):
    b, t, d = xall.shape
    n_tiles = t // ROW_TILE
    const = dict(pipeline_mode=pl.Buffered(1))
    kern = functools.partial(_inproj_kernel, d_ret=d_ret, d_shift=d_shift)
    return pl.pallas_call(
        kern,
        grid=(b, n_tiles),
        in_specs=[pl.BlockSpec((1, ROW_TILE, d), lambda i, j: (i, j, 0)),
                  pl.BlockSpec((1, 1, 8, d), lambda i, j: (i, jnp.minimum(j, 1), 0, 0)),
                  pl.BlockSpec((1, d), lambda i, j: (0, 0)),
                  pl.BlockSpec(wh.shape, lambda i, j: (0, 0), **const),
                  pl.BlockSpec(wl.shape, lambda i, j: (0, 0), **const),
                  pl.BlockSpec((ROW_TILE, d_ret), lambda i, j: (j, 0)),
                  pl.BlockSpec((ROW_TILE, d_ret), lambda i, j: (j, 0))],
        out_specs=[pl.BlockSpec((1, ROW_TILE, 3 * d_ret), lambda i, j: (i, j, 0)),
                   pl.BlockSpec((1, ROW_TILE, d_shift), lambda i, j: (i, j, 0)),
                   pl.BlockSpec((1, ROW_TILE, d_ret + d_rwkv), lambda i, j: (i, j, 0))],
        out_shape=[jax.ShapeDtypeStruct((b, t, 3 * d_ret), f32),
                   jax.ShapeDtypeStruct((b, t, d_shift), f32),
                   jax.ShapeDtypeStruct((b, t, d_ret + d_rwkv), f32)],
        compiler_params=pltpu.CompilerParams(vmem_limit_bytes=VMEM_LIMIT),
        name="in_proj",
    )(xall, modv, nw, wh, wl, cos_t, sin_t)


def _ret_kernel(lg_ref, q_ref, k_ref, v_ref, nw_ref, o_ref,
                s_ref, mask_ref, qd_ref, kd_ref, cd_ref, *, n_ctx_chunks, n_chunks):
    hh = pl.program_id(1)
    c = RET_C
    ii = lax.broadcasted_iota(jnp.int32, (c, c), 0)
    jj = lax.broadcasted_iota(jnp.int32, (c, c), 1)
    ri = lax.broadcasted_iota(jnp.int32, (c, HEAD), 0).astype(f32)
    for d in range(2):
        rel = ((ii - jj) if d == 0 else (jj - ii)).astype(f32)
        for h in range(NH):
            lg = lg_ref[d, hh * NH + h]
            mask_ref[d, h] = jnp.where(rel >= 0.0, jnp.exp(lg * jnp.maximum(rel, 0.0)), 0.0)
            qpow = (ri + 1.0) if d == 0 else (c - ri)
            kpow = (c - 1.0 - ri) if d == 0 else ri
            qd_ref[d, h] = jnp.exp(lg * qpow)
            kd_ref[d, h] = jnp.exp(lg * kpow)
            cd_ref[d, h] = jnp.exp(jnp.full((HEAD, HEAD), lg * c, f32))
    s_ref[...] = jnp.zeros(s_ref.shape, f32)
    o_ref[...] = jnp.zeros(o_ref.shape, f32)

    def step(s, carry):
        for d in range(2):
            if d == 0:
                ch = s
            else:
                ch = jnp.where(s < n_ctx_chunks, n_ctx_chunks - 1 - s, n_chunks + n_ctx_chunks - 1 - s)
            r0 = pl.multiple_of(ch * c, c)
            q = q_ref[0, pl.ds(r0, c), :]
            k = k_ref[0, pl.ds(r0, c), :]
            v = v_ref[0, pl.ds(r0, c), :]
            for h in range(NH):
                sl = slice(h * HEAD, (h + 1) * HEAD)
                qh, kh, vh = q[:, sl], k[:, sl], v[:, sl]
                st = s_ref[d, h]
                sc = _mm(qh, kh, NT, P_RET) * mask_ref[d, h]
                o = _mm(sc, vh, NN, P_RET) + _mm(qh * qd_ref[d, h], st, NN, P_RET)
                s_ref[d, h] = st * cd_ref[d, h] + _mm(kh * kd_ref[d, h], vh, TN, P_RET)
                o_ref[0, pl.ds(r0, c), sl] += o
        return carry

    lax.fori_loop(0, n_chunks, step, 0)

    gmat = _group_mat(HALF)
    nw = nw_ref[...]

    def norm_tile(i, carry):
        r0 = pl.multiple_of(i * ROW_TILE, ROW_TILE)
        o = o_ref[0, pl.ds(r0, ROW_TILE), :]
        ms = _group_sum(o * o, gmat) * (1.0 / HEAD)
        o_ref[0, pl.ds(r0, ROW_TILE), :] = o * lax.rsqrt(ms + NORM_EPS) * nw
        return carry

    lax.fori_loop(0, (n_chunks * c) // ROW_TILE, norm_tile, 0)


def _retention(qkv, log_gamma, norm_w, n_ctx):
    b, t, w3 = qkv.shape
    d_ret = w3 // 3
    nhalf = d_ret // HALF
    kern = functools.partial(_ret_kernel, n_ctx_chunks=n_ctx // RET_C, n_chunks=t // RET_C)
    blk = lambda off: pl.BlockSpec((1, t, HALF), lambda i, j, off=off: (i, 0, off * nhalf + j))
    return pl.pallas_call(
        kern,
        grid=(b, nhalf),
        in_specs=[pl.BlockSpec(memory_space=pltpu.SMEM),
                  blk(0), blk(1), blk(2),
                  pl.BlockSpec((1, HALF), lambda i, j: (0, j))],
        out_specs=pl.BlockSpec((1, t, HALF), lambda i, j: (i, 0, j)),
        out_shape=jax.ShapeDtypeStruct((b, t, d_ret), f32),
        scratch_shapes=[pltpu.VMEM((2, NH, HEAD, HEAD), f32),
                        pltpu.VMEM((2, NH, RET_C, RET_C), f32),
                        pltpu.VMEM((2, NH, RET_C, HEAD), f32),
                        pltpu.VMEM((2, NH, RET_C, HEAD), f32),
                        pltpu.VMEM((2, NH, HEAD, HEAD), f32)],
        compiler_params=pltpu.CompilerParams(vmem_limit_bytes=VMEM_LIMIT),
        name="retention",
    )(log_gamma, qkv, qkv, qkv, norm_w)


def _tri_inverse(low, eye, bd16, off32, off64):
    nd = jnp.where(bd16, -low, 0.0)
    p = eye + nd
    n2 = _mm(nd, nd, NN, P_RW)
    p = p + _mm(p, n2, NN, P_RW)
    n4 = _mm(n2, n2, NN, P_RW)
    p = p + _mm(p, n4, NN, P_RW)
    n8 = _mm(n4, n4, NN, P_RW)
    p = p + _mm(p, n8, NN, P_RW)
    c32 = jnp.where(off32, low, 0.0)
    p = p - _mm(_mm(p, c32, NN, P_RW), p, NN, P_RW)
    c64 = jnp.where(off64, low, 0.0)
    p = p - _mm(_mm(p, c64, NN, P_RW), p, NN, P_RW)
    return p


def _rwkv_kernel(r_ref, k_ref, v_ref, xwa_ref, swr_ref, swk_ref, swv_ref, swx_ref,
                 w0_ref, w2_ref, a0_ref, a2_ref, kk_ref, ka_ref, rk_ref, lnw_ref, lnb_ref,
                 o_ref, rs_ref, ks_ref, vs_ref, kap_ref, bon_ref, xs_ref, s_ref,
                 *, n_ctx_chunks, n_chunks):
    c = RW_C
    t_rows = n_chunks * c
    gmat = _group_mat(HALF)
    rows = lax.broadcasted_iota(jnp.int32, (c, 1), 0)

    def conv(ref, sw_ref, ch, r0):
        x = ref[0, pl.ds(r0, c), :]
        p0 = pl.multiple_of(jnp.maximum(r0 - 8, 0), 8)
        n0 = pl.multiple_of(jnp.minimum(r0 + c, t_rows - 8), 8)
        prev8 = ref[0, pl.ds(p0, 8), :]
        next8 = ref[0, pl.ds(n0, 8), :]
        has_prev = jnp.logical_and(ch != 0, ch != n_ctx_chunks).astype(f32)
        has_next = jnp.logical_and(ch != n_ctx_chunks - 1, ch != n_chunks - 1).astype(f32)
        xd = jnp.where(rows == 0, prev8[7:8, :] * has_prev, pltpu.roll(x, 1, 0))
        xu = jnp.where(rows == c - 1, next8[0:1, :] * has_next, pltpu.roll(x, c - 1, 0))
        sw = sw_ref[...]
        return sw[0:1] * xd + sw[1:2] * x + sw[2:3] * xu

    def features(ch, carry):
        r0 = pl.multiple_of(ch * c, c)
        r = conv(r_ref, swr_ref, ch, r0)
        k = conv(k_ref, swk_ref, ch, r0)
        v = conv(v_ref, swv_ref, ch, r0)
        kk = k * kk_ref[...]
        nrm = jnp.sqrt(_group_sum(kk * kk, gmat))
        sl = pl.ds(r0, c)
        rs_ref[sl, :] = r
        ks_ref[sl, :] = k
        vs_ref[sl, :] = v
        kap_ref[sl, :] = kk / jnp.maximum(nrm, 1e-12)
        bon_ref[sl, :] = _group_sum(r * k * rk_ref[...], gmat) * v
        xs_ref[sl, :] = conv(xwa_ref, swx_ref, ch, r0)
        return carry

    lax.fori_loop(0, n_chunks, features, 0)

    ii = lax.broadcasted_iota(jnp.int32, (c, c), 0)
    jj = lax.broadcasted_iota(jnp.int32, (c, c), 1)
    eye = jnp.where(ii == jj, 1.0, 0.0)
    bd16 = (ii // 16) == (jj // 16)
    same32 = (ii // 32) == (jj // 32)
    off32 = jnp.logical_and(same32, jnp.logical_not(bd16))
    off64 = jnp.logical_not(same32)
    strict = (jj < ii, jj > ii)
    incl = (jj <= ii, jj >= ii)
    tri = tuple(jnp.where(m, 1.0, 0.0).astype(bf16) for m in incl)
    lora = xs_ref.shape[1] // 2
    wscale = -math.exp(-0.5)

    s_ref[...] = jnp.zeros(s_ref.shape, f32)
    o_ref[...] = jnp.zeros(o_ref.shape, f32)

    def step(s, carry):
        for d in range(2):
            if d == 0:
                ch = s
            else:
                ch = jnp.where(s < n_ctx_chunks, n_ctx_chunks - 1 - s, n_chunks + n_ctx_chunks - 1 - s)
            r0 = pl.multiple_of(ch * c, c)
            sl = pl.ds(r0, c)
            xwa = xs_ref[sl, :]
            r = rs_ref[sl, :]
            k = ks_ref[sl, :]
            v = vs_ref[sl, :]
            kap = kap_ref[sl, :]
            u = w0_ref[d:d + 1, :] + _mm(jnp.tanh(xwa[:, :lora]), w2_ref[d], NN, P_RW)
            lw = wscale * jax.nn.sigmoid(u)
            a = jax.nn.sigmoid(a0_ref[d:d + 1, :] + _mm(xwa[:, lora:], a2_ref[d], NN, P_RW))
            kmod = k * (1.0 + (a - 1.0) * ka_ref[...])
            b = kap * a
            lp = _mm_sel(tri[d], lw)
            lpc = lp[c - 1:c, :] if d == 0 else lp[0:1, :]
            pinv = jnp.exp(-lp)
            ptail = jnp.exp(lpc - lp)
            pc = jnp.exp(lpc)
            kq = kap * jnp.exp(lp - lw)
            rq = r * jnp.exp(lp)
            kd = kmod * pinv
            bd = b * pinv
            kh = kmod * ptail
            bh = b * ptail
            for h in range(NH):
                hs = slice(h * HEAD, (h + 1) * HEAD)
                kr = jnp.concatenate([kq[:, hs], rq[:, hs]], axis=0)
                mk = _mm(kr, kd[:, hs], NT, P_RW)
                mb = _mm(kr, bd[:, hs], NT, P_RW)
                m_k = jnp.where(strict[d], mk[:c], 0.0)
                n_k = jnp.where(incl[d], mk[c:], 0.0)
                m_b = jnp.where(strict[d], mb[:c], 0.0)
                n_b = jnp.where(incl[d], mb[c:], 0.0)
                tinv = _tri_inverse(m_b, eye, bd16, off32, off64)
                st = s_ref[d, h]
                vh = v[:, hs]
                rhs = _mm(kq[:, hs], st, NT, P_RW) + _mm(m_k, vh, NN, P_RW)
                z = _mm(tinv, rhs, NN, P_RW)
                y = (_mm(rq[:, hs], st, NT, P_RW) + _mm(n_k, vh, NN, P_RW)) - _mm(n_b, z, NN, P_RW)
                s_ref[d, h] = (st * pc[:, hs] + _mm(vh, kh[:, hs], TN, P_RW)) - _mm(z, bh[:, hs], TN, P_RW)
                o_ref[0, sl, hs] += y
        return carry

    lax.fori_loop(0, n_chunks, step, 0)

    lnw = lnw_ref[...]
    lnb = lnb_ref[...]

    def readout(i, carry):
        r0 = pl.multiple_of(i * ROW_TILE, ROW_TILE)
        sl = pl.ds(r0, ROW_TILE)
        y = o_ref[0, sl, :]
        mu = _group_sum(y, gmat) * (1.0 / HEAD)
        yc = y - mu
        var = _group_sum(yc * yc, gmat) * (1.0 / HEAD)
        o_ref[0, sl, :] = yc * lax.rsqrt(var + GN_EPS) * lnw + lnb + bon_ref[sl, :]
        return carry

    lax.fori_loop(0, t_rows // ROW_TILE, readout, 0)


def _rwkv(rw, shift_w, w0, w2, a0, a2, k_k, k_a, r_k, ln_w, ln_b, n_ctx):
    b, t, d_shift = rw.shape
    d_rwkv = w0.shape[-1]
    lora2 = d_shift - 3 * d_rwkv
    nhalf = d_rwkv // HALF
    xcol = (3 * d_rwkv) // lora2
    kern = functools.partial(_rwkv_kernel, n_ctx_chunks=n_ctx // RW_C, n_chunks=t // RW_C)
    seq = lambda off: pl.BlockSpec((1, t, HALF), lambda i, j, off=off: (i, 0, off * nhalf + j))
    swb = lambda off: pl.BlockSpec((3, HALF), lambda i, j, off=off: (0, off * nhalf + j))
    vec = lambda rows: pl.BlockSpec((rows, HALF), lambda i, j: (0, j))
    lor = pl.BlockSpec((2, lora2 // 2, HALF), lambda i, j: (0, 0, j))
    return pl.pallas_call(
        kern,
        grid=(b, nhalf),
        in_specs=[seq(0), seq(1), seq(2),
                  pl.BlockSpec((1, t, lora2), lambda i, j: (i, 0, xcol)),
                  swb(0), swb(1), swb(2),
                  pl.BlockSpec((3, lora2), lambda i, j: (0, xcol)),
                  vec(2), lor, vec(2), lor, vec(1), vec(1), vec(1), vec(1), vec(1)],
        out_specs=pl.BlockSpec((1, t, HALF), lambda i, j: (i, 0, j)),
        out_shape=jax.ShapeDtypeStruct((b, t, d_rwkv), f32),
        scratch_shapes=[pltpu.VMEM((t, HALF), f32)] * 5
        + [pltpu.VMEM((t, lora2), f32), pltpu.VMEM((2, NH, HEAD, HEAD), f32)],
        compiler_params=pltpu.CompilerParams(vmem_limit_bytes=VMEM_LIMIT),
        name="rwkv7",
    )(rw, rw, rw, rw, shift_w, shift_w, shift_w, shift_w,
      w0, w2, a0, a2, k_k, k_a, r_k, ln_w, ln_b)


def _outproj_kernel(ret_ref, rwo_ref, g_ref, x_ref, mod_ref, wh_ref, wl_ref, fw_ref, o_ref,
                    *, final):
    d_ret = ret_ref.shape[-1]
    g = g_ref[0]
    act = jnp.concatenate([ret_ref[0] * (g[:, :d_ret] * jax.nn.sigmoid(g[:, :d_ret])),
                           rwo_ref[0] * (g[:, d_ret:] * jax.nn.sigmoid(g[:, d_ret:]))], axis=1)
    if P_OUT == 1:
        mix = _dg(act.astype(bf16), wh_ref[...], NN)
    else:
        ah, al = _split2(act)
        wh = wh_ref[...]
        mix = _dg(ah, wh, NN) + (_dg(ah, wl_ref[...], NN) + _dg(al, wh, NN))
    xn = x_ref[0] + mod_ref[0, 0][2:3] * mix
    if final:
        ms = jnp.mean(xn * xn, axis=-1, keepdims=True)
        xn = xn * lax.rsqrt(ms + NORM_EPS) * fw_ref[...]
    o_ref[0] = xn


def _outproj(ret, rwo, gates, xall, modv, wh, wl, fw, n_ctx, final):
    b, t, d = xall.shape
    d_ret = ret.shape[-1]
    d_rw = rwo.shape[-1]
    skip = (n_ctx // ROW_TILE) if final else 0
    n_tiles = t // ROW_TILE - skip
    const = dict(pipeline_mode=pl.Buffered(1))
    row = lambda w: pl.BlockSpec((1, ROW_TILE, w), lambda i, j: (i, j + skip, 0))
    kern = functools.partial(_outproj_kernel, final=final)
    return pl.pallas_call(
        kern,
        grid=(b, n_tiles),
        in_specs=[row(d_ret), row(d_rw), row(d_ret + d_rw), row(d),
                  pl.BlockSpec((1, 1, 8, d), lambda i, j: (i, jnp.minimum(j + skip, 1), 0, 0)),
                  pl.BlockSpec(wh.shape, lambda i, j: (0, 0), **const),
                  pl.BlockSpec(wl.shape, lambda i, j: (0, 0), **const),
                  pl.BlockSpec((1, d), lambda i, j: (0, 0))],
        out_specs=pl.BlockSpec((1, ROW_TILE, d), lambda i, j: (i, j, 0)),
        out_shape=jax.ShapeDtypeStruct((b, n_tiles * ROW_TILE, d), f32),
        compiler_params=pltpu.CompilerParams(vmem_limit_bytes=VMEM_LIMIT),
        name="out_proj",
    )(ret, rwo, gates, xall, modv, wh, wl, fw)


def _rope_tables(n_ctx, seq, d_ret):
    nf = HEAD // 4
    inv = ROPE_BASE ** (-jnp.arange(nf, dtype=f32) / nf)
    pos = jnp.arange(seq)
    row_pos = (pos // GRID_W).astype(f32)
    col_pos = (pos % GRID_W).astype(f32)
    ang_r = row_pos[:, None] * inv[None, :]
    ang_c = col_pos[:, None] * inv[None, :]
    cos_h = jnp.concatenate([jnp.cos(ang_r)] * 2 + [jnp.cos(ang_c)] * 2, axis=-1)
    sin_h = jnp.concatenate([-jnp.sin(ang_r), jnp.sin(ang_r), -jnp.sin(ang_c), jnp.sin(ang_c)], axis=-1)
    reps = d_ret // HEAD
    cos_l = jnp.tile(cos_h, (1, reps))
    sin_l = jnp.tile(sin_h, (1, reps))
    cos_t = jnp.concatenate([jnp.ones((n_ctx, d_ret), f32), cos_l], axis=0)
    sin_t = jnp.concatenate([jnp.zeros((n_ctx, d_ret), f32), sin_l], axis=0)
    return cos_t, sin_t


def _split_weight(w):
    hi = w.astype(bf16)
    lo = (w - hi.astype(f32)).astype(bf16)
    return hi, lo


def kernel(x, c, ctx, c_ctx, norm_w, w_mod, b_mod, w_in, ret_log_gamma, ret_norm_w, rwkv_shift_w, rwkv_w0, rwkv_w2, rwkv_a0, rwkv_a2, rwkv_k_k, rwkv_k_a, rwkv_r_k, rwkv_ln_w, rwkv_ln_b, w_out, final_norm_w):
    b, seq, d = x.shape
    n_ctx = ctx.shape[1]
    depth = w_in.shape[0]
    d_ret = ret_norm_w.shape[-1]
    d_rwkv = rwkv_w0.shape[-1]
    d_shift = rwkv_shift_w.shape[-1]

    cond = jnp.zeros((16, d), f32).at[:b].set(c.astype(f32)).at[b].set(c_ctx.astype(f32))
    mods = _modulation(cond, w_mod, b_mod)
    cos_t, sin_t = _rope_tables(n_ctx, seq, d_ret)
    xall = jnp.concatenate([ctx.astype(f32), x.astype(f32)], axis=1)

    out = None
    for layer in range(depth):
        m = mods[layer].reshape(16, 3, d)
        m = jnp.stack([m[:, 1], m[:, 0], m[:, 2]], axis=1)
        lat = m[:b]
        cx = jnp.broadcast_to(m[b][None], (b, 3, d))
        modv = jnp.pad(jnp.stack([cx, lat], axis=1), ((0, 0), (0, 0), (0, 5), (0, 0)))

        wih, wil = _split_weight(w_in[layer])
        qkv, rw, gates = _inproj(xall, modv, norm_w[layer][None], wih, wil, cos_t, sin_t,
                                 d_ret, d_shift, d_rwkv)
        ret = _retention(qkv, ret_log_gamma[layer], ret_norm_w[layer][None], n_ctx)
        rwo = _rwkv(rw, rwkv_shift_w[layer], rwkv_w0[layer], rwkv_w2[layer], rwkv_a0[layer],
                    rwkv_a2[layer], rwkv_k_k[layer][None], rwkv_k_a[layer][None],
                    rwkv_r_k[layer].reshape(1, d_rwkv), rwkv_ln_w[layer][None],
                    rwkv_ln_b[layer][None], n_ctx)
        woh, wol = _split_weight(w_out[layer])
        final = layer == depth - 1
        res = _outproj(ret, rwo, gates, xall, modv, woh, wol, final_norm_w[None], n_ctx, final)
        if final:
            out = res
        else:
            xall = res
    return out
```

```python
import functools
import math

import jax
import jax.numpy as jnp
from jax import lax
from jax.experimental import pallas as pl
from jax.experimental.pallas import tpu as pltpu

f32 = jnp.float32
bf16 = jnp.bfloat16

HEAD = 64
RET_C = 128
RW_C = 64
GRID_W = 64
ROPE_BASE = 10000.0
NORM_EPS = 1e-6
GN_EPS = 64e-5
ROW_TILE = 256
HALF = 256
NH = HALF // HEAD
VMEM_LIMIT = 56 * 1024 * 1024

NN = ((1,), (0,))
NT = ((1,), (1,))
TN = ((0,), (0,))

P_MOD = 3
P_IN = 3
P_OUT = 3
P_RET = 3
P_RW = 3


def _dg(a, b, dims):
    return lax.dot_general(a, b, (dims, ((), ())), preferred_element_type=f32)


def _split2(a):
    hi = a.astype(bf16)
    lo = (a - hi.astype(f32)).astype(bf16)
    return hi, lo


def _split3(a):
    hi = a.astype(bf16)
    r1 = a - hi.astype(f32)
    mid = r1.astype(bf16)
    lo = (r1 - mid.astype(f32)).astype(bf16)
    return hi, mid, lo


def _mm(a, b, dims=NN, passes=3):
    if passes == 1:
        return _dg(a.astype(bf16), b.astype(bf16), dims)
    ah, al = _split2(a)
    bh, bl = _split2(b)
    return _dg(ah, bh, dims) + (_dg(ah, bl, dims) + _dg(al, bh, dims))


def _mm_sel(sel, x):
    h, m, l = _split3(x)
    return _dg(sel, h, NN) + (_dg(sel, m, NN) + _dg(sel, l, NN))


def _group_sum(x, gmat):
    h, m, l = _split3(x)
    return _dg(h, gmat, NN) + (_dg(m, gmat, NN) + _dg(l, gmat, NN))


def _group_mat(n):
    ii = lax.broadcasted_iota(jnp.int32, (n, n), 0) // HEAD
    jj = lax.broadcasted_iota(jnp.int32, (n, n), 1) // HEAD
    return jnp.where(ii == jj, 1.0, 0.0).astype(bf16)


def _mod_kernel(c_ref, w_ref, b_ref, o_ref):
    cnd = c_ref[...]
    s = cnd * jax.nn.sigmoid(cnd)
    o_ref[0] = _mm(s, w_ref[0], NN, P_MOD) + b_ref[0]


def _modulation(cond, w_mod, b_mod):
    depth, d, d3 = w_mod.shape
    rows = cond.shape[0]
    tn = 1024
    return pl.pallas_call(
        _mod_kernel,
        grid=(depth, d3 // tn),
        in_specs=[pl.BlockSpec((rows, d), lambda l, j: (0, 0)),
                  pl.BlockSpec((1, d, tn), lambda l, j: (l, 0, j)),
                  pl.BlockSpec((1, 1, tn), lambda l, j: (l, 0, j))],
        out_specs=pl.BlockSpec((1, rows, tn), lambda l, j: (l, 0, j)),
        out_shape=jax.ShapeDtypeStruct((depth, rows, d3), f32),
        compiler_params=pltpu.CompilerParams(vmem_limit_bytes=VMEM_LIMIT),
        name="adaln_mod",
    )(cond, w_mod, b_mod.reshape(depth, 1, d3))


def _inproj_kernel(x_ref, mod_ref, nw_ref, wh_ref, wl_ref, cos_ref, sin_ref,
                   qkv_ref, rw_ref, g_ref, *, d_ret, d_shift):
    x = x_ref[0]
    ms = jnp.mean(x * x, axis=-1, keepdims=True)
    m = mod_ref[0, 0]
    h = x * lax.rsqrt(ms + NORM_EPS) * nw_ref[...]
    h = h * (1.0 + m[0:1]) + m[1:2]
    if P_IN == 1:
        hh = h.astype(bf16)
        hl = None
    else:
        hh, hl = _split2(h)

    def proj(lo, hi):
        wh = wh_ref[:, lo:hi]
        acc = _dg(hh, wh, NN)
        if hl is not None:
            acc = acc + (_dg(hh, wl_ref[:, lo:hi], NN) + _dg(hl, wh, NN))
        return acc

    cos = cos_ref[...]
    sin = sin_ref[...]
    lane = lax.broadcasted_iota(jnp.int32, cos.shape, 1)
    first = (lane % 32) < 16

    def rope(t):
        nl = t.shape[-1]
        swapped = jnp.where(first, pltpu.roll(t, nl - 16, 1), pltpu.roll(t, 16, 1))
        return t * cos + swapped * sin

    qkv_ref[0, :, 0:d_ret] = rope(proj(0, d_ret))
    qkv_ref[0, :, d_ret:2 * d_ret] = rope(proj(d_ret, 2 * d_ret)) * (HEAD ** -0.5)
    qkv_ref[0, :, 2 * d_ret:3 * d_ret] = proj(2 * d_ret, 3 * d_ret)
    g_ref[0, :, 0:d_ret] = proj(3 * d_ret, 4 * d_ret)
    o = 4 * d_ret
    rw_ref[0] = proj(o, o + d_shift)
    g_ref[0, :, d_ret:] = proj(o + d_shift, wh_ref.shape[1])


def _inproj(xall, modv, nw, wh, wl, cos_t, sin_t, d_ret, d_shift, d_rwkv):
    b, t, d = xall.shape
    n_tiles = t // ROW_TILE
    const = dict(pipeline_mode=pl.Buffered(1))
    kern = functools.partial(_inproj_kernel, d_ret=d_ret, d_shift=d_shift)
    return pl.pallas_call(
        kern,
        grid=(b, n_tiles),
        in_specs=[pl.BlockSpec((1, ROW_TILE, d), lambda i, j: (i, j, 0)),
                  pl.BlockSpec((1, 1, 8, d), lambda i, j: (i, jnp.minimum(j, 1), 0, 0)),
                  pl.BlockSpec((1, d), lambda i, j: (0, 0)),
                  pl.BlockSpec(wh.shape, lambda i, j: (0, 0), **const),
                  pl.BlockSpec(wl.shape, lambda i, j: (0, 0), **const),
                  pl.BlockSpec((ROW_TILE, d_ret), lambda i, j: (j, 0)),
                  pl.BlockSpec((ROW_TILE, d_ret), lambda i, j: (j, 0))],
        out_specs=[pl.BlockSpec((1, ROW_TILE, 3 * d_ret), lambda i, j: (i, j, 0)),
                   pl.BlockSpec((1, ROW_TILE, d_shift), lambda i, j: (i, j, 0)),
                   pl.BlockSpec((1, ROW_TILE, d_ret + d_rwkv), lambda i, j: (i, j, 0))],
        out_shape=[jax.ShapeDtypeStruct((b, t, 3 * d_ret), f32),
                   jax.ShapeDtypeStruct((b, t, d_shift), f32),
                   jax.ShapeDtypeStruct((b, t, d_ret + d_rwkv), f32)],
        compiler_params=pltpu.CompilerParams(vmem_limit_bytes=VMEM_LIMIT),
        name="in_proj",
    )(xall, modv, nw, wh, wl, cos_t, sin_t)


def _ret_kernel(lg_ref, q_ref, k_ref, v_ref, nw_ref, o_ref,
                s_ref, mask_ref, qd_ref, kd_ref, cd_ref, *, n_ctx_chunks, n_chunks):
    hh = pl.program_id(1)
    c = RET_C
    ii = lax.broadcasted_iota(jnp.int32, (c, c), 0)
    jj = lax.broadcasted_iota(jnp.int32, (c, c), 1)
    ri = lax.broadcasted_iota(jnp.int32, (c, HEAD), 0).astype(f32)
    for d in range(2):
        rel = ((ii - jj) if d == 0 else (jj - ii)).astype(f32)
        for h in range(NH):
            lg = lg_ref[d, hh * NH + h]
            mask_ref[d, h] = jnp.where(rel >= 0.0, jnp.exp(lg * jnp.maximum(rel, 0.0)), 0.0)
            qpow = (ri + 1.0) if d == 0 else (c - ri)
            kpow = (c - 1.0 - ri) if d == 0 else ri
            qd_ref[d, h] = jnp.exp(lg * qpow)
            kd_ref[d, h] = jnp.exp(lg * kpow)
            cd_ref[d, h] = jnp.exp(jnp.full((HEAD, HEAD), lg * c, f32))
    s_ref[...] = jnp.zeros(s_ref.shape, f32)
    o_ref[...] = jnp.zeros(o_ref.shape, f32)

    def step(s, carry):
        for d in range(2):
            if d == 0:
                ch = s
            else:
                ch = jnp.where(s < n_ctx_chunks, n_ctx_chunks - 1 - s, n_chunks + n_ctx_chunks - 1 - s)
            r0 = pl.multiple_of(ch * c, c)
            q = q_ref[0, pl.ds(r0, c), :]
            k = k_ref[0, pl.ds(r0, c), :]
            v = v_ref[0, pl.ds(r0, c), :]
            for h in range(NH):
                sl = slice(h * HEAD, (h + 1) * HEAD)
                qh, kh, vh = q[:, sl], k[:, sl], v[:, sl]
                st = s_ref[d, h]
                sc = _mm(qh, kh, NT, P_RET) * mask_ref[d, h]
                o = _mm(sc, vh, NN, P_RET) + _mm(qh * qd_ref[d, h], st, NN, P_RET)
                s_ref[d, h] = st * cd_ref[d, h] + _mm(kh * kd_ref[d, h], vh, TN, P_RET)
                o_ref[0, pl.ds(r0, c), sl] += o
        return carry

    lax.fori_loop(0, n_chunks, step, 0)

    gmat = _group_mat(HALF)
    nw = nw_ref[...]

    def norm_tile(i, carry):
        r0 = pl.multiple_of(i * ROW_TILE, ROW_TILE)
        o = o_ref[0, pl.ds(r0, ROW_TILE), :]
        ms = _group_sum(o * o, gmat) * (1.0 / HEAD)
        o_ref[0, pl.ds(r0, ROW_TILE), :] = o * lax.rsqrt(ms + NORM_EPS) * nw
        return carry

    lax.fori_loop(0, (n_chunks * c) // ROW_TILE, norm_tile, 0)


def _retention(qkv, log_gamma, norm_w, n_ctx):
    b, t, w3 = qkv.shape
    d_ret = w3 // 3
    nhalf = d_ret // HALF
    kern = functools.partial(_ret_kernel, n_ctx_chunks=n_ctx // RET_C, n_chunks=t // RET_C)
    blk = lambda off: pl.BlockSpec((1, t, HALF), lambda i, j, off=off: (i, 0, off * nhalf + j))
    return pl.pallas_call(
        kern,
        grid=(b, nhalf),
        in_specs=[pl.BlockSpec(memory_space=pltpu.SMEM),
                  blk(0), blk(1), blk(2),
                  pl.BlockSpec((1, HALF), lambda i, j: (0, j))],
        out_specs=pl.BlockSpec((1, t, HALF), lambda i, j: (i, 0, j)),
        out_shape=jax.ShapeDtypeStruct((b, t, d_ret), f32),
        scratch_shapes=[pltpu.VMEM((2, NH, HEAD, HEAD), f32),
                        pltpu.VMEM((2, NH, RET_C, RET_C), f32),
                        pltpu.VMEM((2, NH, RET_C, HEAD), f32),
                        pltpu.VMEM((2, NH, RET_C, HEAD), f32),
                        pltpu.VMEM((2, NH, HEAD, HEAD), f32)],
        compiler_params=pltpu.CompilerParams(vmem_limit_bytes=VMEM_LIMIT),
        name="retention",
    )(log_gamma, qkv, qkv, qkv, norm_w)


class _Mat:
    def __init__(self, x):
        self.x = x
        self._parts = None
        self._lhs = None
        self._rhs = None

    def parts(self):
        if self._parts is None:
            hi = self.x.astype(bf16).astype(f32)
            self._parts = (hi, self.x - hi)
        return self._parts

    def lhs(self):
        if self._lhs is None:
            hi, lo = self.parts()
            self._lhs = jnp.concatenate([hi, lo[:, :HEAD]], axis=1).astype(bf16)
        return self._lhs

    def rhs(self):
        if self._rhs is None:
            hi, lo = self.parts()
            self._rhs = jnp.concatenate([hi, lo, hi], axis=0).astype(bf16)
        return self._rhs

    def lhs_t(self):
        hi, lo = self.parts()
        return jnp.concatenate([hi, hi, lo], axis=0)[:, :HEAD].astype(bf16)

    def nt_lhs(self, first):
        hi, lo = self.parts()
        return jnp.concatenate([hi, jnp.where(first, lo, 0.0)], axis=1).astype(bf16)

    def nt_rhs(self, first):
        hi, lo = self.parts()
        b = jnp.concatenate([jnp.where(first, hi, lo), hi], axis=1).astype(bf16)
        return jnp.concatenate([b, b], axis=0)


def _mmd(a, b):
    return _dg(a.lhs(), b.rhs(), NN)


def _dup_heads(x, first):
    out = []
    for j in range(x.shape[1] // (2 * HEAD)):
        col = x[:, j * 2 * HEAD:(j + 1) * 2 * HEAD]
        rot = pltpu.roll(col, HEAD, 1)
        out.append(jnp.where(first, col, rot))
        out.append(jnp.where(first, rot, col))
    return out


def _rwkv_kernel(r_ref, k_ref, v_ref, xwa_ref, swr_ref, swk_ref, swv_ref, swx_ref,
                 w0_ref, w2_ref, a0_ref, a2_ref, kk_ref, ka_ref, rk_ref, lnw_ref, lnb_ref,
                 o_ref, rs_ref, ks_ref, vs_ref, kap_ref, bon_ref, xs_ref, s_ref,
                 *, n_ctx_chunks, n_chunks):
    c = RW_C
    t_rows = n_chunks * c
    gmat = _group_mat(HALF)
    rows = lax.broadcasted_iota(jnp.int32, (c, 1), 0)

    def conv(ref, sw_ref, ch, r0):
        x = ref[0, pl.ds(r0, c), :]
        p0 = pl.multiple_of(jnp.maximum(r0 - 8, 0), 8)
        n0 = pl.multiple_of(jnp.minimum(r0 + c, t_rows - 8), 8)
        prev8 = ref[0, pl.ds(p0, 8), :]
        next8 = ref[0, pl.ds(n0, 8), :]
        has_prev = jnp.logical_and(ch != 0, ch != n_ctx_chunks).astype(f32)
        has_next = jnp.logical_and(ch != n_ctx_chunks - 1, ch != n_chunks - 1).astype(f32)
        xd = jnp.where(rows == 0, prev8[7:8, :] * has_prev, pltpu.roll(x, 1, 0))
        xu = jnp.where(rows == c - 1, next8[0:1, :] * has_next, pltpu.roll(x, c - 1, 0))
        sw = sw_ref[...]
        return sw[0:1] * xd + sw[1:2] * x + sw[2:3] * xu

    def features(ch, carry):
        r0 = pl.multiple_of(ch * c, c)
        r = conv(r_ref, swr_ref, ch, r0)
        k = conv(k_ref, swk_ref, ch, r0)
        v = conv(v_ref, swv_ref, ch, r0)
        kk = k * kk_ref[...]
        nrm = jnp.sqrt(_group_sum(kk * kk, gmat))
        sl = pl.ds(r0, c)
        rs_ref[sl, :] = r
        ks_ref[sl, :] = k
        vs_ref[sl, :] = v
        kap_ref[sl, :] = kk / jnp.maximum(nrm, 1e-12)
        bon_ref[sl, :] = _group_sum(r * k * rk_ref[...], gmat) * v
        xs_ref[sl, :] = conv(xwa_ref, swx_ref, ch, r0)
        return carry

    lax.fori_loop(0, n_chunks, features, 0)

    ii = lax.broadcasted_iota(jnp.int32, (c, 2 * HEAD), 0)
    jj = lax.broadcasted_iota(jnp.int32, (c, 2 * HEAD), 1) % HEAD
    first = lax.broadcasted_iota(jnp.int32, (1, 2 * HEAD), 1) < HEAD
    eye = jnp.where(ii == jj, 1.0, 0.0)
    bd16 = (ii // 16) == (jj // 16)
    same32 = (ii // 32) == (jj // 32)
    off32 = jnp.logical_and(same32, jnp.logical_not(bd16))
    off64 = jnp.logical_not(same32)
    strict = (jj < ii, jj > ii)
    incl = (jj <= ii, jj >= ii)
    i2 = lax.broadcasted_iota(jnp.int32, (c, c), 0)
    j2 = lax.broadcasted_iota(jnp.int32, (c, c), 1)
    tri = (jnp.where(j2 <= i2, 1.0, 0.0).astype(bf16), jnp.where(j2 >= i2, 1.0, 0.0).astype(bf16))
    lora = xs_ref.shape[1] // 2
    wscale = -math.exp(-0.5)
    chains = [(d, h) for d in range(2) for h in range(NH)]

    s_ref[...] = jnp.zeros(s_ref.shape, f32)
    o_ref[...] = jnp.zeros(o_ref.shape, f32)

    def step(s, carry):
        sls = []
        feat = []
        for d in range(2):
            if d == 0:
                ch = s
            else:
                ch = jnp.where(s < n_ctx_chunks, n_ctx_chunks - 1 - s, n_chunks + n_ctx_chunks - 1 - s)
            r0 = pl.multiple_of(ch * c, c)
            sl = pl.ds(r0, c)
            sls.append(sl)
            xwa = xs_ref[sl, :]
            r = rs_ref[sl, :]
            k = ks_ref[sl, :]
            v = vs_ref[sl, :]
            kap = kap_ref[sl, :]
            u = w0_ref[d:d + 1, :] + _mm(jnp.tanh(xwa[:, :lora]), w2_ref[d], NN, P_RW)
            lw = wscale * jax.nn.sigmoid(u)
            a = jax.nn.sigmoid(a0_ref[d:d + 1, :] + _mm(xwa[:, lora:], a2_ref[d], NN, P_RW))
            kmod = k * (1.0 + (a - 1.0) * ka_ref[...])
            b = kap * a
            lp = _mm_sel(tri[d], lw)
            lpc = lp[c - 1:c, :] if d == 0 else lp[0:1, :]
            pinv = jnp.exp(-lp)
            ptail = jnp.exp(lpc - lp)
            pc = jnp.broadcast_to(jnp.exp(lpc), (8, lp.shape[1]))
            feat.append(dict(
                kq=_dup_heads(kap * jnp.exp(lp - lw), first),
                rq=_dup_heads(r * jnp.exp(lp), first),
                kd=_dup_heads(kmod * pinv, first),
                bd=_dup_heads(b * pinv, first),
                kh=_dup_heads(kmod * ptail, first),
                bh=_dup_heads(b * ptail, first),
                v=_dup_heads(v, first),
                pc=_dup_heads(pc, first)))

        st, kr_l, m_k, n_k, m_b, n_b, vmat = {}, {}, {}, {}, {}, {}, {}
        for (d, h) in chains:
            f = feat[d]
            kr = _Mat(jnp.concatenate([f["kq"][h], f["rq"][h]], axis=0)).nt_lhs(first)
            kr_l[d, h] = kr
            mk = _dg(kr, _Mat(f["kd"][h]).nt_rhs(first), NT)
            mb = _dg(kr, _Mat(f["bd"][h]).nt_rhs(first), NT)
            m_k[d, h] = _Mat(jnp.where(strict[d], mk[:c], 0.0))
            n_k[d, h] = _Mat(jnp.where(incl[d], mk[c:], 0.0))
            m_b[d, h] = jnp.where(strict[d], mb[:c], 0.0)
            n_b[d, h] = _Mat(jnp.where(incl[d], mb[c:], 0.0))
            st[d, h] = s_ref[d, h]
            vmat[d, h] = _Mat(f["v"][h])
        nd, p, n2, rhs, y0 = {}, {}, {}, {}, {}
        for ck in chains:
            nd[ck] = _Mat(jnp.where(bd16, -m_b[ck], 0.0))
            p[ck] = eye + nd[ck].x
            n2[ck] = _Mat(_mmd(nd[ck], nd[ck]))
            s_nt = _Mat(st[ck]).nt_rhs(first)
            rhs[ck] = _dg(kr_l[ck][:c], s_nt, NT) + _mmd(m_k[ck], vmat[ck])
            y0[ck] = _dg(kr_l[ck][c:], s_nt, NT) + _mmd(n_k[ck], vmat[ck])
        n4 = {}
        for ck in chains:
            p[ck] = p[ck] + _mmd(_Mat(p[ck]), n2[ck])
            n4[ck] = _Mat(_mmd(n2[ck], n2[ck]))
        n8 = {}
        for ck in chains:
            p[ck] = p[ck] + _mmd(_Mat(p[ck]), n4[ck])
            n8[ck] = _Mat(_mmd(n4[ck], n4[ck]))
        for ck in chains:
            p[ck] = p[ck] + _mmd(_Mat(p[ck]), n8[ck])
        for off in (off32, off64):
            tmp, pm = {}, {}
            for ck in chains:
                pm[ck] = _Mat(p[ck])
                tmp[ck] = _Mat(_mmd(pm[ck], _Mat(jnp.where(off, m_b[ck], 0.0))))
            for ck in chains:
                p[ck] = p[ck] - _mmd(tmp[ck], pm[ck])
        z = {}
        for ck in chains:
            z[ck] = _Mat(_mmd(_Mat(p[ck]), _Mat(rhs[ck])))
        y = {}
        for (d, h) in chains:
            ck = (d, h)
            f = feat[d]
            y[ck] = y0[ck] - _mmd(n_b[ck], z[ck])
            s_ref[d, h] = (st[ck] * f["pc"][h][0:1]
                           + _dg(vmat[ck].lhs_t(), _Mat(f["kh"][h]).rhs(), TN)) \
                - _dg(z[ck].lhs_t(), _Mat(f["bh"][h]).rhs(), TN)
        for d in range(2):
            for j in range(NH // 2):
                o_ref[0, sls[d], j * 2 * HEAD:(j + 1) * 2 * HEAD] += jnp.where(
                    first, y[d, 2 * j], y[d, 2 * j + 1])
        return carry

    lax.fori_loop(0, n_chunks, step, 0)

    lnw = lnw_ref[...]
    lnb = lnb_ref[...]

    def readout(i, carry):
        r0 = pl.multiple_of(i * ROW_TILE, ROW_TILE)
        sl = pl.ds(r0, ROW_TILE)
        y = o_ref[0, sl, :]
        mu = _group_sum(y, gmat) * (1.0 / HEAD)
        yc = y - mu
        var = _group_sum(yc * yc, gmat) * (1.0 / HEAD)
        o_ref[0, sl, :] = yc * lax.rsqrt(var + GN_EPS) * lnw + lnb + bon_ref[sl, :]
        return carry

    lax.fori_loop(0, t_rows // ROW_TILE, readout, 0)


def _rwkv(rw, shift_w, w0, w2, a0, a2, k_k, k_a, r_k, ln_w, ln_b, n_ctx):
    b, t, d_shift = rw.shape
    d_rwkv = w0.shape[-1]
    lora2 = d_shift - 3 * d_rwkv
    nhalf = d_rwkv // HALF
    xcol = (3 * d_rwkv) // lora2
    kern = functools.partial(_rwkv_kernel, n_ctx_chunks=n_ctx // RW_C, n_chunks=t // RW_C)
    seq = lambda off: pl.BlockSpec((1, t, HALF), lambda i, j, off=off: (i, 0, off * nhalf + j))
    swb = lambda off: pl.BlockSpec((3, HALF), lambda i, j, off=off: (0, off * nhalf + j))
    vec = lambda rows: pl.BlockSpec((rows, HALF), lambda i, j: (0, j))
    lor = pl.BlockSpec((2, lora2 // 2, HALF), lambda i, j: (0, 0, j))
    return pl.pallas_call(
        kern,
        grid=(b, nhalf),
        in_specs=[seq(0), seq(1), seq(2),
                  pl.BlockSpec((1, t, lora2), lambda i, j: (i, 0, xcol)),
                  swb(0), swb(1), swb(2),
                  pl.BlockSpec((3, lora2), lambda i, j: (0, xcol)),
                  vec(2), lor, vec(2), lor, vec(1), vec(1), vec(1), vec(1), vec(1)],
        out_specs=pl.BlockSpec((1, t, HALF), lambda i, j: (i, 0, j)),
        out_shape=jax.ShapeDtypeStruct((b, t, d_rwkv), f32),
        scratch_shapes=[pltpu.VMEM((t, HALF), f32)] * 5
        + [pltpu.VMEM((t, lora2), f32), pltpu.VMEM((2, NH, HEAD, 2 * HEAD), f32)],
        compiler_params=pltpu.CompilerParams(vmem_limit_bytes=VMEM_LIMIT),
        name="rwkv7",
    )(rw, rw, rw, rw, shift_w, shift_w, shift_w, shift_w,
      w0, w2, a0, a2, k_k, k_a, r_k, ln_w, ln_b)


def _outproj_kernel(ret_ref, rwo_ref, g_ref, x_ref, mod_ref, wh_ref, wl_ref, fw_ref, o_ref,
                    *, final):
    d_ret = ret_ref.shape[-1]
    g = g_ref[0]
    act = jnp.concatenate([ret_ref[0] * (g[:, :d_ret] * jax.nn.sigmoid(g[:, :d_ret])),
                           rwo_ref[0] * (g[:, d_ret:] * jax.nn.sigmoid(g[:, d_ret:]))], axis=1)
    if P_OUT == 1:
        mix = _dg(act.astype(bf16), wh_ref[...], NN)
    else:
        ah, al = _split2(act)
        wh = wh_ref[...]
        mix = _dg(ah, wh, NN) + (_dg(ah, wl_ref[...], NN) + _dg(al, wh, NN))
    xn = x_ref[0] + mod_ref[0, 0][2:3] * mix
    if final:
        ms = jnp.mean(xn * xn, axis=-1, keepdims=True)
        xn = xn * lax.rsqrt(ms + NORM_EPS) * fw_ref[...]
    o_ref[0] = xn


def _outproj(ret, rwo, gates, xall, modv, wh, wl, fw, n_ctx, final):
    b, t, d = xall.shape
    d_ret = ret.shape[-1]
    d_rw = rwo.shape[-1]
    skip = (n_ctx // ROW_TILE) if final else 0
    n_tiles = t // ROW_TILE - skip
    const = dict(pipeline_mode=pl.Buffered(1))
    row = lambda w: pl.BlockSpec((1, ROW_TILE, w), lambda i, j: (i, j + skip, 0))
    kern = functools.partial(_outproj_kernel, final=final)
    return pl.pallas_call(
        kern,
        grid=(b, n_tiles),
        in_specs=[row(d_ret), row(d_rw), row(d_ret + d_rw), row(d),
                  pl.BlockSpec((1, 1, 8, d), lambda i, j: (i, jnp.minimum(j + skip, 1), 0, 0)),
                  pl.BlockSpec(wh.shape, lambda i, j: (0, 0), **const),
                  pl.BlockSpec(wl.shape, lambda i, j: (0, 0), **const),
                  pl.BlockSpec((1, d), lambda i, j: (0, 0))],
        out_specs=pl.BlockSpec((1, ROW_TILE, d), lambda i, j: (i, j, 0)),
        out_shape=jax.ShapeDtypeStruct((b, n_tiles * ROW_TILE, d), f32),
        compiler_params=pltpu.CompilerParams(vmem_limit_bytes=VMEM_LIMIT),
        name="out_proj",
    )(ret, rwo, gates, xall, modv, wh, wl, fw)


def _rope_tables(n_ctx, seq, d_ret):
    nf = HEAD // 4
    inv = ROPE_BASE ** (-jnp.arange(nf, dtype=f32) / nf)
    pos = jnp.arange(seq)
    row_pos = (pos // GRID_W).astype(f32)
    col_pos = (pos % GRID_W).astype(f32)
    ang_r = row_pos[:, None] * inv[None, :]
    ang_c = col_pos[:, None] * inv[None, :]
    cos_h = jnp.concatenate([jnp.cos(ang_r)] * 2 + [jnp.cos(ang_c)] * 2, axis=-1)
    sin_h = jnp.concatenate([-jnp.sin(ang_r), jnp.sin(ang_r), -jnp.sin(ang_c), jnp.sin(ang_c)], axis=-1)
    reps = d_ret // HEAD
    cos_l = jnp.tile(cos_h, (1, reps))
    sin_l = jnp.tile(sin_h, (1, reps))
    cos_t = jnp.concatenate([jnp.ones((n_ctx, d_ret), f32), cos_l], axis=0)
    sin_t = jnp.concatenate([jnp.zeros((n_ctx, d_ret), f32), sin_l], axis=0)
    return cos_t, sin_t


def _split_weight(w):
    hi = w.astype(bf16)
    lo = (w - hi.astype(f32)).astype(bf16)
    return hi, lo


def kernel(x, c, ctx, c_ctx, norm_w, w_mod, b_mod, w_in, ret_log_gamma, ret_norm_w, rwkv_shift_w, rwkv_w0, rwkv_w2, rwkv_a0, rwkv_a2, rwkv_k_k, rwkv_k_a, rwkv_r_k, rwkv_ln_w, rwkv_ln_b, w_out, final_norm_w):
    b, seq, d = x.shape
    n_ctx = ctx.shape[1]
    depth = w_in.shape[0]
    d_ret = ret_norm_w.shape[-1]
    d_rwkv = rwkv_w0.shape[-1]
    d_shift = rwkv_shift_w.shape[-1]

    cond = jnp.zeros((16, d), f32).at[:b].set(c.astype(f32)).at[b].set(c_ctx.astype(f32))
    mods = _modulation(cond, w_mod, b_mod)
    cos_t, sin_t = _rope_tables(n_ctx, seq, d_ret)
    xall = jnp.concatenate([ctx.astype(f32), x.astype(f32)], axis=1)

    out = None
    for layer in range(depth):
        m = mods[layer].reshape(16, 3, d)
        m = jnp.stack([m[:, 1], m[:, 0], m[:, 2]], axis=1)
        lat = m[:b]
        cx = jnp.broadcast_to(m[b][None], (b, 3, d))
        modv = jnp.pad(jnp.stack([cx, lat], axis=1), ((0, 0), (0, 0), (0, 5), (0, 0)))

        wih, wil = _split_weight(w_in[layer])
        qkv, rw, gates = _inproj(xall, modv, norm_w[layer][None], wih, wil, cos_t, sin_t,
                                 d_ret, d_shift, d_rwkv)
        ret = _retention(qkv, ret_log_gamma[layer], ret_norm_w[layer][None], n_ctx)
        rwo = _rwkv(rw, rwkv_shift_w[layer], rwkv_w0[layer], rwkv_w2[layer], rwkv_a0[layer],
                    rwkv_a2[layer], rwkv_k_k[layer][None], rwkv_k_a[layer][None],
                    rwkv_r_k[layer].reshape(1, d_rwkv), rwkv_ln_w[layer][None],
                    rwkv_ln_b[layer][None], n_ctx)
        woh, wol = _split_weight(w_out[layer])
        final = layer == depth - 1
        res = _outproj(ret, rwo, gates, xall, modv, woh, wol, final_norm_w[None], n_ctx, final)
        if final:
            out = res
        else:
            xall = res
    return out
```

```python
import functools
import math

import jax
import jax.numpy as jnp
from jax import lax
from jax.experimental import pallas as pl
from jax.experimental.pallas import tpu as pltpu

f32 = jnp.float32
bf16 = jnp.bfloat16

HEAD = 64
PAIR = 2 * HEAD
RET_C = 128
RW_C = 64
RW_UNROLL = 2
GRID_W = 64
ROPE_BASE = 10000.0
NORM_EPS = 1e-6
GN_EPS = 64e-5
ROW_TILE = 256
HALF = 256
NP = HALF // PAIR
VMEM_LIMIT = 56 * 1024 * 1024

NN = ((1,), (0,))
NT = ((1,), (1,))
TN = ((0,), (0,))

P_MOD = 3
P_IN = 1
P_OUT = 1
P_RET = 1
P_RW = 1


def _dg(a, b, dims):
    return lax.dot_general(a, b, (dims, ((), ())), preferred_element_type=f32)


def _split2(a):
    hi = a.astype(bf16)
    lo = (a - hi.astype(f32)).astype(bf16)
    return hi, lo


def _split3(a):
    hi = a.astype(bf16)
    r1 = a - hi.astype(f32)
    mid = r1.astype(bf16)
    lo = (r1 - mid.astype(f32)).astype(bf16)
    return hi, mid, lo


def _mm(a, b, dims=NN, passes=3):
    if passes == 1:
        return _dg(a.astype(bf16), b.astype(bf16), dims)
    ah, al = _split2(a)
    bh, bl = _split2(b)
    return _dg(ah, bh, dims) + (_dg(ah, bl, dims) + _dg(al, bh, dims))


def _mm_sel(sel, x):
    h, m, l = _split3(x)
    return _dg(sel, h, NN) + (_dg(sel, m, NN) + _dg(sel, l, NN))


def _group_sum(x, gmat):
    h, m, l = _split3(x)
    return _dg(h, gmat, NN) + (_dg(m, gmat, NN) + _dg(l, gmat, NN))


def _group_mat(n):
    ii = lax.broadcasted_iota(jnp.int32, (n, n), 0) // HEAD
    jj = lax.broadcasted_iota(jnp.int32, (n, n), 1) // HEAD
    return jnp.where(ii == jj, 1.0, 0.0).astype(bf16)


def _first_half():
    return lax.broadcasted_iota(jnp.int32, (1, PAIR), 1) < HEAD


def _bd(x, first):
    return jnp.concatenate([jnp.where(first, x, 0.0), jnp.where(first, 0.0, x)], axis=0)


def _heads_to_rows(x):
    return jnp.concatenate([x[:, :HEAD], pltpu.roll(x, HEAD, 1)[:, :HEAD]], axis=0)


def _mod_kernel(c_ref, w_ref, b_ref, o_ref):
    cnd = c_ref[...]
    s = cnd * jax.nn.sigmoid(cnd)
    o_ref[0] = _mm(s, w_ref[0], NN, P_MOD) + b_ref[0]


def _modulation(cond, w_mod, b_mod):
    depth, d, d3 = w_mod.shape
    rows = cond.shape[0]
    tn = 1024
    return pl.pallas_call(
        _mod_kernel,
        grid=(depth, d3 // tn),
        in_specs=[pl.BlockSpec((rows, d), lambda l, j: (0, 0)),
                  pl.BlockSpec((1, d, tn), lambda l, j: (l, 0, j)),
                  pl.BlockSpec((1, 1, tn), lambda l, j: (l, 0, j))],
        out_specs=pl.BlockSpec((1, rows, tn), lambda l, j: (l, 0, j)),
        out_shape=jax.ShapeDtypeStruct((depth, rows, d3), f32),
        compiler_params=pltpu.CompilerParams(vmem_limit_bytes=VMEM_LIMIT),
        name="adaln_mod",
    )(cond, w_mod, b_mod.reshape(depth, 1, d3))


def _inproj_kernel(x_ref, mod_ref, nw_ref, wh_ref, wl_ref, cos_ref, sin_ref,
                   qkv_ref, rw_ref, g_ref, *, d_ret, d_shift):
    x = x_ref[0]
    ms = jnp.mean(x * x, axis=-1, keepdims=True)
    m = mod_ref[0, 0]
    h = x * lax.rsqrt(ms + NORM_EPS) * nw_ref[...]
    h = h * (1.0 + m[0:1]) + m[1:2]
    if P_IN == 1:
        hh = h.astype(bf16)
        hl = None
    else:
        hh, hl = _split2(h)

    def proj(lo, hi):
        wh = wh_ref[:, lo:hi]
        acc = _dg(hh, wh, NN)
        if hl is not None:
            acc = acc + (_dg(hh, wl_ref[:, lo:hi], NN) + _dg(hl, wh, NN))
        return acc

    cos = cos_ref[...]
    sin = sin_ref[...]
    lane = lax.broadcasted_iota(jnp.int32, cos.shape, 1)
    first = (lane % 32) < 16

    def rope(t):
        nl = t.shape[-1]
        swapped = jnp.where(first, pltpu.roll(t, nl - 16, 1), pltpu.roll(t, 16, 1))
        return t * cos + swapped * sin

    qkv_ref[0, :, 0:d_ret] = rope(proj(0, d_ret))
    qkv_ref[0, :, d_ret:2 * d_ret] = rope(proj(d_ret, 2 * d_ret)) * (HEAD ** -0.5)
    qkv_ref[0, :, 2 * d_ret:3 * d_ret] = proj(2 * d_ret, 3 * d_ret)
    g_ref[0, :, 0:d_ret] = proj(3 * d_ret, 4 * d_ret)
    o = 4 * d_ret
    rw_ref[0] = proj(o, o + d_shift)
    g_ref[0, :, d_ret:] = proj(o + d_shift, wh_ref.shape[1])


def _inproj(xall, modv, nw, wh, wl, cos_t, sin_t, d_ret, d_shift, d_rwkv):
    b, t, d = xall.shape
    n_tiles = t // ROW_TILE
    const = dict(pipeline_mode=pl.Buffered(1))
    kern = functools.partial(_inproj_kernel, d_ret=d_ret, d_shift=d_shift)
    return pl.pallas_call(
        kern,
        grid=(b, n_tiles),
        in_specs=[pl.BlockSpec((1, ROW_TILE, d), lambda i, j: (i, j, 0)),
                  pl.BlockSpec((1, 1, 8, d), lambda i, j: (i, jnp.minimum(j, 1), 0, 0)),
                  pl.BlockSpec((1, d), lambda i, j: (0, 0)),
                  pl.BlockSpec(wh.shape, lambda i, j: (0, 0), **const),
                  pl.BlockSpec(wl.shape, lambda i, j: (0, 0), **const),
                  pl.BlockSpec((ROW_TILE, d_ret), lambda i, j: (j, 0)),
                  pl.BlockSpec((ROW_TILE, d_ret), lambda i, j: (j, 0))],
        out_specs=[pl.BlockSpec((1, ROW_TILE, 3 * d_ret), lambda i, j: (i, j, 0)),
                   pl.BlockSpec((1, ROW_TILE, d_shift), lambda i, j: (i, j, 0)),
                   pl.BlockSpec((1, ROW_TILE, d_ret + d_rwkv), lambda i, j: (i, j, 0))],
        out_shape=[jax.ShapeDtypeStruct((b, t, 3 * d_ret), f32),
                   jax.ShapeDtypeStruct((b, t, d_shift), f32),
                   jax.ShapeDtypeStruct((b, t, d_ret + d_rwkv), f32)],
        compiler_params=pltpu.CompilerParams(vmem_limit_bytes=VMEM_LIMIT),
        name="in_proj",
    )(xall, modv, nw, wh, wl, cos_t, sin_t)


def _scan_chunk(idx, forward, n_ctx_chunks, n_chunks):
    if forward:
        return idx
    return jnp.where(idx < n_ctx_chunks, n_ctx_chunks - 1 - idx, n_chunks + n_ctx_chunks - 1 - idx)


def _ret_kernel(lg_ref, q_ref, k_ref, v_ref, nw_ref, o_ref,
                s_ref, mask_ref, qd_ref, kd_ref, cd_ref, *, n_ctx_chunks, n_chunks):
    hh = pl.program_id(1)
    c = RET_C
    first = _first_half()
    ii = lax.broadcasted_iota(jnp.int32, (c, 2 * c), 0)
    nn = lax.broadcasted_iota(jnp.int32, (c, 2 * c), 1)
    ri = lax.broadcasted_iota(jnp.int32, (c, PAIR), 0).astype(f32)
    for d in range(2):
        rel = ((ii - nn % c) if d == 0 else (nn % c - ii)).astype(f32)
        for j in range(NP):
            lg0 = lg_ref[d, hh * 2 * NP + 2 * j]
            lg1 = lg_ref[d, hh * 2 * NP + 2 * j + 1]
            lgm = jnp.where(nn < c, lg0, lg1)
            lgv = jnp.where(first, lg0, lg1)
            mask_ref[d, j] = jnp.where(rel >= 0.0, jnp.exp(lgm * jnp.maximum(rel, 0.0)), 0.0)
            qpow = (ri + 1.0) if d == 0 else (c - ri)
            kpow = (c - 1.0 - ri) if d == 0 else ri
            qd_ref[d, j] = jnp.exp(lgv * qpow)
            kd_ref[d, j] = jnp.exp(lgv * kpow)
            cd_ref[d, j] = jnp.exp(jnp.broadcast_to(lgv * c, (HEAD, PAIR)))
    s_ref[...] = jnp.zeros(s_ref.shape, f32)
    o_ref[...] = jnp.zeros(o_ref.shape, f32)
    chains = [(d, j) for d in range(2) for j in range(NP)]

    def step(s, carry):
        rows, qp, kp, vp = {}, {}, {}, {}
        for d in range(2):
            ch = _scan_chunk(s, d == 0, n_ctx_chunks, n_chunks)
            r0 = pl.multiple_of(ch * c, c)
            rows[d] = pl.ds(r0, c)
            q = q_ref[0, rows[d], :]
            k = k_ref[0, rows[d], :]
            v = v_ref[0, rows[d], :]
            for j in range(NP):
                col = slice(j * PAIR, (j + 1) * PAIR)
                qp[d, j], kp[d, j], vp[d, j] = q[:, col], k[:, col], v[:, col]
        sc, st, vb = {}, {}, {}
        for ck in chains:
            sc[ck] = _mm(qp[ck], _bd(kp[ck], first), NT, P_RET) * mask_ref[ck[0], ck[1]]
            st[ck] = s_ref[ck[0], ck[1]]
            vb[ck] = _bd(vp[ck], first)
        for (d, j) in chains:
            ck = (d, j)
            o = _mm(sc[ck], vb[ck], NN, P_RET) + _mm(qp[ck] * qd_ref[d, j], _bd(st[ck], first), NN, P_RET)
            s_ref[d, j] = st[ck] * cd_ref[d, j] + _mm(_heads_to_rows(kp[ck] * kd_ref[d, j]), vb[ck], TN, P_RET)
            o_ref[0, rows[d], j * PAIR:(j + 1) * PAIR] += o
        return carry

    lax.fori_loop(0, n_chunks, step, 0)

    gmat = _group_mat(HALF)
    nw = nw_ref[...]

    def norm_tile(i, carry):
        r0 = pl.multiple_of(i * ROW_TILE, ROW_TILE)
        o = o_ref[0, pl.ds(r0, ROW_TILE), :]
        ms = _group_sum(o * o, gmat) * (1.0 / HEAD)
        o_ref[0, pl.ds(r0, ROW_TILE), :] = o * lax.rsqrt(ms + NORM_EPS) * nw
        return carry

    lax.fori_loop(0, (n_chunks * c) // ROW_TILE, norm_tile, 0)


def _retention(qkv, log_gamma, norm_w, n_ctx):
    b, t, w3 = qkv.shape
    d_ret = w3 // 3
    nhalf = d_ret // HALF
    kern = functools.partial(_ret_kernel, n_ctx_chunks=n_ctx // RET_C, n_chunks=t // RET_C)
    blk = lambda off: pl.BlockSpec((1, t, HALF), lambda i, j, off=off: (i, 0, off * nhalf + j))
    return pl.pallas_call(
        kern,
        grid=(b, nhalf),
        in_specs=[pl.BlockSpec(memory_space=pltpu.SMEM),
                  blk(0), blk(1), blk(2),
                  pl.BlockSpec((1, HALF), lambda i, j: (0, j))],
        out_specs=pl.BlockSpec((1, t, HALF), lambda i, j: (i, 0, j)),
        out_shape=jax.ShapeDtypeStruct((b, t, d_ret), f32),
        scratch_shapes=[pltpu.VMEM((2, NP, HEAD, PAIR), f32),
                        pltpu.VMEM((2, NP, RET_C, 2 * RET_C), f32),
                        pltpu.VMEM((2, NP, RET_C, PAIR), f32),
                        pltpu.VMEM((2, NP, RET_C, PAIR), f32),
                        pltpu.VMEM((2, NP, HEAD, PAIR), f32)],
        compiler_params=pltpu.CompilerParams(vmem_limit_bytes=VMEM_LIMIT),
        name="retention",
    )(log_gamma, qkv, qkv, qkv, norm_w)


class _Pair:
    def __init__(self, x, first):
        self.x = x
        self.first = first
        self._parts = None
        self._lhs = None
        self._bd = None

    def parts(self):
        if self._parts is None:
            hi = self.x.astype(bf16).astype(f32)
            self._parts = (hi, self.x - hi)
        return self._parts

    def lhs(self):
        if self._lhs is None:
            if P_RW == 1:
                self._lhs = self.x.astype(bf16)
            else:
                hi, lo = self.parts()
                self._lhs = jnp.concatenate([hi, hi, lo], axis=1).astype(bf16)
        return self._lhs

    def _bd_parts(self):
        if self._bd is None:
            if P_RW == 1:
                self._bd = (_bd(self.x, self.first).astype(bf16),)
            else:
                hi, lo = self.parts()
                bh = _bd(hi, self.first).astype(bf16)
                self._bd = (bh, _bd(lo, self.first).astype(bf16), bh)
        return self._bd

    def rhs(self):
        return jnp.concatenate(self._bd_parts(), axis=0)

    def rhs_nt(self):
        return jnp.concatenate(self._bd_parts(), axis=1)

    def lhs_rows(self):
        if P_RW == 1:
            return _heads_to_rows(self.x).astype(bf16)
        hi, lo = self.parts()
        th = _heads_to_rows(hi)
        return jnp.concatenate([th, th, _heads_to_rows(lo)], axis=0).astype(bf16)


def _mmp(a, b):
    return _dg(a.lhs(), b.rhs(), NN)


def _rwkv_kernel(r_ref, k_ref, v_ref, xwa_ref, swr_ref, swk_ref, swv_ref, swx_ref,
                 w0_ref, w2_ref, a0_ref, a2_ref, kk_ref, ka_ref, rk_ref, lnw_ref, lnb_ref,
                 o_ref, rs_ref, ks_ref, vs_ref, kap_ref, bon_ref, xs_ref, s_ref,
                 *, n_ctx_chunks, n_chunks):
    c = RW_C
    t_rows = n_chunks * c
    gmat = _group_mat(HALF)
    rows = lax.broadcasted_iota(jnp.int32, (c, 1), 0)

    def conv(ref, sw_ref, ch, r0):
        x = ref[0, pl.ds(r0, c), :]
        p0 = pl.multiple_of(jnp.maximum(r0 - 8, 0), 8)
        n0 = pl.multiple_of(jnp.minimum(r0 + c, t_rows - 8), 8)
        prev8 = ref[0, pl.ds(p0, 8), :]
        next8 = ref[0, pl.ds(n0, 8), :]
        has_prev = jnp.logical_and(ch != 0, ch != n_ctx_chunks).astype(f32)
        has_next = jnp.logical_and(ch != n_ctx_chunks - 1, ch != n_chunks - 1).astype(f32)
        xd = jnp.where(rows == 0, prev8[7:8, :] * has_prev, pltpu.roll(x, 1, 0))
        xu = jnp.where(rows == c - 1, next8[0:1, :] * has_next, pltpu.roll(x, c - 1, 0))
        sw = sw_ref[...]
        return sw[0:1] * xd + sw[1:2] * x + sw[2:3] * xu

    def features(ch, carry):
        r0 = pl.multiple_of(ch * c, c)
        r = conv(r_ref, swr_ref, ch, r0)
        k = conv(k_ref, swk_ref, ch, r0)
        v = conv(v_ref, swv_ref, ch, r0)
        kk = k * kk_ref[...]
        nrm = jnp.sqrt(_group_sum(kk * kk, gmat))
        sl = pl.ds(r0, c)
        rs_ref[sl, :] = r
        ks_ref[sl, :] = k
        vs_ref[sl, :] = v
        kap_ref[sl, :] = kk / jnp.maximum(nrm, 1e-12)
        bon_ref[sl, :] = _group_sum(r * k * rk_ref[...], gmat) * v
        xs_ref[sl, :] = conv(xwa_ref, swx_ref, ch, r0)
        return carry

    lax.fori_loop(0, n_chunks, features, 0)

    first = _first_half()
    ii = lax.broadcasted_iota(jnp.int32, (c, PAIR), 0)
    jj = lax.broadcasted_iota(jnp.int32, (c, PAIR), 1) % HEAD
    eye = jnp.where(ii == jj, 1.0, 0.0)
    bd16 = (ii // 16) == (jj // 16)
    same32 = (ii // 32) == (jj // 32)
    off32 = jnp.logical_and(same32, jnp.logical_not(bd16))
    off64 = jnp.logical_not(same32)
    strict = (jj < ii, jj > ii)
    incl = (jj <= ii, jj >= ii)
    both = tuple(jnp.concatenate([strict[d], incl[d]], axis=0) for d in range(2))
    i2 = lax.broadcasted_iota(jnp.int32, (c, c), 0)
    j2 = lax.broadcasted_iota(jnp.int32, (c, c), 1)
    tri = (jnp.where(j2 <= i2, 1.0, 0.0).astype(bf16), jnp.where(j2 >= i2, 1.0, 0.0).astype(bf16))
    lora = xs_ref.shape[1] // 2
    wscale = -math.exp(-0.5)
    mk = lambda x: _Pair(x, first)

    s_ref[...] = jnp.zeros(s_ref.shape, f32)
    o_ref[...] = jnp.zeros(o_ref.shape, f32)

    def step(s, carry):
        feat = {}
        for d in range(2):
            for u in range(RW_UNROLL):
                ch = _scan_chunk(s * RW_UNROLL + u, d == 0, n_ctx_chunks, n_chunks)
                r0 = pl.multiple_of(ch * c, c)
                sl = pl.ds(r0, c)
                xwa = xs_ref[sl, :]
                r = rs_ref[sl, :]
                k = ks_ref[sl, :]
                kap = kap_ref[sl, :]
                u_in = w0_ref[d:d + 1, :] + _mm(jnp.tanh(xwa[:, :lora]), w2_ref[d], NN, 3)
                lw = wscale * jax.nn.sigmoid(u_in)
                a = jax.nn.sigmoid(a0_ref[d:d + 1, :] + _mm(xwa[:, lora:], a2_ref[d], NN, 3))
                kmod = k * (1.0 + (a - 1.0) * ka_ref[...])
                b = kap * a
                lp = _mm_sel(tri[d], lw)
                lpc = lp[c - 1:c, :] if d == 0 else lp[0:1, :]
                pinv = jnp.exp(-lp)
                ptail = jnp.exp(lpc - lp)
                feat[d, u] = dict(
                    sl=sl, pc=jnp.exp(lpc), v=vs_ref[sl, :],
                    kr=jnp.concatenate([kap * jnp.exp(lp - lw), r * jnp.exp(lp)], axis=0),
                    kd=kmod * pinv, bd=b * pinv, kh=kmod * ptail, bh=b * ptail)

        chains = [(d, u, j) for d in range(2) for u in range(RW_UNROLL) for j in range(NP)]
        col = lambda j: slice(j * PAIR, (j + 1) * PAIR)
        kr, mkv, m_b, n_b, vp = {}, {}, {}, {}, {}
        for ck in chains:
            d, u, j = ck
            f = feat[d, u]
            kr[ck] = mk(f["kr"][:, col(j)])
            mk_ = _dg(kr[ck].lhs(), mk(f["kd"][:, col(j)]).rhs_nt(), NT)
            mb_ = _dg(kr[ck].lhs(), mk(f["bd"][:, col(j)]).rhs_nt(), NT)
            vp[ck] = mk(f["v"][:, col(j)])
            mkv[ck] = _mmp(mk(jnp.where(both[d], mk_, 0.0)), vp[ck])
            m_b[ck] = jnp.where(strict[d], mb_[:c], 0.0)
            n_b[ck] = mk(jnp.where(incl[d], mb_[c:], 0.0))
        p, npow = {}, {}
        for ck in chains:
            nd = mk(jnp.where(bd16, -m_b[ck], 0.0))
            p[ck] = eye + nd.x
            npow[ck] = mk(_mmp(nd, nd))
        for level in range(2):
            for ck in chains:
                both_ = _mmp(mk(jnp.concatenate([p[ck], npow[ck].x], axis=0)), npow[ck])
                p[ck] = p[ck] + both_[:c]
                npow[ck] = mk(both_[c:])
        for ck in chains:
            p[ck] = p[ck] + _mmp(mk(p[ck]), npow[ck])
        for off in (off32, off64):
            tmp, pm = {}, {}
            for ck in chains:
                pm[ck] = mk(p[ck])
                tmp[ck] = mk(_mmp(pm[ck], mk(jnp.where(off, m_b[ck], 0.0))))
            for ck in chains:
                p[ck] = p[ck] - _mmp(tmp[ck], pm[ck])
        tinv = {ck: mk(p[ck]) for ck in chains}
        for u in range(RW_UNROLL):
            sub = [(d, u, j) for d in range(2) for j in range(NP)]
            st, ks = {}, {}
            for ck in sub:
                st[ck] = s_ref[ck[0], ck[2]]
                ks[ck] = _dg(kr[ck].lhs(), mk(st[ck]).rhs_nt(), NT) + mkv[ck]
            z = {ck: mk(_mmp(tinv[ck], mk(ks[ck][:c]))) for ck in sub}
            for ck in sub:
                d, _, j = ck
                f = feat[d, u]
                y = ks[ck][c:] - _mmp(n_b[ck], z[ck])
                s_ref[d, j] = (st[ck] * f["pc"][:, col(j)]
                               + _dg(vp[ck].lhs_rows(), mk(f["kh"][:, col(j)]).rhs(), TN)) \
                    - _dg(z[ck].lhs_rows(), mk(f["bh"][:, col(j)]).rhs(), TN)
                o_ref[0, f["sl"], col(j)] += y
        return carry

    lax.fori_loop(0, n_chunks // RW_UNROLL, step, 0)

    lnw = lnw_ref[...]
    lnb = lnb_ref[...]

    def readout(i, carry):
        r0 = pl.multiple_of(i * ROW_TILE, ROW_TILE)
        sl = pl.ds(r0, ROW_TILE)
        y = o_ref[0, sl, :]
        mu = _group_sum(y, gmat) * (1.0 / HEAD)
        yc = y - mu
        var = _group_sum(yc * yc, gmat) * (1.0 / HEAD)
        o_ref[0, sl, :] = yc * lax.rsqrt(var + GN_EPS) * lnw + lnb + bon_ref[sl, :]
        return carry

    lax.fori_loop(0, t_rows // ROW_TILE, readout, 0)


def _rwkv(rw, shift_w, w0, w2, a0, a2, k_k, k_a, r_k, ln_w, ln_b, n_ctx):
    b, t, d_shift = rw.shape
    d_rwkv = w0.shape[-1]
    lora2 = d_shift - 3 * d_rwkv
    nhalf = d_rwkv // HALF
    xcol = (3 * d_rwkv) // lora2
    assert (n_ctx // RW_C) % RW_UNROLL == 0 and (t // RW_C) % RW_UNROLL == 0
    kern = functools.partial(_rwkv_kernel, n_ctx_chunks=n_ctx // RW_C, n_chunks=t // RW_C)
    seq = lambda off: pl.BlockSpec((1, t, HALF), lambda i, j, off=off: (i, 0, off * nhalf + j))
    swb = lambda off: pl.BlockSpec((3, HALF), lambda i, j, off=off: (0, off * nhalf + j))
    vec = lambda rows: pl.BlockSpec((rows, HALF), lambda i, j: (0, j))
    lor = pl.BlockSpec((2, lora2 // 2, HALF), lambda i, j: (0, 0, j))
    return pl.pallas_call(
        kern,
        grid=(b, nhalf),
        in_specs=[seq(0), seq(1), seq(2),
                  pl.BlockSpec((1, t, lora2), lambda i, j: (i, 0, xcol)),
                  swb(0), swb(1), swb(2),
                  pl.BlockSpec((3, lora2), lambda i, j: (0, xcol)),
                  vec(2), lor, vec(2), lor, vec(1), vec(1), vec(1), vec(1), vec(1)],
        out_specs=pl.BlockSpec((1, t, HALF), lambda i, j: (i, 0, j)),
        out_shape=jax.ShapeDtypeStruct((b, t, d_rwkv), f32),
        scratch_shapes=[pltpu.VMEM((t, HALF), f32)] * 5
        + [pltpu.VMEM((t, lora2), f32), pltpu.VMEM((2, NP, HEAD, PAIR), f32)],
        compiler_params=pltpu.CompilerParams(vmem_limit_bytes=VMEM_LIMIT),
        name="rwkv7",
    )(rw, rw, rw, rw, shift_w, shift_w, shift_w, shift_w,
      w0, w2, a0, a2, k_k, k_a, r_k, ln_w, ln_b)


def _outproj_kernel(ret_ref, rwo_ref, g_ref, x_ref, mod_ref, wh_ref, wl_ref, fw_ref, o_ref,
                    *, final):
    d_ret = ret_ref.shape[-1]
    g = g_ref[0]
    act = jnp.concatenate([ret_ref[0] * (g[:, :d_ret] * jax.nn.sigmoid(g[:, :d_ret])),
                           rwo_ref[0] * (g[:, d_ret:] * jax.nn.sigmoid(g[:, d_ret:]))], axis=1)
    if P_OUT == 1:
        mix = _dg(act.astype(bf16), wh_ref[...], NN)
    else:
        ah, al = _split2(act)
        wh = wh_ref[...]
        mix = _dg(ah, wh, NN) + (_dg(ah, wl_ref[...], NN) + _dg(al, wh, NN))
    xn = x_ref[0] + mod_ref[0, 0][2:3] * mix
    if final:
        ms = jnp.mean(xn * xn, axis=-1, keepdims=True)
        xn = xn * lax.rsqrt(ms + NORM_EPS) * fw_ref[...]
    o_ref[0] = xn


def _outproj(ret, rwo, gates, xall, modv, wh, wl, fw, n_ctx, final):
    b, t, d = xall.shape
    d_ret = ret.shape[-1]
    d_rw = rwo.shape[-1]
    skip = (n_ctx // ROW_TILE) if final else 0
    n_tiles = t // ROW_TILE - skip
    const = dict(pipeline_mode=pl.Buffered(1))
    row = lambda w: pl.BlockSpec((1, ROW_TILE, w), lambda i, j: (i, j + skip, 0))
    kern = functools.partial(_outproj_kernel, final=final)
    return pl.pallas_call(
        kern,
        grid=(b, n_tiles),
        in_specs=[row(d_ret), row(d_rw), row(d_ret + d_rw), row(d),
                  pl.BlockSpec((1, 1, 8, d), lambda i, j: (i, jnp.minimum(j + skip, 1), 0, 0)),
                  pl.BlockSpec(wh.shape, lambda i, j: (0, 0), **const),
                  pl.BlockSpec(wl.shape, lambda i, j: (0, 0), **const),
                  pl.BlockSpec((1, d), lambda i, j: (0, 0))],
        out_specs=pl.BlockSpec((1, ROW_TILE, d), lambda i, j: (i, j, 0)),
        out_shape=jax.ShapeDtypeStruct((b, n_tiles * ROW_TILE, d), f32),
        compiler_params=pltpu.CompilerParams(vmem_limit_bytes=VMEM_LIMIT),
        name="out_proj",
    )(ret, rwo, gates, xall, modv, wh, wl, fw)


def _rope_tables(n_ctx, seq, d_ret):
    nf = HEAD // 4
    inv = ROPE_BASE ** (-jnp.arange(nf, dtype=f32) / nf)
    pos = jnp.arange(seq)
    row_pos = (pos // GRID_W).astype(f32)
    col_pos = (pos % GRID_W).astype(f32)
    ang_r = row_pos[:, None] * inv[None, :]
    ang_c = col_pos[:, None] * inv[None, :]
    cos_h = jnp.concatenate([jnp.cos(ang_r)] * 2 + [jnp.cos(ang_c)] * 2, axis=-1)
    sin_h = jnp.concatenate([-jnp.sin(ang_r), jnp.sin(ang_r), -jnp.sin(ang_c), jnp.sin(ang_c)], axis=-1)
    reps = d_ret // HEAD
    cos_l = jnp.tile(cos_h, (1, reps))
    sin_l = jnp.tile(sin_h, (1, reps))
    cos_t = jnp.concatenate([jnp.ones((n_ctx, d_ret), f32), cos_l], axis=0)
    sin_t = jnp.concatenate([jnp.zeros((n_ctx, d_ret), f32), sin_l], axis=0)
    return cos_t, sin_t


def _split_weight(w):
    hi = w.astype(bf16)
    lo = (w - hi.astype(f32)).astype(bf16)
    return hi, lo


def kernel(x, c, ctx, c_ctx, norm_w, w_mod, b_mod, w_in, ret_log_gamma, ret_norm_w, rwkv_shift_w, rwkv_w0, rwkv_w2, rwkv_a0, rwkv_a2, rwkv_k_k, rwkv_k_a, rwkv_r_k, rwkv_ln_w, rwkv_ln_b, w_out, final_norm_w):
    b, seq, d = x.shape
    n_ctx = ctx.shape[1]
    depth = w_in.shape[0]
    d_ret = ret_norm_w.shape[-1]
    d_rwkv = rwkv_w0.shape[-1]
    d_shift = rwkv_shift_w.shape[-1]

    cond = jnp.zeros((16, d), f32).at[:b].set(c.astype(f32)).at[b].set(c_ctx.astype(f32))
    mods = _modulation(cond, w_mod, b_mod)
    cos_t, sin_t = _rope_tables(n_ctx, seq, d_ret)
    xall = jnp.concatenate([ctx.astype(f32), x.astype(f32)], axis=1)

    out = None
    for layer in range(depth):
        m = mods[layer].reshape(16, 3, d)
        m = jnp.stack([m[:, 1], m[:, 0], m[:, 2]], axis=1)
        lat = m[:b]
        cx = jnp.broadcast_to(m[b][None], (b, 3, d))
        modv = jnp.pad(jnp.stack([cx, lat], axis=1), ((0, 0), (0, 0), (0, 5), (0, 0)))

        wih, wil = _split_weight(w_in[layer])
        qkv, rw, gates = _inproj(xall, modv, norm_w[layer][None], wih, wil, cos_t, sin_t,
                                 d_ret, d_shift, d_rwkv)
        ret = _retention(qkv, ret_log_gamma[layer], ret_norm_w[layer][None], n_ctx)
        rwo = _rwkv(rw, rwkv_shift_w[layer], rwkv_w0[layer], rwkv_w2[layer], rwkv_a0[layer],
                    rwkv_a2[layer], rwkv_k_k[layer][None], rwkv_k_a[layer][None],
                    rwkv_r_k[layer].reshape(1, d_rwkv), rwkv_ln_w[layer][None],
                    rwkv_ln_b[layer][None], n_ctx)
        woh, wol = _split_weight(w_out[layer])
        final = layer == depth - 1
        res = _outproj(ret, rwo, gates, xall, modv, woh, wol, final_norm_w[None], n_ctx, final)
        if final:
            out = res
        else:
            xall = res
    return out
```

```python
import functools
import math

import jax
import jax.numpy as jnp
from jax import lax
from jax.experimental import pallas as pl
from jax.experimental.pallas import tpu as pltpu

f32 = jnp.float32
bf16 = jnp.bfloat16

HEAD = 64
PAIR = 2 * HEAD
RET_C = 128
RW_C = 64
RW_UNROLL = 4
SCAN_UNROLL = 2
GRID_W = 64
ROPE_BASE = 10000.0
NORM_EPS = 1e-6
GN_EPS = 64e-5
ROW_TILE = 256
HALF = 256
NP = HALF // PAIR
VMEM_LIMIT = 56 * 1024 * 1024

NN = ((1,), (0,))
NT = ((1,), (1,))
TN = ((0,), (0,))

P_MOD = 3
P_IN = 1
P_OUT = 1
P_RET = 1
P_RW = 1


def _dg(a, b, dims):
    return lax.dot_general(a, b, (dims, ((), ())), preferred_element_type=f32)


def _split2(a):
    hi = a.astype(bf16)
    lo = (a - hi.astype(f32)).astype(bf16)
    return hi, lo


def _split3(a):
    hi = a.astype(bf16)
    r1 = a - hi.astype(f32)
    mid = r1.astype(bf16)
    lo = (r1 - mid.astype(f32)).astype(bf16)
    return hi, mid, lo


def _mm(a, b, dims=NN, passes=3):
    if passes == 1:
        return _dg(a.astype(bf16), b.astype(bf16), dims)
    ah, al = _split2(a)
    bh, bl = _split2(b)
    return _dg(ah, bh, dims) + (_dg(ah, bl, dims) + _dg(al, bh, dims))


def _mm_sel(sel, x):
    h, m, l = _split3(x)
    return _dg(sel, h, NN) + (_dg(sel, m, NN) + _dg(sel, l, NN))


def _group_sum(x, gmat):
    h, m, l = _split3(x)
    return _dg(h, gmat, NN) + (_dg(m, gmat, NN) + _dg(l, gmat, NN))


def _group_mat(n):
    ii = lax.broadcasted_iota(jnp.int32, (n, n), 0) // HEAD
    jj = lax.broadcasted_iota(jnp.int32, (n, n), 1) // HEAD
    return jnp.where(ii == jj, 1.0, 0.0).astype(bf16)


def _first_half():
    return lax.broadcasted_iota(jnp.int32, (1, PAIR), 1) < HEAD


def _bd(x, first):
    return jnp.concatenate([jnp.where(first, x, 0.0), jnp.where(first, 0.0, x)], axis=0)


def _heads_to_rows(x):
    return jnp.concatenate([x[:, :HEAD], pltpu.roll(x, HEAD, 1)[:, :HEAD]], axis=0)


def _mod_kernel(c_ref, w_ref, b_ref, o_ref):
    cnd = c_ref[...]
    s = cnd * jax.nn.sigmoid(cnd)
    o_ref[0] = _mm(s, w_ref[0], NN, P_MOD) + b_ref[0]


def _modulation(cond, w_mod, b_mod):
    depth, d, d3 = w_mod.shape
    rows = cond.shape[0]
    tn = 1024
    return pl.pallas_call(
        _mod_kernel,
        grid=(depth, d3 // tn),
        in_specs=[pl.BlockSpec((rows, d), lambda l, j: (0, 0)),
                  pl.BlockSpec((1, d, tn), lambda l, j: (l, 0, j)),
                  pl.BlockSpec((1, 1, tn), lambda l, j: (l, 0, j))],
        out_specs=pl.BlockSpec((1, rows, tn), lambda l, j: (l, 0, j)),
        out_shape=jax.ShapeDtypeStruct((depth, rows, d3), f32),
        compiler_params=pltpu.CompilerParams(vmem_limit_bytes=VMEM_LIMIT),
        name="adaln_mod",
    )(cond, w_mod, b_mod.reshape(depth, 1, d3))


def _inproj_kernel(x_ref, mod_ref, nw_ref, wh_ref, wl_ref, cos_ref, sin_ref,
                   qkv_ref, rw_ref, g_ref, *, d_ret, d_shift):
    x = x_ref[0]
    ms = jnp.mean(x * x, axis=-1, keepdims=True)
    m = mod_ref[0, 0]
    h = x * lax.rsqrt(ms + NORM_EPS) * nw_ref[...]
    h = h * (1.0 + m[0:1]) + m[1:2]
    if P_IN == 1:
        hh = h.astype(bf16)
        hl = None
    else:
        hh, hl = _split2(h)

    def proj(lo, hi):
        wh = wh_ref[:, lo:hi]
        acc = _dg(hh, wh, NN)
        if hl is not None:
            acc = acc + (_dg(hh, wl_ref[:, lo:hi], NN) + _dg(hl, wh, NN))
        return acc

    cos = cos_ref[...]
    sin = sin_ref[...]
    lane = lax.broadcasted_iota(jnp.int32, cos.shape, 1)
    first = (lane % 32) < 16

    def rope(t):
        nl = t.shape[-1]
        swapped = jnp.where(first, pltpu.roll(t, nl - 16, 1), pltpu.roll(t, 16, 1))
        return t * cos + swapped * sin

    qkv_ref[0, :, 0:d_ret] = rope(proj(0, d_ret))
    qkv_ref[0, :, d_ret:2 * d_ret] = rope(proj(d_ret, 2 * d_ret)) * (HEAD ** -0.5)
    qkv_ref[0, :, 2 * d_ret:3 * d_ret] = proj(2 * d_ret, 3 * d_ret)
    g_ref[0, :, 0:d_ret] = proj(3 * d_ret, 4 * d_ret)
    o = 4 * d_ret
    rw_ref[0] = proj(o, o + d_shift)
    g_ref[0, :, d_ret:] = proj(o + d_shift, wh_ref.shape[1])


def _inproj(xall, modv, nw, wh, wl, cos_t, sin_t, d_ret, d_shift, d_rwkv):
    b, t, d = xall.shape
    n_tiles = t // ROW_TILE
    const = dict(pipeline_mode=pl.Buffered(1))
    kern = functools.partial(_inproj_kernel, d_ret=d_ret, d_shift=d_shift)
    return pl.pallas_call(
        kern,
        grid=(b, n_tiles),
        in_specs=[pl.BlockSpec((1, ROW_TILE, d), lambda i, j: (i, j, 0)),
                  pl.BlockSpec((1, 1, 8, d), lambda i, j: (i, jnp.minimum(j, 1), 0, 0)),
                  pl.BlockSpec((1, d), lambda i, j: (0, 0)),
                  pl.BlockSpec(wh.shape, lambda i, j: (0, 0), **const),
                  pl.BlockSpec(wl.shape, lambda i, j: (0, 0), **const),
                  pl.BlockSpec((ROW_TILE, d_ret), lambda i, j: (j, 0)),
                  pl.BlockSpec((ROW_TILE, d_ret), lambda i, j: (j, 0))],
        out_specs=[pl.BlockSpec((1, ROW_TILE, 3 * d_ret), lambda i, j: (i, j, 0)),
                   pl.BlockSpec((1, ROW_TILE, d_shift), lambda i, j: (i, j, 0)),
                   pl.BlockSpec((1, ROW_TILE, d_ret + d_rwkv), lambda i, j: (i, j, 0))],
        out_shape=[jax.ShapeDtypeStruct((b, t, 3 * d_ret), f32),
                   jax.ShapeDtypeStruct((b, t, d_shift), f32),
                   jax.ShapeDtypeStruct((b, t, d_ret + d_rwkv), f32)],
        compiler_params=pltpu.CompilerParams(vmem_limit_bytes=VMEM_LIMIT),
        name="in_proj",
    )(xall, modv, nw, wh, wl, cos_t, sin_t)


def _scan_chunk(idx, forward, n_ctx_chunks, n_chunks):
    if forward:
        return idx
    return jnp.where(idx < n_ctx_chunks, n_ctx_chunks - 1 - idx, n_chunks + n_ctx_chunks - 1 - idx)


def _ret_kernel(lg_ref, q_ref, k_ref, v_ref, nw_ref, o_ref,
                s_ref, mask_ref, qd_ref, kd_ref, cd_ref, *, n_ctx_chunks, n_chunks):
    hh = pl.program_id(1)
    c = RET_C
    first = _first_half()
    ii = lax.broadcasted_iota(jnp.int32, (c, 2 * c), 0)
    nn = lax.broadcasted_iota(jnp.int32, (c, 2 * c), 1)
    ri = lax.broadcasted_iota(jnp.int32, (c, PAIR), 0).astype(f32)
    for d in range(2):
        rel = ((ii - nn % c) if d == 0 else (nn % c - ii)).astype(f32)
        for j in range(NP):
            lg0 = lg_ref[d, hh * 2 * NP + 2 * j]
            lg1 = lg_ref[d, hh * 2 * NP + 2 * j + 1]
            lgm = jnp.where(nn < c, lg0, lg1)
            lgv = jnp.where(first, lg0, lg1)
            mask_ref[d, j] = jnp.where(rel >= 0.0, jnp.exp(lgm * jnp.maximum(rel, 0.0)), 0.0)
            qpow = (ri + 1.0) if d == 0 else (c - ri)
            kpow = (c - 1.0 - ri) if d == 0 else ri
            qd_ref[d, j] = jnp.exp(lgv * qpow)
            kd_ref[d, j] = jnp.exp(lgv * kpow)
            cd_ref[d, j] = jnp.exp(jnp.broadcast_to(lgv * c, (HEAD, PAIR)))
    s_ref[...] = jnp.zeros(s_ref.shape, f32)
    o_ref[...] = jnp.zeros(o_ref.shape, f32)
    chains = [(d, j) for d in range(2) for j in range(NP)]

    def step(s, carry):
        rows, qp, kp, vp = {}, {}, {}, {}
        for d in range(2):
            ch = _scan_chunk(s, d == 0, n_ctx_chunks, n_chunks)
            r0 = pl.multiple_of(ch * c, c)
            rows[d] = pl.ds(r0, c)
            q = q_ref[0, rows[d], :]
            k = k_ref[0, rows[d], :]
            v = v_ref[0, rows[d], :]
            for j in range(NP):
                col = slice(j * PAIR, (j + 1) * PAIR)
                qp[d, j], kp[d, j], vp[d, j] = q[:, col], k[:, col], v[:, col]
        sc, st, vb = {}, {}, {}
        for ck in chains:
            sc[ck] = _mm(qp[ck], _bd(kp[ck], first), NT, P_RET) * mask_ref[ck[0], ck[1]]
            st[ck] = s_ref[ck[0], ck[1]]
            vb[ck] = _bd(vp[ck], first)
        for (d, j) in chains:
            ck = (d, j)
            o = _mm(sc[ck], vb[ck], NN, P_RET) + _mm(qp[ck] * qd_ref[d, j], _bd(st[ck], first), NN, P_RET)
            s_ref[d, j] = st[ck] * cd_ref[d, j] + _mm(_heads_to_rows(kp[ck] * kd_ref[d, j]), vb[ck], TN, P_RET)
            o_ref[0, rows[d], j * PAIR:(j + 1) * PAIR] += o
        return carry

    lax.fori_loop(0, n_chunks, step, 0)

    gmat = _group_mat(HALF)
    nw = nw_ref[...]

    def norm_tile(i, carry):
        r0 = pl.multiple_of(i * ROW_TILE, ROW_TILE)
        o = o_ref[0, pl.ds(r0, ROW_TILE), :]
        ms = _group_sum(o * o, gmat) * (1.0 / HEAD)
        o_ref[0, pl.ds(r0, ROW_TILE), :] = o * lax.rsqrt(ms + NORM_EPS) * nw
        return carry

    lax.fori_loop(0, (n_chunks * c) // ROW_TILE, norm_tile, 0)


def _retention(qkv, log_gamma, norm_w, n_ctx):
    b, t, w3 = qkv.shape
    d_ret = w3 // 3
    nhalf = d_ret // HALF
    kern = functools.partial(_ret_kernel, n_ctx_chunks=n_ctx // RET_C, n_chunks=t // RET_C)
    blk = lambda off: pl.BlockSpec((1, t, HALF), lambda i, j, off=off: (i, 0, off * nhalf + j))
    return pl.pallas_call(
        kern,
        grid=(b, nhalf),
        in_specs=[pl.BlockSpec(memory_space=pltpu.SMEM),
                  blk(0), blk(1), blk(2),
                  pl.BlockSpec((1, HALF), lambda i, j: (0, j))],
        out_specs=pl.BlockSpec((1, t, HALF), lambda i, j: (i, 0, j)),
        out_shape=jax.ShapeDtypeStruct((b, t, d_ret), f32),
        scratch_shapes=[pltpu.VMEM((2, NP, HEAD, PAIR), f32),
                        pltpu.VMEM((2, NP, RET_C, 2 * RET_C), f32),
                        pltpu.VMEM((2, NP, RET_C, PAIR), f32),
                        pltpu.VMEM((2, NP, RET_C, PAIR), f32),
                        pltpu.VMEM((2, NP, HEAD, PAIR), f32)],
        compiler_params=pltpu.CompilerParams(vmem_limit_bytes=VMEM_LIMIT),
        name="retention",
    )(log_gamma, qkv, qkv, qkv, norm_w)


class _Pair:
    def __init__(self, x, first):
        self.x = x
        self.first = first
        self._parts = None
        self._lhs = None
        self._bd = None

    def parts(self):
        if self._parts is None:
            hi = self.x.astype(bf16).astype(f32)
            self._parts = (hi, self.x - hi)
        return self._parts

    def lhs(self):
        if self._lhs is None:
            if P_RW == 1:
                self._lhs = self.x.astype(bf16)
            else:
                hi, lo = self.parts()
                self._lhs = jnp.concatenate([hi, hi, lo], axis=1).astype(bf16)
        return self._lhs

    def _bd_parts(self):
        if self._bd is None:
            if P_RW == 1:
                self._bd = (_bd(self.x, self.first).astype(bf16),)
            else:
                hi, lo = self.parts()
                bh = _bd(hi, self.first).astype(bf16)
                self._bd = (bh, _bd(lo, self.first).astype(bf16), bh)
        return self._bd

    def rhs(self):
        return jnp.concatenate(self._bd_parts(), axis=0)

    def rhs_nt(self):
        return jnp.concatenate(self._bd_parts(), axis=1)

    def lhs_rows(self):
        if P_RW == 1:
            return _heads_to_rows(self.x).astype(bf16)
        hi, lo = self.parts()
        th = _heads_to_rows(hi)
        return jnp.concatenate([th, th, _heads_to_rows(lo)], axis=0).astype(bf16)


def _lhs_rows(x):
    if P_RW == 1:
        return x.astype(bf16)
    hi = x.astype(bf16).astype(f32)
    return jnp.concatenate([hi, hi, x - hi], axis=0).astype(bf16)


def _mmp(a, b):
    return _dg(a.lhs(), b.rhs(), NN)


def _rwkv_kernel(r_ref, k_ref, v_ref, xwa_ref, swr_ref, swk_ref, swv_ref, swx_ref,
                 w0_ref, w2_ref, a0_ref, a2_ref, kk_ref, ka_ref, rk_ref, lnw_ref, lnb_ref,
                 o_ref, bon_ref,
                 g_ref, h_ref, q_ref, y_ref, pc_ref,
                 *, n_ctx_chunks, n_chunks):
    c = RW_C
    t_rows = n_chunks * c
    gmat = _group_mat(HALF)
    rows = lax.broadcasted_iota(jnp.int32, (c, 1), 0)

    def conv(ref, sw_ref, ch, r0):
        x = ref[0, pl.ds(r0, c), :]
        p0 = pl.multiple_of(jnp.maximum(r0 - 8, 0), 8)
        n0 = pl.multiple_of(jnp.minimum(r0 + c, t_rows - 8), 8)
        prev8 = ref[0, pl.ds(p0, 8), :]
        next8 = ref[0, pl.ds(n0, 8), :]
        has_prev = jnp.logical_and(ch != 0, ch != n_ctx_chunks).astype(f32)
        has_next = jnp.logical_and(ch != n_ctx_chunks - 1, ch != n_chunks - 1).astype(f32)
        xd = jnp.where(rows == 0, prev8[7:8, :] * has_prev, pltpu.roll(x, 1, 0))
        xu = jnp.where(rows == c - 1, next8[0:1, :] * has_next, pltpu.roll(x, c - 1, 0))
        sw = sw_ref[...]
        return sw[0:1] * xd + sw[1:2] * x + sw[2:3] * xu

    first = _first_half()
    ii = lax.broadcasted_iota(jnp.int32, (c, PAIR), 0)
    jj = lax.broadcasted_iota(jnp.int32, (c, PAIR), 1) % HEAD
    eye = jnp.where(ii == jj, 1.0, 0.0)
    bd16 = (ii // 16) == (jj // 16)
    same32 = (ii // 32) == (jj // 32)
    off32 = jnp.logical_and(same32, jnp.logical_not(bd16))
    off64 = jnp.logical_not(same32)
    strict = (jj < ii, jj > ii)
    incl = (jj <= ii, jj >= ii)
    both = tuple(jnp.concatenate([strict[d], incl[d]], axis=0) for d in range(2))
    i2 = lax.broadcasted_iota(jnp.int32, (c, c), 0)
    j2 = lax.broadcasted_iota(jnp.int32, (c, c), 1)
    tri = (jnp.where(j2 <= i2, 1.0, 0.0).astype(bf16), jnp.where(j2 >= i2, 1.0, 0.0).astype(bf16))
    lora = xwa_ref.shape[2] // 2
    wscale = -math.exp(-0.5)
    mk = lambda x: _Pair(x, first)
    col = lambda j: slice(j * PAIR, (j + 1) * PAIR)

    def prepare(it, carry):
        feat = {}
        for u in range(RW_UNROLL):
            ch = it * RW_UNROLL + u
            r0 = pl.multiple_of(ch * c, c)
            sl = pl.ds(r0, c)
            r = conv(r_ref, swr_ref, ch, r0)
            k = conv(k_ref, swk_ref, ch, r0)
            v = conv(v_ref, swv_ref, ch, r0)
            xwa = conv(xwa_ref, swx_ref, ch, r0)
            kk = k * kk_ref[...]
            kap = kk / jnp.maximum(jnp.sqrt(_group_sum(kk * kk, gmat)), 1e-12)
            bon_ref[sl, :] = _group_sum(r * k * rk_ref[...], gmat) * v
            txw = jnp.tanh(xwa[:, :lora])
            for d in range(2):
                u_in = w0_ref[d:d + 1, :] + _mm(txw, w2_ref[d], NN, 3)
                lw = wscale * jax.nn.sigmoid(u_in)
                a = jax.nn.sigmoid(a0_ref[d:d + 1, :] + _mm(xwa[:, lora:], a2_ref[d], NN, 3))
                kmod = k * (1.0 + (a - 1.0) * ka_ref[...])
                b = kap * a
                lp = _mm_sel(tri[d], lw)
                lpc = lp[c - 1:c, :] if d == 0 else lp[0:1, :]
                pinv = jnp.exp(-lp)
                ptail = jnp.exp(lpc - lp)
                pc_ref[d, pl.ds(pl.multiple_of(ch * 8, 8), 8), :] = jnp.broadcast_to(jnp.exp(lpc), (8, lp.shape[1]))
                feat[u, d] = dict(
                    sl=sl, v=v, rq=r * jnp.exp(lp),
                    kr=jnp.concatenate([kap * jnp.exp(lp - lw), r * jnp.exp(lp)], axis=0),
                    kd=kmod * pinv, bd=b * pinv, kh=kmod * ptail, bh=b * ptail)

        chains = [(u, d, j) for u in range(RW_UNROLL) for d in range(2) for j in range(NP)]
        mkv, m_b, n_b, vp = {}, {}, {}, {}
        for ck in chains:
            u, d, j = ck
            f = feat[u, d]
            kr = mk(f["kr"][:, col(j)])
            mk_ = _dg(kr.lhs(), mk(f["kd"][:, col(j)]).rhs_nt(), NT)
            mb_ = _dg(kr.lhs(), mk(f["bd"][:, col(j)]).rhs_nt(), NT)
            vp[ck] = mk(f["v"][:, col(j)])
            mkv[ck] = _mmp(mk(jnp.where(both[d], mk_, 0.0)), vp[ck])
            m_b[ck] = jnp.where(strict[d], mb_[:c], 0.0)
            n_b[ck] = mk(jnp.where(incl[d], mb_[c:], 0.0))
        p, npow = {}, {}
        for ck in chains:
            nd = mk(jnp.where(bd16, -m_b[ck], 0.0))
            p[ck] = eye + nd.x
            npow[ck] = mk(_mmp(nd, nd))
        for level in range(2):
            for ck in chains:
                both_ = _mmp(mk(jnp.concatenate([p[ck], npow[ck].x], axis=0)), npow[ck])
                p[ck] = p[ck] + both_[:c]
                npow[ck] = mk(both_[c:])
        for ck in chains:
            p[ck] = p[ck] + _mmp(mk(p[ck]), npow[ck])
        for off in (off32, off64):
            tmp, pm = {}, {}
            for ck in chains:
                pm[ck] = mk(p[ck])
                tmp[ck] = mk(_mmp(pm[ck], mk(jnp.where(off, m_b[ck], 0.0))))
            for ck in chains:
                p[ck] = p[ck] - _mmp(tmp[ck], pm[ck])
        ta = {}
        for ck in chains:
            u, d, j = ck
            kq = feat[u, d]["kr"][:c, col(j)]
            ta[ck] = _dg(mk(p[ck]).lhs(), jnp.concatenate([mk(kq).rhs(), mk(mkv[ck][:c]).rhs()], axis=1), NN)
        for ck in chains:
            u, d, j = ck
            f = feat[u, d]
            a1, z0 = ta[ck][:, :PAIR], ta[ck][:, PAIR:]
            nb = _dg(n_b[ck].lhs(), jnp.concatenate([mk(a1).rhs(), mk(z0).rhs()], axis=1), NN)
            az = jnp.concatenate([jnp.where(first, a1, pltpu.roll(z0, HEAD, 1)),
                                  jnp.where(first, pltpu.roll(a1, HEAD, 1), z0)], axis=0)
            bh = mk(f["bh"][:, col(j)])
            ab = _dg(_lhs_rows(az), bh.rhs(), TN)
            vk = _dg(vp[ck].lhs_rows(), mk(f["kh"][:, col(j)]).rhs(), TN)
            g_ref[d, f["sl"], col(j)] = -ab[:HEAD]
            h_ref[d, f["sl"], col(j)] = vk - ab[HEAD:]
            q_ref[d, f["sl"], col(j)] = f["rq"][:, col(j)] - nb[:, :PAIR]
            y_ref[d, f["sl"], col(j)] = mkv[ck][c:] - nb[:, PAIR:]
        return carry

    lax.fori_loop(0, n_chunks // RW_UNROLL, prepare, 0)

    o_ref[...] = jnp.zeros(o_ref.shape, f32)

    def scan(it, st):
        st = list(st)
        for u in range(SCAN_UNROLL):
            for d in range(2):
                ch = _scan_chunk(it * SCAN_UNROLL + u, d == 0, n_ctx_chunks, n_chunks)
                sl = pl.ds(pl.multiple_of(ch * c, c), c)
                pc = pc_ref[d, pl.ds(pl.multiple_of(ch * 8, 8), 8), :][0:1]
                g, h, q, y0 = g_ref[d, sl, :], h_ref[d, sl, :], q_ref[d, sl, :], y_ref[d, sl, :]
                for j in range(NP):
                    s0 = st[d * NP + j]
                    sp = mk(s0)
                    y = _dg(mk(q[:, col(j)]).lhs(), sp.rhs_nt(), NT) + y0[:, col(j)]
                    st[d * NP + j] = (s0 * pc[:, col(j)] + _mmp(sp, mk(g[:, col(j)]))) + h[:, col(j)]
                    o_ref[0, sl, col(j)] += y
        return tuple(st)

    lax.fori_loop(0, n_chunks // SCAN_UNROLL, scan,
                  tuple(jnp.zeros((HEAD, PAIR), f32) for _ in range(2 * NP)))

    lnw = lnw_ref[...]
    lnb = lnb_ref[...]

    def readout(i, carry):
        r0 = pl.multiple_of(i * ROW_TILE, ROW_TILE)
        sl = pl.ds(r0, ROW_TILE)
        y = o_ref[0, sl, :]
        mu = _group_sum(y, gmat) * (1.0 / HEAD)
        yc = y - mu
        var = _group_sum(yc * yc, gmat) * (1.0 / HEAD)
        o_ref[0, sl, :] = yc * lax.rsqrt(var + GN_EPS) * lnw + lnb + bon_ref[sl, :]
        return carry

    lax.fori_loop(0, t_rows // ROW_TILE, readout, 0)


def _rwkv(rw, shift_w, w0, w2, a0, a2, k_k, k_a, r_k, ln_w, ln_b, n_ctx):
    b, t, d_shift = rw.shape
    d_rwkv = w0.shape[-1]
    lora2 = d_shift - 3 * d_rwkv
    nhalf = d_rwkv // HALF
    xcol = (3 * d_rwkv) // lora2
    assert (t // RW_C) % RW_UNROLL == 0 and (t // RW_C) % SCAN_UNROLL == 0
    kern = functools.partial(_rwkv_kernel, n_ctx_chunks=n_ctx // RW_C, n_chunks=t // RW_C)
    seq = lambda off: pl.BlockSpec((1, t, HALF), lambda i, j, off=off: (i, 0, off * nhalf + j))
    swb = lambda off: pl.BlockSpec((3, HALF), lambda i, j, off=off: (0, off * nhalf + j))
    vec = lambda rows: pl.BlockSpec((rows, HALF), lambda i, j: (0, j))
    lor = pl.BlockSpec((2, lora2 // 2, HALF), lambda i, j: (0, 0, j))
    return pl.pallas_call(
        kern,
        grid=(b, nhalf),
        in_specs=[seq(0), seq(1), seq(2),
                  pl.BlockSpec((1, t, lora2), lambda i, j: (i, 0, xcol)),
                  swb(0), swb(1), swb(2),
                  pl.BlockSpec((3, lora2), lambda i, j: (0, xcol)),
                  vec(2), lor, vec(2), lor, vec(1), vec(1), vec(1), vec(1), vec(1)],
        out_specs=pl.BlockSpec((1, t, HALF), lambda i, j: (i, 0, j)),
        out_shape=jax.ShapeDtypeStruct((b, t, d_rwkv), f32),
        scratch_shapes=[pltpu.VMEM((t, HALF), f32)] + [pltpu.VMEM((2, t, HALF), f32)] * 4
        + [pltpu.VMEM((2, (t // RW_C) * 8, HALF), f32)],
        compiler_params=pltpu.CompilerParams(vmem_limit_bytes=VMEM_LIMIT),
        name="rwkv7",
    )(rw, rw, rw, rw, shift_w, shift_w, shift_w, shift_w,
      w0, w2, a0, a2, k_k, k_a, r_k, ln_w, ln_b)


def _outproj_kernel(ret_ref, rwo_ref, g_ref, x_ref, mod_ref, wh_ref, wl_ref, fw_ref, o_ref,
                    *, final):
    d_ret = ret_ref.shape[-1]
    g = g_ref[0]
    act = jnp.concatenate([ret_ref[0] * (g[:, :d_ret] * jax.nn.sigmoid(g[:, :d_ret])),
                           rwo_ref[0] * (g[:, d_ret:] * jax.nn.sigmoid(g[:, d_ret:]))], axis=1)
    if P_OUT == 1:
        mix = _dg(act.astype(bf16), wh_ref[...], NN)
    else:
        ah, al = _split2(act)
        wh = wh_ref[...]
        mix = _dg(ah, wh, NN) + (_dg(ah, wl_ref[...], NN) + _dg(al, wh, NN))
    xn = x_ref[0] + mod_ref[0, 0][2:3] * mix
    if final:
        ms = jnp.mean(xn * xn, axis=-1, keepdims=True)
        xn = xn * lax.rsqrt(ms + NORM_EPS) * fw_ref[...]
    o_ref[0] = xn


def _outproj(ret, rwo, gates, xall, modv, wh, wl, fw, n_ctx, final):
    b, t, d = xall.shape
    d_ret = ret.shape[-1]
    d_rw = rwo.shape[-1]
    skip = (n_ctx // ROW_TILE) if final else 0
    n_tiles = t // ROW_TILE - skip
    const = dict(pipeline_mode=pl.Buffered(1))
    row = lambda w: pl.BlockSpec((1, ROW_TILE, w), lambda i, j: (i, j + skip, 0))
    kern = functools.partial(_outproj_kernel, final=final)
    return pl.pallas_call(
        kern,
        grid=(b, n_tiles),
        in_specs=[row(d_ret), row(d_rw), row(d_ret + d_rw), row(d),
                  pl.BlockSpec((1, 1, 8, d), lambda i, j: (i, jnp.minimum(j + skip, 1), 0, 0)),
                  pl.BlockSpec(wh.shape, lambda i, j: (0, 0), **const),
                  pl.BlockSpec(wl.shape, lambda i, j: (0, 0), **const),
                  pl.BlockSpec((1, d), lambda i, j: (0, 0))],
        out_specs=pl.BlockSpec((1, ROW_TILE, d), lambda i, j: (i, j, 0)),
        out_shape=jax.ShapeDtypeStruct((b, n_tiles * ROW_TILE, d), f32),
        compiler_params=pltpu.CompilerParams(vmem_limit_bytes=VMEM_LIMIT),
        name="out_proj",
    )(ret, rwo, gates, xall, modv, wh, wl, fw)


def _rope_tables(n_ctx, seq, d_ret):
    nf = HEAD // 4
    inv = ROPE_BASE ** (-jnp.arange(nf, dtype=f32) / nf)
    pos = jnp.arange(seq)
    row_pos = (pos // GRID_W).astype(f32)
    col_pos = (pos % GRID_W).astype(f32)
    ang_r = row_pos[:, None] * inv[None, :]
    ang_c = col_pos[:, None] * inv[None, :]
    cos_h = jnp.concatenate([jnp.cos(ang_r)] * 2 + [jnp.cos(ang_c)] * 2, axis=-1)
    sin_h = jnp.concatenate([-jnp.sin(ang_r), jnp.sin(ang_r), -jnp.sin(ang_c), jnp.sin(ang_c)], axis=-1)
    reps = d_ret // HEAD
    cos_l = jnp.tile(cos_h, (1, reps))
    sin_l = jnp.tile(sin_h, (1, reps))
    cos_t = jnp.concatenate([jnp.ones((n_ctx, d_ret), f32), cos_l], axis=0)
    sin_t = jnp.concatenate([jnp.zeros((n_ctx, d_ret), f32), sin_l], axis=0)
    return cos_t, sin_t


def _split_weight(w):
    hi = w.astype(bf16)
    lo = (w - hi.astype(f32)).astype(bf16)
    return hi, lo


def kernel(x, c, ctx, c_ctx, norm_w, w_mod, b_mod, w_in, ret_log_gamma, ret_norm_w, rwkv_shift_w, rwkv_w0, rwkv_w2, rwkv_a0, rwkv_a2, rwkv_k_k, rwkv_k_a, rwkv_r_k, rwkv_ln_w, rwkv_ln_b, w_out, final_norm_w):
    b, seq, d = x.shape
    n_ctx = ctx.shape[1]
    depth = w_in.shape[0]
    d_ret = ret_norm_w.shape[-1]
    d_rwkv = rwkv_w0.shape[-1]
    d_shift = rwkv_shift_w.shape[-1]

    cond = jnp.zeros((16, d), f32).at[:b].set(c.astype(f32)).at[b].set(c_ctx.astype(f32))
    mods = _modulation(cond, w_mod, b_mod)
    cos_t, sin_t = _rope_tables(n_ctx, seq, d_ret)
    xall = jnp.concatenate([ctx.astype(f32), x.astype(f32)], axis=1)

    out = None
    for layer in range(depth):
        m = mods[layer].reshape(16, 3, d)
        m = jnp.stack([m[:, 1], m[:, 0], m[:, 2]], axis=1)
        lat = m[:b]
        cx = jnp.broadcast_to(m[b][None], (b, 3, d))
        modv = jnp.pad(jnp.stack([cx, lat], axis=1), ((0, 0), (0, 0), (0, 5), (0, 0)))

        wih, wil = _split_weight(w_in[layer])
        qkv, rw, gates = _inproj(xall, modv, norm_w[layer][None], wih, wil, cos_t, sin_t,
                                 d_ret, d_shift, d_rwkv)
        ret = _retention(qkv, ret_log_gamma[layer], ret_norm_w[layer][None], n_ctx)
        rwo = _rwkv(rw, rwkv_shift_w[layer], rwkv_w0[layer], rwkv_w2[layer], rwkv_a0[layer],
                    rwkv_a2[layer], rwkv_k_k[layer][None], rwkv_k_a[layer][None],
                    rwkv_r_k[layer].reshape(1, d_rwkv), rwkv_ln_w[layer][None],
                    rwkv_ln_b[layer][None], n_ctx)
        woh, wol = _split_weight(w_out[layer])
        final = layer == depth - 1
        res = _outproj(ret, rwo, gates, xall, modv, woh, wol, final_norm_w[None], n_ctx, final)
        if final:
            out = res
        else:
            xall = res
    return out
```

```python
import functools
import math

import jax
import jax.numpy as jnp
from jax import lax
from jax.experimental import pallas as pl
from jax.experimental.pallas import tpu as pltpu

f32 = jnp.float32
bf16 = jnp.bfloat16

HEAD = 64
PAIR = 2 * HEAD
RET_C = 128
RW_C = 64
RW_UNROLL = 4
SCAN_UNROLL = 2
GRID_W = 64
ROPE_BASE = 10000.0
NORM_EPS = 1e-6
GN_EPS = 64e-5
ROW_TILE = 256
HALF = 256
NP = HALF // PAIR
RW_W = 128
RW_NP = HALF // RW_W
VMEM_LIMIT = 56 * 1024 * 1024

NN = ((1,), (0,))
NT = ((1,), (1,))
TN = ((0,), (0,))

P_MOD = 3
P_RET = 1
P_RW = 1


def _dg(a, b, dims):
    return lax.dot_general(a, b, (dims, ((), ())), preferred_element_type=f32)


def _split2(a):
    hi = a.astype(bf16)
    lo = (a - hi.astype(f32)).astype(bf16)
    return hi, lo


def _mm(a, b, dims=NN, passes=3):
    if passes == 1:
        return _dg(a.astype(bf16), b.astype(bf16), dims)
    ah, al = _split2(a)
    bh, bl = _split2(b)
    return _dg(ah, bh, dims) + (_dg(ah, bl, dims) + _dg(al, bh, dims))


def _mm_sel(sel, x):
    h, l = _split2(x)
    return _dg(sel, h, NN) + _dg(sel, l, NN)


def _group_sum(x, gmat):
    h, l = _split2(x)
    return _dg(h, gmat, NN) + _dg(l, gmat, NN)


def _group_mat(n):
    ii = lax.broadcasted_iota(jnp.int32, (n, n), 0) // HEAD
    jj = lax.broadcasted_iota(jnp.int32, (n, n), 1) // HEAD
    return jnp.where(ii == jj, 1.0, 0.0).astype(bf16)


def _head_masks(width):
    blk = lax.broadcasted_iota(jnp.int32, (1, width), 1) // HEAD
    return [blk == h for h in range(width // HEAD)]


def _bd(x, masks):
    return jnp.concatenate([jnp.where(m, x, 0.0) for m in masks], axis=0)


def _heads_to_rows(x):
    w = x.shape[1]
    return jnp.concatenate([x[:, :HEAD]] + [pltpu.roll(x, w - h * HEAD, 1)[:, :HEAD]
                                            for h in range(1, w // HEAD)], axis=0)


def _mod_kernel(c_ref, w_ref, b_ref, o_ref):
    cnd = c_ref[...]
    s = cnd * jax.nn.sigmoid(cnd)
    o_ref[0] = _mm(s, w_ref[0], NN, P_MOD) + b_ref[0]


def _modulation(cond, w_mod, b_mod):
    depth, d, d3 = w_mod.shape
    rows = cond.shape[0]
    tn = 1024
    return pl.pallas_call(
        _mod_kernel,
        grid=(depth, d3 // tn),
        in_specs=[pl.BlockSpec((rows, d), lambda l, j: (0, 0)),
                  pl.BlockSpec((1, d, tn), lambda l, j: (l, 0, j)),
                  pl.BlockSpec((1, 1, tn), lambda l, j: (l, 0, j))],
        out_specs=pl.BlockSpec((1, rows, tn), lambda l, j: (l, 0, j)),
        out_shape=jax.ShapeDtypeStruct((depth, rows, d3), f32),
        compiler_params=pltpu.CompilerParams(vmem_limit_bytes=VMEM_LIMIT),
        name="adaln_mod",
    )(cond, w_mod, b_mod.reshape(depth, 1, d3))


def _rows_of(xs_refs):
    if len(xs_refs) == 1:
        return xs_refs[0][0]
    return jnp.where(pl.program_id(1) == 0, xs_refs[0][0], xs_refs[1][0])


def _row_specs(xs, skip=0):
    d = xs[0].shape[-1]
    if len(xs) == 1:
        return [pl.BlockSpec((1, ROW_TILE, d), lambda i, j: (i, j + skip, 0))]
    assert skip == 0 and xs[0].shape[1] == ROW_TILE
    return [pl.BlockSpec((1, ROW_TILE, d), lambda i, j: (i, 0, 0)),
            pl.BlockSpec((1, ROW_TILE, d), lambda i, j: (i, jnp.maximum(j - 1, 0), 0))]


def _inproj_kernel(*refs, n_x, d_ret, d_shift):
    xs_refs = refs[:n_x]
    mod_ref, nw_ref, w_ref, cos_ref, sin_ref, qkv_ref, rw_ref, g_ref = refs[n_x:]
    x = _rows_of(xs_refs)
    ms = jnp.mean(x * x, axis=-1, keepdims=True)
    m = mod_ref[0, 0]
    h = x * lax.rsqrt(ms + NORM_EPS) * nw_ref[...]
    h = (h * (1.0 + m[0:1]) + m[1:2]).astype(bf16)

    def proj(lo, hi):
        return _dg(h, w_ref[:, lo:hi], NN)

    cos = cos_ref[...]
    sin = sin_ref[...]
    lane = lax.broadcasted_iota(jnp.int32, cos.shape, 1)
    first = (lane % 32) < 16

    def rope(t):
        nl = t.shape[-1]
        swapped = jnp.where(first, pltpu.roll(t, nl - 16, 1), pltpu.roll(t, 16, 1))
        return t * cos + swapped * sin

    qkv_ref[0, :, 0:d_ret] = rope(proj(0, d_ret))
    qkv_ref[0, :, d_ret:2 * d_ret] = rope(proj(d_ret, 2 * d_ret)) * (HEAD ** -0.5)
    qkv_ref[0, :, 2 * d_ret:3 * d_ret] = proj(2 * d_ret, 3 * d_ret)
    g_ref[0, :, 0:d_ret] = proj(3 * d_ret, 4 * d_ret)
    o = 4 * d_ret
    rw_ref[0] = proj(o, o + d_shift)
    g_ref[0, :, d_ret:] = proj(o + d_shift, w_ref.shape[1])


def _inproj(xs, modv, nw, w, cos_t, sin_t, d_ret, d_shift, d_rwkv):
    b, d = xs[0].shape[0], xs[0].shape[-1]
    t = sum(a.shape[1] for a in xs)
    n_tiles = t // ROW_TILE
    kern = functools.partial(_inproj_kernel, n_x=len(xs), d_ret=d_ret, d_shift=d_shift)
    return pl.pallas_call(
        kern,
        grid=(b, n_tiles),
        in_specs=_row_specs(xs) + [
            pl.BlockSpec((1, 1, 8, d), lambda i, j: (i, jnp.minimum(j, 1), 0, 0)),
            pl.BlockSpec((1, d), lambda i, j: (0, 0)),
            pl.BlockSpec(w.shape, lambda i, j: (0, 0), pipeline_mode=pl.Buffered(1)),
            pl.BlockSpec((ROW_TILE, d_ret), lambda i, j: (j, 0)),
            pl.BlockSpec((ROW_TILE, d_ret), lambda i, j: (j, 0))],
        out_specs=[pl.BlockSpec((1, ROW_TILE, 3 * d_ret), lambda i, j: (i, j, 0)),
                   pl.BlockSpec((1, ROW_TILE, d_shift), lambda i, j: (i, j, 0)),
                   pl.BlockSpec((1, ROW_TILE, d_ret + d_rwkv), lambda i, j: (i, j, 0))],
        out_shape=[jax.ShapeDtypeStruct((b, t, 3 * d_ret), f32),
                   jax.ShapeDtypeStruct((b, t, d_shift), f32),
                   jax.ShapeDtypeStruct((b, t, d_ret + d_rwkv), f32)],
        compiler_params=pltpu.CompilerParams(vmem_limit_bytes=VMEM_LIMIT),
        name="in_proj",
    )(*xs, modv, nw, w, cos_t, sin_t)


def _scan_chunk(idx, forward, n_ctx_chunks, n_chunks):
    if forward:
        return idx
    return jnp.where(idx < n_ctx_chunks, n_ctx_chunks - 1 - idx, n_chunks + n_ctx_chunks - 1 - idx)


def _ret_kernel(lg_ref, q_ref, k_ref, v_ref, nw_ref, o_ref,
                s_ref, mask_ref, qd_ref, kd_ref, cd_ref, *, n_ctx_chunks, n_chunks):
    hh = pl.program_id(1)
    c = RET_C
    masks = _head_masks(PAIR)
    first = masks[0]
    ii = lax.broadcasted_iota(jnp.int32, (c, 2 * c), 0)
    nn = lax.broadcasted_iota(jnp.int32, (c, 2 * c), 1)
    ri = lax.broadcasted_iota(jnp.int32, (c, PAIR), 0).astype(f32)
    for d in range(2):
        rel = ((ii - nn % c) if d == 0 else (nn % c - ii)).astype(f32)
        for j in range(NP):
            lg0 = lg_ref[d, hh * 2 * NP + 2 * j]
            lg1 = lg_ref[d, hh * 2 * NP + 2 * j + 1]
            lgm = jnp.where(nn < c, lg0, lg1)
            lgv = jnp.where(first, lg0, lg1)
            mask_ref[d, j] = jnp.where(rel >= 0.0, jnp.exp(lgm * jnp.maximum(rel, 0.0)), 0.0)
            qpow = (ri + 1.0) if d == 0 else (c - ri)
            kpow = (c - 1.0 - ri) if d == 0 else ri
            qd_ref[d, j] = jnp.exp(lgv * qpow)
            kd_ref[d, j] = jnp.exp(lgv * kpow)
            cd_ref[d, j] = jnp.exp(jnp.broadcast_to(lgv * c, (HEAD, PAIR)))
    s_ref[...] = jnp.zeros(s_ref.shape, f32)
    o_ref[...] = jnp.zeros(o_ref.shape, f32)
    chains = [(d, j) for d in range(2) for j in range(NP)]

    def step(s, carry):
        rows, qp, kp, vp = {}, {}, {}, {}
        for d in range(2):
            ch = _scan_chunk(s, d == 0, n_ctx_chunks, n_chunks)
            r0 = pl.multiple_of(ch * c, c)
            rows[d] = pl.ds(r0, c)
            q = q_ref[0, rows[d], :]
            k = k_ref[0, rows[d], :]
            v = v_ref[0, rows[d], :]
            for j in range(NP):
                col = slice(j * PAIR, (j + 1) * PAIR)
                qp[d, j], kp[d, j], vp[d, j] = q[:, col], k[:, col], v[:, col]
        sc, st, vb = {}, {}, {}
        for ck in chains:
            sc[ck] = _mm(qp[ck], _bd(kp[ck], masks), NT, P_RET) * mask_ref[ck[0], ck[1]]
            st[ck] = s_ref[ck[0], ck[1]]
            vb[ck] = _bd(vp[ck], masks)
        for (d, j) in chains:
            ck = (d, j)
            o = _mm(sc[ck], vb[ck], NN, P_RET) + _mm(qp[ck] * qd_ref[d, j], _bd(st[ck], masks), NN, P_RET)
            s_ref[d, j] = st[ck] * cd_ref[d, j] + _mm(_heads_to_rows(kp[ck] * kd_ref[d, j]), vb[ck], TN, P_RET)
            o_ref[0, rows[d], j * PAIR:(j + 1) * PAIR] += o
        return carry

    lax.fori_loop(0, n_chunks, step, 0)

    gmat = _group_mat(HALF)
    nw = nw_ref[...]

    def norm_tile(i, carry):
        r0 = pl.multiple_of(i * ROW_TILE, ROW_TILE)
        o = o_ref[0, pl.ds(r0, ROW_TILE), :]
        ms = _group_sum(o * o, gmat) * (1.0 / HEAD)
        o_ref[0, pl.ds(r0, ROW_TILE), :] = o * lax.rsqrt(ms + NORM_EPS) * nw
        return carry

    lax.fori_loop(0, (n_chunks * c) // ROW_TILE, norm_tile, 0)


def _retention(qkv, log_gamma, norm_w, n_ctx):
    b, t, w3 = qkv.shape
    d_ret = w3 // 3
    nhalf = d_ret // HALF
    kern = functools.partial(_ret_kernel, n_ctx_chunks=n_ctx // RET_C, n_chunks=t // RET_C)
    blk = lambda off: pl.BlockSpec((1, t, HALF), lambda i, j, off=off: (i, 0, off * nhalf + j))
    return pl.pallas_call(
        kern,
        grid=(b, nhalf),
        in_specs=[pl.BlockSpec(memory_space=pltpu.SMEM),
                  blk(0), blk(1), blk(2),
                  pl.BlockSpec((1, HALF), lambda i, j: (0, j))],
        out_specs=pl.BlockSpec((1, t, HALF), lambda i, j: (i, 0, j)),
        out_shape=jax.ShapeDtypeStruct((b, t, d_ret), f32),
        scratch_shapes=[pltpu.VMEM((2, NP, HEAD, PAIR), f32),
                        pltpu.VMEM((2, NP, RET_C, 2 * RET_C), f32),
                        pltpu.VMEM((2, NP, RET_C, PAIR), f32),
                        pltpu.VMEM((2, NP, RET_C, PAIR), f32),
                        pltpu.VMEM((2, NP, HEAD, PAIR), f32)],
        compiler_params=pltpu.CompilerParams(vmem_limit_bytes=VMEM_LIMIT),
        name="retention",
    )(log_gamma, qkv, qkv, qkv, norm_w)


class _Pair:
    def __init__(self, x, masks):
        self.x = x
        self.masks = masks
        self._parts = None
        self._lhs = None
        self._bd = None

    def parts(self):
        if self._parts is None:
            hi = self.x.astype(bf16).astype(f32)
            self._parts = (hi, self.x - hi)
        return self._parts

    def lhs(self):
        if self._lhs is None:
            if P_RW == 1:
                self._lhs = self.x.astype(bf16)
            else:
                hi, lo = self.parts()
                self._lhs = jnp.concatenate([hi, hi, lo], axis=1).astype(bf16)
        return self._lhs

    def _bd_parts(self):
        if self._bd is None:
            if P_RW == 1:
                self._bd = (_bd(self.x, self.masks).astype(bf16),)
            else:
                hi, lo = self.parts()
                bh = _bd(hi, self.masks).astype(bf16)
                self._bd = (bh, _bd(lo, self.masks).astype(bf16), bh)
        return self._bd

    def rhs(self):
        return jnp.concatenate(self._bd_parts(), axis=0)

    def rhs_nt(self):
        return jnp.concatenate(self._bd_parts(), axis=1)

    def lhs_rows(self):
        if P_RW == 1:
            return _heads_to_rows(self.x).astype(bf16)
        hi, lo = self.parts()
        th = _heads_to_rows(hi)
        return jnp.concatenate([th, th, _heads_to_rows(lo)], axis=0).astype(bf16)


def _lhs_rows(x):
    if P_RW == 1:
        return x.astype(bf16)
    hi = x.astype(bf16).astype(f32)
    return jnp.concatenate([hi, hi, x - hi], axis=0).astype(bf16)


def _mmp(a, b):
    return _dg(a.lhs(), b.rhs(), NN)


def _rwkv_kernel(r_ref, k_ref, v_ref, xwa_ref, swr_ref, swk_ref, swv_ref, swx_ref,
                 w0_ref, w2_ref, a0_ref, a2_ref, kk_ref, ka_ref, rk_ref, lnw_ref, lnb_ref,
                 o_ref, bon_ref,
                 g_ref, h_ref, q_ref, y_ref, pc_ref,
                 *, n_ctx_chunks, n_chunks):
    c = RW_C
    t_rows = n_chunks * c
    gmat = _group_mat(HALF)
    rows = lax.broadcasted_iota(jnp.int32, (c, 1), 0)

    def conv(ref, sw_ref, ch, r0):
        x = ref[0, pl.ds(r0, c), :]
        p0 = pl.multiple_of(jnp.maximum(r0 - 8, 0), 8)
        n0 = pl.multiple_of(jnp.minimum(r0 + c, t_rows - 8), 8)
        prev8 = ref[0, pl.ds(p0, 8), :]
        next8 = ref[0, pl.ds(n0, 8), :]
        has_prev = jnp.logical_and(ch != 0, ch != n_ctx_chunks).astype(f32)
        has_next = jnp.logical_and(ch != n_ctx_chunks - 1, ch != n_chunks - 1).astype(f32)
        xd = jnp.where(rows == 0, prev8[7:8, :] * has_prev, pltpu.roll(x, 1, 0))
        xu = jnp.where(rows == c - 1, next8[0:1, :] * has_next, pltpu.roll(x, c - 1, 0))
        sw = sw_ref[...]
        return sw[0:1] * xd + sw[1:2] * x + sw[2:3] * xu

    masks = _head_masks(RW_W)
    ii = lax.broadcasted_iota(jnp.int32, (c, RW_W), 0)
    jj = lax.broadcasted_iota(jnp.int32, (c, RW_W), 1) % HEAD
    eye = jnp.where(ii == jj, 1.0, 0.0)
    bd16 = (ii // 16) == (jj // 16)
    same32 = (ii // 32) == (jj // 32)
    off32 = jnp.logical_and(same32, jnp.logical_not(bd16))
    off64 = jnp.logical_not(same32)
    strict = (jj < ii, jj > ii)
    incl = (jj <= ii, jj >= ii)
    both = tuple(jnp.concatenate([strict[d], incl[d]], axis=0) for d in range(2))
    i2 = lax.broadcasted_iota(jnp.int32, (c, c), 0)
    j2 = lax.broadcasted_iota(jnp.int32, (c, c), 1)
    tri = (jnp.where(j2 <= i2, 1.0, 0.0).astype(bf16), jnp.where(j2 >= i2, 1.0, 0.0).astype(bf16))
    lora = xwa_ref.shape[2] // 2
    wscale = -math.exp(-0.5)
    mk = lambda x: _Pair(x, masks)
    col = lambda j: slice(j * RW_W, (j + 1) * RW_W)

    n_groups = n_chunks // RW_UNROLL

    def features(grp):
        out = []
        rs, ks, vs, xs = [], [], [], []
        for u in range(RW_UNROLL):
            ch = grp * RW_UNROLL + u
            r0 = pl.multiple_of(ch * c, c)
            rs.append(conv(r_ref, swr_ref, ch, r0))
            ks.append(conv(k_ref, swk_ref, ch, r0))
            vs.append(conv(v_ref, swv_ref, ch, r0))
            xs.append(conv(xwa_ref, swx_ref, ch, r0))
        kks = [k * kk_ref[...] for k in ks]
        sums = _group_sum(jnp.concatenate([kk * kk for kk in kks]
                                          + [r * k * rk_ref[...] for r, k in zip(rs, ks)], axis=0), gmat)
        xall = jnp.concatenate(xs, axis=0)
        txw = jnp.tanh(xall[:, :lora])
        kaps = []
        for u in range(RW_UNROLL):
            ch = grp * RW_UNROLL + u
            kaps.append(kks[u] / jnp.maximum(jnp.sqrt(sums[u * c:(u + 1) * c]), 1e-12))
            bon_ref[pl.ds(pl.multiple_of(ch * c, c), c), :] = sums[(RW_UNROLL + u) * c:(RW_UNROLL + u + 1) * c] * vs[u]
            out.append(vs[u])
        for d in range(2):
            lw_all = wscale * jax.nn.sigmoid(w0_ref[d:d + 1, :] + _mm(txw, w2_ref[d], NN, P_RW))
            a_all = jax.nn.sigmoid(a0_ref[d:d + 1, :] + _mm(xall[:, lora:], a2_ref[d], NN, P_RW))
            for u in range(RW_UNROLL):
                ch = grp * RW_UNROLL + u
                lw = lw_all[u * c:(u + 1) * c]
                a = a_all[u * c:(u + 1) * c]
                kmod = ks[u] * (1.0 + (a - 1.0) * ka_ref[...])
                b = kaps[u] * a
                lp = _mm_sel(tri[d], lw)
                lpc = lp[c - 1:c, :] if d == 0 else lp[0:1, :]
                pinv = jnp.exp(-lp)
                ptail = jnp.exp(lpc - lp)
                pc_ref[d, pl.ds(pl.multiple_of(ch * 8, 8), 8), :] = jnp.broadcast_to(jnp.exp(lpc), (8, lp.shape[1]))
                out += [jnp.concatenate([kaps[u] * jnp.exp(lp - lw), rs[u] * jnp.exp(lp)], axis=0),
                        kmod * pinv, b * pinv, kmod * ptail, b * ptail]
        return out

    def prepare(it, carry):
        cur = features(it)
        feat = {}
        for u in range(RW_UNROLL):
            sl = pl.ds(pl.multiple_of((it * RW_UNROLL + u) * c, c), c)
            for d in range(2):
                kr, kd, bd, kh, bh = cur[RW_UNROLL + 5 * (d * RW_UNROLL + u):RW_UNROLL + 5 * (d * RW_UNROLL + u + 1)]
                feat[u, d] = dict(sl=sl, v=cur[u], kr=kr, rq=kr[c:], kd=kd, bd=bd, kh=kh, bh=bh)

        chains = [(u, d, j) for u in range(RW_UNROLL) for d in range(2) for j in range(RW_NP)]
        mkv, m_b, n_b, vp = {}, {}, {}, {}
        for ck in chains:
            u, d, j = ck
            f = feat[u, d]
            kr = mk(f["kr"][:, col(j)])
            mk_ = _dg(kr.lhs(), mk(f["kd"][:, col(j)]).rhs_nt(), NT)
            mb_ = _dg(kr.lhs(), mk(f["bd"][:, col(j)]).rhs_nt(), NT)
            vp[ck] = mk(f["v"][:, col(j)])
            mkv[ck] = _mmp(mk(jnp.where(both[d], mk_, 0.0)), vp[ck])
            m_b[ck] = jnp.where(strict[d], mb_[:c], 0.0)
            n_b[ck] = mk(jnp.where(incl[d], mb_[c:], 0.0))
        p, npow = {}, {}
        for ck in chains:
            nd = mk(jnp.where(bd16, -m_b[ck], 0.0))
            p[ck] = eye + nd.x
            npow[ck] = mk(_mmp(nd, nd))
        for level in range(2):
            for ck in chains:
                both_ = _mmp(mk(jnp.concatenate([p[ck], npow[ck].x], axis=0)), npow[ck])
                p[ck] = p[ck] + both_[:c]
                npow[ck] = mk(both_[c:])
        for ck in chains:
            p[ck] = p[ck] + _mmp(mk(p[ck]), npow[ck])
        for off in (off32, off64):
            tmp, pm = {}, {}
            for ck in chains:
                pm[ck] = mk(p[ck])
                tmp[ck] = mk(_mmp(pm[ck], mk(jnp.where(off, m_b[ck], 0.0))))
            for ck in chains:
                p[ck] = p[ck] - _mmp(tmp[ck], pm[ck])
        ta = {}
        for ck in chains:
            u, d, j = ck
            kq = feat[u, d]["kr"][:c, col(j)]
            ta[ck] = _dg(mk(p[ck]).lhs(), jnp.concatenate([mk(kq).rhs(), mk(mkv[ck][:c]).rhs()], axis=1), NN)
        for ck in chains:
            u, d, j = ck
            f = feat[u, d]
            a1, z0 = ta[ck][:, :RW_W], ta[ck][:, RW_W:]
            nb = _dg(n_b[ck].lhs(), jnp.concatenate([mk(a1).rhs(), mk(z0).rhs()], axis=1), NN)
            az = jnp.concatenate([_heads_to_rows(a1), _heads_to_rows(z0)], axis=1)
            bh = mk(f["bh"][:, col(j)])
            ab = _dg(_lhs_rows(az), bh.rhs(), TN)
            vk = _dg(vp[ck].lhs_rows(), mk(f["kh"][:, col(j)]).rhs(), TN)
            g_ref[d, f["sl"], col(j)] = -ab[:HEAD]
            h_ref[d, f["sl"], col(j)] = vk - ab[HEAD:]
            q_ref[d, f["sl"], col(j)] = f["rq"][:, col(j)] - nb[:, :RW_W]
            y_ref[d, f["sl"], col(j)] = mkv[ck][c:] - nb[:, RW_W:]
        return carry

    lax.fori_loop(0, n_groups, prepare, 0)

    o_ref[...] = jnp.zeros(o_ref.shape, f32)

    def scan(it, st):
        st = list(st)
        for u in range(SCAN_UNROLL):
            for d in range(2):
                ch = _scan_chunk(it * SCAN_UNROLL + u, d == 0, n_ctx_chunks, n_chunks)
                sl = pl.ds(pl.multiple_of(ch * c, c), c)
                pc = pc_ref[d, pl.ds(pl.multiple_of(ch * 8, 8), 8), :][0:1]
                g, h, q, y0 = g_ref[d, sl, :], h_ref[d, sl, :], q_ref[d, sl, :], y_ref[d, sl, :]
                for j in range(RW_NP):
                    s0 = st[d * RW_NP + j]
                    sp = mk(s0)
                    y = _dg(mk(q[:, col(j)]).lhs(), sp.rhs_nt(), NT) + y0[:, col(j)]
                    st[d * RW_NP + j] = (s0 * pc[:, col(j)] + _mmp(sp, mk(g[:, col(j)]))) + h[:, col(j)]
                    o_ref[0, sl, col(j)] += y
        return tuple(st)

    lax.fori_loop(0, n_chunks // SCAN_UNROLL, scan,
                  tuple(jnp.zeros((HEAD, RW_W), f32) for _ in range(2 * RW_NP)))

    lnw = lnw_ref[...]
    lnb = lnb_ref[...]

    def readout(i, carry):
        r0 = pl.multiple_of(i * ROW_TILE, ROW_TILE)
        sl = pl.ds(r0, ROW_TILE)
        y = o_ref[0, sl, :]
        mu = _group_sum(y, gmat) * (1.0 / HEAD)
        yc = y - mu
        var = _group_sum(yc * yc, gmat) * (1.0 / HEAD)
        o_ref[0, sl, :] = yc * lax.rsqrt(var + GN_EPS) * lnw + lnb + bon_ref[sl, :]
        return carry

    lax.fori_loop(0, t_rows // ROW_TILE, readout, 0)


def _rwkv(rw, shift_w, w0, w2, a0, a2, k_k, k_a, r_k, ln_w, ln_b, n_ctx):
    b, t, d_shift = rw.shape
    d_rwkv = w0.shape[-1]
    lora2 = d_shift - 3 * d_rwkv
    nhalf = d_rwkv // HALF
    xcol = (3 * d_rwkv) // lora2
    assert (t // RW_C) % RW_UNROLL == 0 and (t // RW_C) % SCAN_UNROLL == 0
    kern = functools.partial(_rwkv_kernel, n_ctx_chunks=n_ctx // RW_C, n_chunks=t // RW_C)
    seq = lambda off: pl.BlockSpec((1, t, HALF), lambda i, j, off=off: (i, 0, off * nhalf + j))
    swb = lambda off: pl.BlockSpec((3, HALF), lambda i, j, off=off: (0, off * nhalf + j))
    vec = lambda rows: pl.BlockSpec((rows, HALF), lambda i, j: (0, j))
    lor = pl.BlockSpec((2, lora2 // 2, HALF), lambda i, j: (0, 0, j))
    return pl.pallas_call(
        kern,
        grid=(b, nhalf),
        in_specs=[seq(0), seq(1), seq(2),
                  pl.BlockSpec((1, t, lora2), lambda i, j: (i, 0, xcol)),
                  swb(0), swb(1), swb(2),
                  pl.BlockSpec((3, lora2), lambda i, j: (0, xcol)),
                  vec(2), lor, vec(2), lor, vec(1), vec(1), vec(1), vec(1), vec(1)],
        out_specs=pl.BlockSpec((1, t, HALF), lambda i, j: (i, 0, j)),
        out_shape=jax.ShapeDtypeStruct((b, t, d_rwkv), f32),
        scratch_shapes=[pltpu.VMEM((t, HALF), f32)] + [pltpu.VMEM((2, t, HALF), f32)] * 4
        + [pltpu.VMEM((2, (t // RW_C) * 8, HALF), f32)],
        compiler_params=pltpu.CompilerParams(vmem_limit_bytes=VMEM_LIMIT),
        name="rwkv7",
    )(rw, rw, rw, rw, shift_w, shift_w, shift_w, shift_w,
      w0, w2, a0, a2, k_k, k_a, r_k, ln_w, ln_b)


def _outproj_kernel(*refs, n_x, final):
    ret_ref, rwo_ref, g_ref = refs[:3]
    xs_refs = refs[3:3 + n_x]
    mod_ref, w_ref, fw_ref, o_ref = refs[3 + n_x:]
    d_ret = ret_ref.shape[-1]
    g = g_ref[0]
    act = jnp.concatenate([ret_ref[0] * (g[:, :d_ret] * jax.nn.sigmoid(g[:, :d_ret])),
                           rwo_ref[0] * (g[:, d_ret:] * jax.nn.sigmoid(g[:, d_ret:]))], axis=1)
    mix = _dg(act.astype(bf16), w_ref[...], NN)
    xn = _rows_of(xs_refs) + mod_ref[0, 0][2:3] * mix
    if final:
        ms = jnp.mean(xn * xn, axis=-1, keepdims=True)
        xn = xn * lax.rsqrt(ms + NORM_EPS) * fw_ref[...]
    o_ref[0] = xn


def _outproj(ret, rwo, gates, xs, modv, w, fw, n_ctx, final):
    b, t, d_ret = ret.shape
    d = xs[0].shape[-1]
    d_rw = rwo.shape[-1]
    skip = (n_ctx // ROW_TILE) if final else 0
    n_tiles = t // ROW_TILE - skip
    row = lambda w_: pl.BlockSpec((1, ROW_TILE, w_), lambda i, j: (i, j + skip, 0))
    kern = functools.partial(_outproj_kernel, n_x=len(xs), final=final)
    return pl.pallas_call(
        kern,
        grid=(b, n_tiles),
        in_specs=[row(d_ret), row(d_rw), row(d_ret + d_rw)] + _row_specs(xs, skip) + [
            pl.BlockSpec((1, 1, 8, d), lambda i, j: (i, jnp.minimum(j + skip, 1), 0, 0)),
            pl.BlockSpec(w.shape, lambda i, j: (0, 0), pipeline_mode=pl.Buffered(1)),
            pl.BlockSpec((1, d), lambda i, j: (0, 0))],
        out_specs=pl.BlockSpec((1, ROW_TILE, d), lambda i, j: (i, j, 0)),
        out_shape=jax.ShapeDtypeStruct((b, n_tiles * ROW_TILE, d), f32),
        compiler_params=pltpu.CompilerParams(vmem_limit_bytes=VMEM_LIMIT),
        name="out_proj",
    )(ret, rwo, gates, *xs, modv, w, fw)


def _rope_tables(n_ctx, seq, d_ret):
    nf = HEAD // 4
    inv = ROPE_BASE ** (-jnp.arange(nf, dtype=f32) / nf)
    pos = jnp.arange(seq)
    row_pos = (pos // GRID_W).astype(f32)
    col_pos = (pos % GRID_W).astype(f32)
    ang_r = row_pos[:, None] * inv[None, :]
    ang_c = col_pos[:, None] * inv[None, :]
    cos_h = jnp.concatenate([jnp.cos(ang_r)] * 2 + [jnp.cos(ang_c)] * 2, axis=-1)
    sin_h = jnp.concatenate([-jnp.sin(ang_r), jnp.sin(ang_r), -jnp.sin(ang_c), jnp.sin(ang_c)], axis=-1)
    reps = d_ret // HEAD
    cos_l = jnp.tile(cos_h, (1, reps))
    sin_l = jnp.tile(sin_h, (1, reps))
    cos_t = jnp.concatenate([jnp.ones((n_ctx, d_ret), f32), cos_l], axis=0)
    sin_t = jnp.concatenate([jnp.zeros((n_ctx, d_ret), f32), sin_l], axis=0)
    return cos_t, sin_t


def kernel(x, c, ctx, c_ctx, norm_w, w_mod, b_mod, w_in, ret_log_gamma, ret_norm_w, rwkv_shift_w, rwkv_w0, rwkv_w2, rwkv_a0, rwkv_a2, rwkv_k_k, rwkv_k_a, rwkv_r_k, rwkv_ln_w, rwkv_ln_b, w_out, final_norm_w):
    b, seq, d = x.shape
    n_ctx = ctx.shape[1]
    depth = w_in.shape[0]
    d_ret = ret_norm_w.shape[-1]
    d_rwkv = rwkv_w0.shape[-1]
    d_shift = rwkv_shift_w.shape[-1]

    cond = jnp.zeros((16, d), f32).at[:b].set(c.astype(f32)).at[b].set(c_ctx.astype(f32))
    mods = _modulation(cond, w_mod, b_mod)
    cos_t, sin_t = _rope_tables(n_ctx, seq, d_ret)
    if depth > 1 and n_ctx == ROW_TILE:
        xs = (ctx.astype(f32), x.astype(f32))
    else:
        xs = (jnp.concatenate([ctx.astype(f32), x.astype(f32)], axis=1),)

    out = None
    for layer in range(depth):
        m = mods[layer].reshape(16, 3, d)
        m = jnp.stack([m[:, 1], m[:, 0], m[:, 2]], axis=1)
        lat = m[:b]
        cx = jnp.broadcast_to(m[b][None], (b, 3, d))
        modv = jnp.pad(jnp.stack([cx, lat], axis=1), ((0, 0), (0, 0), (0, 5), (0, 0)))

        qkv, rw, gates = _inproj(xs, modv, norm_w[layer][None], w_in[layer].astype(bf16), cos_t, sin_t,
                                 d_ret, d_shift, d_rwkv)
        ret = _retention(qkv, ret_log_gamma[layer], ret_norm_w[layer][None], n_ctx)
        rwo = _rwkv(rw, rwkv_shift_w[layer], rwkv_w0[layer], rwkv_w2[layer], rwkv_a0[layer],
                    rwkv_a2[layer], rwkv_k_k[layer][None], rwkv_k_a[layer][None],
                    rwkv_r_k[layer].reshape(1, d_rwkv), rwkv_ln_w[layer][None],
                    rwkv_ln_b[layer][None], n_ctx)
        final = layer == depth - 1
        res = _outproj(ret, rwo, gates, xs, modv, w_out[layer].astype(bf16), final_norm_w[None], n_ctx, final)
        if final:
            out = res
        else:
            xs = (res,)
    return out
```

```python
import functools
import math

import jax
import jax.numpy as jnp
from jax import lax
from jax.experimental import pallas as pl
from jax.experimental.pallas import tpu as pltpu

f32 = jnp.float32
bf16 = jnp.bfloat16

HEAD = 64
PAIR = 2 * HEAD
RET_C = 128
RW_C = 64
RW_UNROLL = 4
SCAN_UNROLL = 2
GRID_W = 64
ROPE_BASE = 10000.0
NORM_EPS = 1e-6
GN_EPS = 64e-5
ROW_TILE = 256
HALF = 256
RET_W = 512
NP = RET_W // PAIR
RW_W = 128
RW_NP = HALF // RW_W
VMEM_LIMIT = 56 * 1024 * 1024

NN = ((1,), (0,))
NT = ((1,), (1,))
TN = ((0,), (0,))

P_MOD = 3
P_RET = 1
P_RW = 1


def _dg(a, b, dims):
    return lax.dot_general(a, b, (dims, ((), ())), preferred_element_type=f32)


def _split2(a):
    hi = a.astype(bf16)
    lo = (a - hi.astype(f32)).astype(bf16)
    return hi, lo


def _mm(a, b, dims=NN, passes=3):
    if passes == 1:
        return _dg(a.astype(bf16), b.astype(bf16), dims)
    ah, al = _split2(a)
    bh, bl = _split2(b)
    return _dg(ah, bh, dims) + (_dg(ah, bl, dims) + _dg(al, bh, dims))


def _mm_sel(sel, x):
    h, l = _split2(x)
    return _dg(sel, h, NN) + _dg(sel, l, NN)


def _group_sum(x, gmat):
    h, l = _split2(x)
    return _dg(h, gmat, NN) + _dg(l, gmat, NN)


def _group_mat(n):
    ii = lax.broadcasted_iota(jnp.int32, (n, n), 0) // HEAD
    jj = lax.broadcasted_iota(jnp.int32, (n, n), 1) // HEAD
    return jnp.where(ii == jj, 1.0, 0.0).astype(bf16)


def _head_masks(width):
    blk = lax.broadcasted_iota(jnp.int32, (1, width), 1) // HEAD
    return [blk == h for h in range(width // HEAD)]


def _bd(x, masks):
    return jnp.concatenate([jnp.where(m, x, 0.0) for m in masks], axis=0)


def _heads_to_rows(x):
    w = x.shape[1]
    return jnp.concatenate([x[:, :HEAD]] + [pltpu.roll(x, w - h * HEAD, 1)[:, :HEAD]
                                            for h in range(1, w // HEAD)], axis=0)


def _mod_kernel(c_ref, w_ref, b_ref, o_ref):
    cnd = c_ref[...]
    s = cnd * jax.nn.sigmoid(cnd)
    o_ref[0] = _mm(s, w_ref[0], NN, P_MOD) + b_ref[0]


def _modulation(cond, w_mod, b_mod):
    depth, d, d3 = w_mod.shape
    rows = cond.shape[0]
    tn = 1024
    return pl.pallas_call(
        _mod_kernel,
        grid=(depth, d3 // tn),
        in_specs=[pl.BlockSpec((rows, d), lambda l, j: (0, 0)),
                  pl.BlockSpec((1, d, tn), lambda l, j: (l, 0, j)),
                  pl.BlockSpec((1, 1, tn), lambda l, j: (l, 0, j))],
        out_specs=pl.BlockSpec((1, rows, tn), lambda l, j: (l, 0, j)),
        out_shape=jax.ShapeDtypeStruct((depth, rows, d3), f32),
        compiler_params=pltpu.CompilerParams(vmem_limit_bytes=VMEM_LIMIT),
        name="adaln_mod",
    )(cond, w_mod, b_mod.reshape(depth, 1, d3))


def _rows_of(xs_refs):
    if len(xs_refs) == 1:
        return xs_refs[0][0]
    return jnp.where(pl.program_id(1) == 0, xs_refs[0][0], xs_refs[1][0])


def _row_specs(xs, skip=0):
    d = xs[0].shape[-1]
    if len(xs) == 1:
        return [pl.BlockSpec((1, ROW_TILE, d), lambda i, j: (i, j + skip, 0))]
    assert skip == 0 and xs[0].shape[1] == ROW_TILE
    return [pl.BlockSpec((1, ROW_TILE, d), lambda i, j: (i, 0, 0)),
            pl.BlockSpec((1, ROW_TILE, d), lambda i, j: (i, jnp.maximum(j - 1, 0), 0))]


def _inproj_kernel(*refs, n_x, d_ret, d_shift):
    xs_refs = refs[:n_x]
    mod_ref, nw_ref, w_ref, cos_ref, sin_ref, qkv_ref, rw_ref, g_ref = refs[n_x:]
    x = _rows_of(xs_refs)
    ms = jnp.mean(x * x, axis=-1, keepdims=True)
    m = mod_ref[0, 0]
    h = x * lax.rsqrt(ms + NORM_EPS) * nw_ref[...]
    h = (h * (1.0 + m[0:1]) + m[1:2]).astype(bf16)

    def proj(lo, hi):
        return _dg(h, w_ref[:, lo:hi], NN)

    cos = cos_ref[...]
    sin = sin_ref[...]
    lane = lax.broadcasted_iota(jnp.int32, cos.shape, 1)
    first = (lane % 32) < 16

    def rope(t):
        nl = t.shape[-1]
        swapped = jnp.where(first, pltpu.roll(t, nl - 16, 1), pltpu.roll(t, 16, 1))
        return t * cos + swapped * sin

    qkv_ref[0, :, 0:d_ret] = rope(proj(0, d_ret))
    qkv_ref[0, :, d_ret:2 * d_ret] = rope(proj(d_ret, 2 * d_ret)) * (HEAD ** -0.5)
    qkv_ref[0, :, 2 * d_ret:3 * d_ret] = proj(2 * d_ret, 3 * d_ret)
    g_ref[0, :, 0:d_ret] = proj(3 * d_ret, 4 * d_ret)
    o = 4 * d_ret
    rw_ref[0] = proj(o, o + d_shift)
    g_ref[0, :, d_ret:] = proj(o + d_shift, w_ref.shape[1])


def _inproj(xs, modv, nw, w, cos_t, sin_t, d_ret, d_shift, d_rwkv):
    b, d = xs[0].shape[0], xs[0].shape[-1]
    t = sum(a.shape[1] for a in xs)
    n_tiles = t // ROW_TILE
    kern = functools.partial(_inproj_kernel, n_x=len(xs), d_ret=d_ret, d_shift=d_shift)
    return pl.pallas_call(
        kern,
        grid=(b, n_tiles),
        in_specs=_row_specs(xs) + [
            pl.BlockSpec((1, 1, 8, d), lambda i, j: (i, jnp.minimum(j, 1), 0, 0)),
            pl.BlockSpec((1, d), lambda i, j: (0, 0)),
            pl.BlockSpec(w.shape, lambda i, j: (0, 0), pipeline_mode=pl.Buffered(1)),
            pl.BlockSpec((ROW_TILE, d_ret), lambda i, j: (j, 0)),
            pl.BlockSpec((ROW_TILE, d_ret), lambda i, j: (j, 0))],
        out_specs=[pl.BlockSpec((1, ROW_TILE, 3 * d_ret), lambda i, j: (i, j, 0)),
                   pl.BlockSpec((1, ROW_TILE, d_shift), lambda i, j: (i, j, 0)),
                   pl.BlockSpec((1, ROW_TILE, d_ret + d_rwkv), lambda i, j: (i, j, 0))],
        out_shape=[jax.ShapeDtypeStruct((b, t, 3 * d_ret), f32),
                   jax.ShapeDtypeStruct((b, t, d_shift), f32),
                   jax.ShapeDtypeStruct((b, t, d_ret + d_rwkv), f32)],
        compiler_params=pltpu.CompilerParams(vmem_limit_bytes=VMEM_LIMIT),
        name="in_proj",
    )(*xs, modv, nw, w, cos_t, sin_t)


def _scan_chunk(idx, forward, n_ctx_chunks, n_chunks):
    if forward:
        return idx
    return jnp.where(idx < n_ctx_chunks, n_ctx_chunks - 1 - idx, n_chunks + n_ctx_chunks - 1 - idx)


def _ret_kernel(lg_ref, q_ref, k_ref, v_ref, gate_ref, nw_ref, o_ref,
                acc_ref, s_ref, mask_ref, qd_ref, kd_ref, cd_ref, *, n_ctx_chunks, n_chunks):
    hh = pl.program_id(1)
    c = RET_C
    masks = _head_masks(PAIR)
    first = masks[0]
    ii = lax.broadcasted_iota(jnp.int32, (c, 2 * c), 0)
    nn = lax.broadcasted_iota(jnp.int32, (c, 2 * c), 1)
    ri = lax.broadcasted_iota(jnp.int32, (c, PAIR), 0).astype(f32)
    for d in range(2):
        rel = ((ii - nn % c) if d == 0 else (nn % c - ii)).astype(f32)
        for j in range(NP):
            lg0 = lg_ref[d, hh * 2 * NP + 2 * j]
            lg1 = lg_ref[d, hh * 2 * NP + 2 * j + 1]
            lgm = jnp.where(nn < c, lg0, lg1)
            lgv = jnp.where(first, lg0, lg1)
            mask_ref[d, j] = jnp.where(rel >= 0.0, jnp.exp(lgm * jnp.maximum(rel, 0.0)), 0.0)
            qpow = (ri + 1.0) if d == 0 else (c - ri)
            kpow = (c - 1.0 - ri) if d == 0 else ri
            qd_ref[d, j] = jnp.exp(lgv * qpow)
            kd_ref[d, j] = jnp.exp(lgv * kpow)
            cd_ref[d, j] = jnp.exp(jnp.broadcast_to(lgv * c, (HEAD, PAIR)))
    s_ref[...] = jnp.zeros(s_ref.shape, f32)
    acc_ref[...] = jnp.zeros(acc_ref.shape, f32)
    chains = [(d, j) for d in range(2) for j in range(NP)]

    def step(s, carry):
        rows, qp, kp, vp = {}, {}, {}, {}
        for d in range(2):
            ch = _scan_chunk(s, d == 0, n_ctx_chunks, n_chunks)
            r0 = pl.multiple_of(ch * c, c)
            rows[d] = pl.ds(r0, c)
            q = q_ref[0, rows[d], :]
            k = k_ref[0, rows[d], :]
            v = v_ref[0, rows[d], :]
            for j in range(NP):
                col = slice(j * PAIR, (j + 1) * PAIR)
                qp[d, j], kp[d, j], vp[d, j] = q[:, col], k[:, col], v[:, col]
        sc, st, vb = {}, {}, {}
        for ck in chains:
            sc[ck] = _mm(qp[ck], _bd(kp[ck], masks), NT, P_RET) * mask_ref[ck[0], ck[1]]
            st[ck] = s_ref[ck[0], ck[1]]
            vb[ck] = _bd(vp[ck], masks)
        for (d, j) in chains:
            ck = (d, j)
            o = _mm(sc[ck], vb[ck], NN, P_RET) + _mm(qp[ck] * qd_ref[d, j], _bd(st[ck], masks), NN, P_RET)
            s_ref[d, j] = st[ck] * cd_ref[d, j] + _mm(_heads_to_rows(kp[ck] * kd_ref[d, j]), vb[ck], TN, P_RET)
            acc_ref[rows[d], j * PAIR:(j + 1) * PAIR] += o
        return carry

    lax.fori_loop(0, n_chunks, step, 0)

    gmat = _group_mat(RET_W)
    nw = nw_ref[...]

    def norm_tile(i, carry):
        r0 = pl.multiple_of(i * ROW_TILE, ROW_TILE)
        o = acc_ref[pl.ds(r0, ROW_TILE), :]
        ms = _group_sum(o * o, gmat) * (1.0 / HEAD)
        g = gate_ref[0, pl.ds(r0, ROW_TILE), :]
        o_ref[0, pl.ds(r0, ROW_TILE), :] = ((o * lax.rsqrt(ms + NORM_EPS) * nw)
                                            * (g * jax.nn.sigmoid(g))).astype(bf16)
        return carry

    lax.fori_loop(0, (n_chunks * c) // ROW_TILE, norm_tile, 0)


def _retention(qkv, gates, log_gamma, norm_w, n_ctx):
    b, t, w3 = qkv.shape
    d_ret = w3 // 3
    nhalf = d_ret // RET_W
    kern = functools.partial(_ret_kernel, n_ctx_chunks=n_ctx // RET_C, n_chunks=t // RET_C)
    blk = lambda off: pl.BlockSpec((1, t, RET_W), lambda i, j, off=off: (i, 0, off * nhalf + j))
    return pl.pallas_call(
        kern,
        grid=(b, nhalf),
        in_specs=[pl.BlockSpec(memory_space=pltpu.SMEM),
                  blk(0), blk(1), blk(2),
                  pl.BlockSpec((1, t, RET_W), lambda i, j: (i, 0, j)),
                  pl.BlockSpec((1, RET_W), lambda i, j: (0, j))],
        out_specs=pl.BlockSpec((1, t, RET_W), lambda i, j: (i, 0, j)),
        out_shape=jax.ShapeDtypeStruct((b, t, d_ret), bf16),
        scratch_shapes=[pltpu.VMEM((t, RET_W), f32),
                        pltpu.VMEM((2, NP, HEAD, PAIR), f32),
                        pltpu.VMEM((2, NP, RET_C, 2 * RET_C), f32),
                        pltpu.VMEM((2, NP, RET_C, PAIR), f32),
                        pltpu.VMEM((2, NP, RET_C, PAIR), f32),
                        pltpu.VMEM((2, NP, HEAD, PAIR), f32)],
        compiler_params=pltpu.CompilerParams(vmem_limit_bytes=VMEM_LIMIT),
        name="retention",
    )(log_gamma, qkv, qkv, qkv, gates, norm_w)


class _Pair:
    def __init__(self, x, masks):
        self.x = x
        self.masks = masks
        self._parts = None
        self._lhs = None
        self._bd = None

    def parts(self):
        if self._parts is None:
            hi = self.x.astype(bf16).astype(f32)
            self._parts = (hi, self.x - hi)
        return self._parts

    def lhs(self):
        if self._lhs is None:
            if P_RW == 1:
                self._lhs = self.x.astype(bf16)
            else:
                hi, lo = self.parts()
                self._lhs = jnp.concatenate([hi, hi, lo], axis=1).astype(bf16)
        return self._lhs

    def _bd_parts(self):
        if self._bd is None:
            if P_RW == 1:
                self._bd = (_bd(self.x, self.masks).astype(bf16),)
            else:
                hi, lo = self.parts()
                bh = _bd(hi, self.masks).astype(bf16)
                self._bd = (bh, _bd(lo, self.masks).astype(bf16), bh)
        return self._bd

    def rhs(self):
        return jnp.concatenate(self._bd_parts(), axis=0)

    def rhs_nt(self):
        return jnp.concatenate(self._bd_parts(), axis=1)

    def lhs_rows(self):
        if P_RW == 1:
            return _heads_to_rows(self.x).astype(bf16)
        hi, lo = self.parts()
        th = _heads_to_rows(hi)
        return jnp.concatenate([th, th, _heads_to_rows(lo)], axis=0).astype(bf16)


def _lhs_rows(x):
    if P_RW == 1:
        return x.astype(bf16)
    hi = x.astype(bf16).astype(f32)
    return jnp.concatenate([hi, hi, x - hi], axis=0).astype(bf16)


def _mmp(a, b):
    return _dg(a.lhs(), b.rhs(), NN)


def _rwkv_kernel(r_ref, k_ref, v_ref, xwa_ref, swr_ref, swk_ref, swv_ref, swx_ref,
                 w0_ref, w2_ref, a0_ref, a2_ref, kk_ref, ka_ref, rk_ref, lnw_ref, lnb_ref, gate_ref,
                 o_ref, acc_ref, bon_ref,
                 g_ref, h_ref, q_ref, y_ref, pc_ref,
                 *, n_ctx_chunks, n_chunks):
    c = RW_C
    t_rows = n_chunks * c
    gmat = _group_mat(HALF)
    rows = lax.broadcasted_iota(jnp.int32, (c, 1), 0)

    def conv(ref, sw_ref, ch, r0):
        x = ref[0, pl.ds(r0, c), :]
        p0 = pl.multiple_of(jnp.maximum(r0 - 8, 0), 8)
        n0 = pl.multiple_of(jnp.minimum(r0 + c, t_rows - 8), 8)
        prev8 = ref[0, pl.ds(p0, 8), :]
        next8 = ref[0, pl.ds(n0, 8), :]
        has_prev = jnp.logical_and(ch != 0, ch != n_ctx_chunks).astype(f32)
        has_next = jnp.logical_and(ch != n_ctx_chunks - 1, ch != n_chunks - 1).astype(f32)
        xd = jnp.where(rows == 0, prev8[7:8, :] * has_prev, pltpu.roll(x, 1, 0))
        xu = jnp.where(rows == c - 1, next8[0:1, :] * has_next, pltpu.roll(x, c - 1, 0))
        sw = sw_ref[...]
        return sw[0:1] * xd + sw[1:2] * x + sw[2:3] * xu

    masks = _head_masks(RW_W)
    ii = lax.broadcasted_iota(jnp.int32, (c, RW_W), 0)
    jj = lax.broadcasted_iota(jnp.int32, (c, RW_W), 1) % HEAD
    eye = jnp.where(ii == jj, 1.0, 0.0)
    bd16 = (ii // 16) == (jj // 16)
    same32 = (ii // 32) == (jj // 32)
    off32 = jnp.logical_and(same32, jnp.logical_not(bd16))
    off64 = jnp.logical_not(same32)
    strict = (jj < ii, jj > ii)
    incl = (jj <= ii, jj >= ii)
    both = tuple(jnp.concatenate([strict[d], incl[d]], axis=0) for d in range(2))
    i2 = lax.broadcasted_iota(jnp.int32, (c, c), 0)
    j2 = lax.broadcasted_iota(jnp.int32, (c, c), 1)
    tri = (jnp.where(j2 <= i2, 1.0, 0.0).astype(bf16), jnp.where(j2 >= i2, 1.0, 0.0).astype(bf16))
    lora = xwa_ref.shape[2] // 2
    wscale = -math.exp(-0.5)
    mk = lambda x: _Pair(x, masks)
    col = lambda j: slice(j * RW_W, (j + 1) * RW_W)

    n_groups = n_chunks // RW_UNROLL

    def features(grp):
        out = []
        rs, ks, vs, xs = [], [], [], []
        for u in range(RW_UNROLL):
            ch = grp * RW_UNROLL + u
            r0 = pl.multiple_of(ch * c, c)
            rs.append(conv(r_ref, swr_ref, ch, r0))
            ks.append(conv(k_ref, swk_ref, ch, r0))
            vs.append(conv(v_ref, swv_ref, ch, r0))
            xs.append(conv(xwa_ref, swx_ref, ch, r0))
        kks = [k * kk_ref[...] for k in ks]
        sums = _group_sum(jnp.concatenate([kk * kk for kk in kks]
                                          + [r * k * rk_ref[...] for r, k in zip(rs, ks)], axis=0), gmat)
        xall = jnp.concatenate(xs, axis=0)
        txw = jnp.tanh(xall[:, :lora])
        kaps = []
        for u in range(RW_UNROLL):
            ch = grp * RW_UNROLL + u
            kaps.append(kks[u] / jnp.maximum(jnp.sqrt(sums[u * c:(u + 1) * c]), 1e-12))
            bon_ref[pl.ds(pl.multiple_of(ch * c, c), c), :] = sums[(RW_UNROLL + u) * c:(RW_UNROLL + u + 1) * c] * vs[u]
            out.append(vs[u])
        for d in range(2):
            lw_all = wscale * jax.nn.sigmoid(w0_ref[d:d + 1, :] + _mm(txw, w2_ref[d], NN, P_RW))
            a_all = jax.nn.sigmoid(a0_ref[d:d + 1, :] + _mm(xall[:, lora:], a2_ref[d], NN, P_RW))
            for u in range(RW_UNROLL):
                ch = grp * RW_UNROLL + u
                lw = lw_all[u * c:(u + 1) * c]
                a = a_all[u * c:(u + 1) * c]
                kmod = ks[u] * (1.0 + (a - 1.0) * ka_ref[...])
                b = kaps[u] * a
                lp = _mm_sel(tri[d], lw)
                lpc = lp[c - 1:c, :] if d == 0 else lp[0:1, :]
                pinv = jnp.exp(-lp)
                ptail = jnp.exp(lpc - lp)
                pc_ref[d, pl.ds(pl.multiple_of(ch * 8, 8), 8), :] = jnp.broadcast_to(jnp.exp(lpc), (8, lp.shape[1]))
                out += [jnp.concatenate([kaps[u] * jnp.exp(lp - lw), rs[u] * jnp.exp(lp)], axis=0),
                        kmod * pinv, b * pinv, kmod * ptail, b * ptail]
        return out

    def prepare(it, carry):
        cur = features(it)
        feat = {}
        for u in range(RW_UNROLL):
            sl = pl.ds(pl.multiple_of((it * RW_UNROLL + u) * c, c), c)
            for d in range(2):
                kr, kd, bd, kh, bh = cur[RW_UNROLL + 5 * (d * RW_UNROLL + u):RW_UNROLL + 5 * (d * RW_UNROLL + u + 1)]
                feat[u, d] = dict(sl=sl, v=cur[u], kr=kr, rq=kr[c:], kd=kd, bd=bd, kh=kh, bh=bh)

        chains = [(u, d, j) for u in range(RW_UNROLL) for d in range(2) for j in range(RW_NP)]
        mkv, m_b, n_b, vp = {}, {}, {}, {}
        for ck in chains:
            u, d, j = ck
            f = feat[u, d]
            kr = mk(f["kr"][:, col(j)])
            mk_ = _dg(kr.lhs(), mk(f["kd"][:, col(j)]).rhs_nt(), NT)
            mb_ = _dg(kr.lhs(), mk(f["bd"][:, col(j)]).rhs_nt(), NT)
            vp[ck] = mk(f["v"][:, col(j)])
            mkv[ck] = _mmp(mk(jnp.where(both[d], mk_, 0.0)), vp[ck])
            m_b[ck] = jnp.where(strict[d], mb_[:c], 0.0)
            n_b[ck] = mk(jnp.where(incl[d], mb_[c:], 0.0))
        p, npow = {}, {}
        for ck in chains:
            nd = mk(jnp.where(bd16, -m_b[ck], 0.0))
            p[ck] = eye + nd.x
            npow[ck] = mk(_mmp(nd, nd))
        for level in range(2):
            for ck in chains:
                both_ = _mmp(mk(jnp.concatenate([p[ck], npow[ck].x], axis=0)), npow[ck])
                p[ck] = p[ck] + both_[:c]
                npow[ck] = mk(both_[c:])
        for ck in chains:
            p[ck] = p[ck] + _mmp(mk(p[ck]), npow[ck])
        for off in (off32, off64):
            tmp, pm = {}, {}
            for ck in chains:
                pm[ck] = mk(p[ck])
                tmp[ck] = mk(_mmp(pm[ck], mk(jnp.where(off, m_b[ck], 0.0))))
            for ck in chains:
                p[ck] = p[ck] - _mmp(tmp[ck], pm[ck])
        ta = {}
        for ck in chains:
            u, d, j = ck
            kq = feat[u, d]["kr"][:c, col(j)]
            ta[ck] = _dg(mk(p[ck]).lhs(), jnp.concatenate([mk(kq).rhs(), mk(mkv[ck][:c]).rhs()], axis=1), NN)
        for ck in chains:
            u, d, j = ck
            f = feat[u, d]
            a1, z0 = ta[ck][:, :RW_W], ta[ck][:, RW_W:]
            nb = _dg(n_b[ck].lhs(), jnp.concatenate([mk(a1).rhs(), mk(z0).rhs()], axis=1), NN)
            az = jnp.concatenate([_heads_to_rows(a1), _heads_to_rows(z0)], axis=1)
            bh = mk(f["bh"][:, col(j)])
            ab = _dg(_lhs_rows(az), bh.rhs(), TN)
            vk = _dg(vp[ck].lhs_rows(), mk(f["kh"][:, col(j)]).rhs(), TN)
            g_ref[d, f["sl"], col(j)] = -ab[:HEAD]
            h_ref[d, f["sl"], col(j)] = vk - ab[HEAD:]
            q_ref[d, f["sl"], col(j)] = f["rq"][:, col(j)] - nb[:, :RW_W]
            y_ref[d, f["sl"], col(j)] = mkv[ck][c:] - nb[:, RW_W:]
        return carry

    lax.fori_loop(0, n_groups, prepare, 0)

    acc_ref[...] = jnp.zeros(acc_ref.shape, f32)

    def scan(it, st):
        st = list(st)
        for u in range(SCAN_UNROLL):
            for d in range(2):
                ch = _scan_chunk(it * SCAN_UNROLL + u, d == 0, n_ctx_chunks, n_chunks)
                sl = pl.ds(pl.multiple_of(ch * c, c), c)
                pc = pc_ref[d, pl.ds(pl.multiple_of(ch * 8, 8), 8), :][0:1]
                g, h, q, y0 = g_ref[d, sl, :], h_ref[d, sl, :], q_ref[d, sl, :], y_ref[d, sl, :]
                for j in range(RW_NP):
                    s0 = st[d * RW_NP + j]
                    sp = mk(s0)
                    y = _dg(mk(q[:, col(j)]).lhs(), sp.rhs_nt(), NT) + y0[:, col(j)]
                    st[d * RW_NP + j] = (s0 * pc[:, col(j)] + _mmp(sp, mk(g[:, col(j)]))) + h[:, col(j)]
                    acc_ref[sl, col(j)] += y
        return tuple(st)

    lax.fori_loop(0, n_chunks // SCAN_UNROLL, scan,
                  tuple(jnp.zeros((HEAD, RW_W), f32) for _ in range(2 * RW_NP)))

    lnw = lnw_ref[...]
    lnb = lnb_ref[...]

    def readout(i, carry):
        r0 = pl.multiple_of(i * ROW_TILE, ROW_TILE)
        sl = pl.ds(r0, ROW_TILE)
        y = acc_ref[sl, :]
        mu = _group_sum(y, gmat) * (1.0 / HEAD)
        yc = y - mu
        var = _group_sum(yc * yc, gmat) * (1.0 / HEAD)
        g = gate_ref[0, sl, :]
        o_ref[0, sl, :] = ((yc * lax.rsqrt(var + GN_EPS) * lnw + lnb + bon_ref[sl, :])
                           * (g * jax.nn.sigmoid(g))).astype(bf16)
        return carry

    lax.fori_loop(0, t_rows // ROW_TILE, readout, 0)


def _rwkv(rw, gates, shift_w, w0, w2, a0, a2, k_k, k_a, r_k, ln_w, ln_b, n_ctx):
    b, t, d_shift = rw.shape
    d_rwkv = w0.shape[-1]
    lora2 = d_shift - 3 * d_rwkv
    nhalf = d_rwkv // HALF
    xcol = (3 * d_rwkv) // lora2
    assert (t // RW_C) % RW_UNROLL == 0 and (t // RW_C) % SCAN_UNROLL == 0
    kern = functools.partial(_rwkv_kernel, n_ctx_chunks=n_ctx // RW_C, n_chunks=t // RW_C)
    seq = lambda off: pl.BlockSpec((1, t, HALF), lambda i, j, off=off: (i, 0, off * nhalf + j))
    swb = lambda off: pl.BlockSpec((3, HALF), lambda i, j, off=off: (0, off * nhalf + j))
    vec = lambda rows: pl.BlockSpec((rows, HALF), lambda i, j: (0, j))
    lor = pl.BlockSpec((2, lora2 // 2, HALF), lambda i, j: (0, 0, j))
    return pl.pallas_call(
        kern,
        grid=(b, nhalf),
        in_specs=[seq(0), seq(1), seq(2),
                  pl.BlockSpec((1, t, lora2), lambda i, j: (i, 0, xcol)),
                  swb(0), swb(1), swb(2),
                  pl.BlockSpec((3, lora2), lambda i, j: (0, xcol)),
                  vec(2), lor, vec(2), lor, vec(1), vec(1), vec(1), vec(1), vec(1),
                  pl.BlockSpec((1, t, HALF), lambda i, j: (i, 0, gates.shape[-1] // HALF - nhalf + j))],
        out_specs=pl.BlockSpec((1, t, HALF), lambda i, j: (i, 0, j)),
        out_shape=jax.ShapeDtypeStruct((b, t, d_rwkv), bf16),
        scratch_shapes=[pltpu.VMEM((t, HALF), f32)] * 2 + [pltpu.VMEM((2, t, HALF), f32)] * 4
        + [pltpu.VMEM((2, (t // RW_C) * 8, HALF), f32)],
        compiler_params=pltpu.CompilerParams(vmem_limit_bytes=VMEM_LIMIT),
        name="rwkv7",
    )(rw, rw, rw, rw, shift_w, shift_w, shift_w, shift_w,
      w0, w2, a0, a2, k_k, k_a, r_k, ln_w, ln_b, gates)


def _outproj_kernel(*refs, n_x, final):
    ret_ref, rwo_ref = refs[:2]
    xs_refs = refs[2:2 + n_x]
    mod_ref, w_ref, fw_ref, o_ref = refs[2 + n_x:]
    mix = _dg(jnp.concatenate([ret_ref[0], rwo_ref[0]], axis=1), w_ref[...], NN)
    xn = _rows_of(xs_refs) + mod_ref[0, 0][2:3] * mix
    if final:
        ms = jnp.mean(xn * xn, axis=-1, keepdims=True)
        xn = xn * lax.rsqrt(ms + NORM_EPS) * fw_ref[...]
    o_ref[0] = xn


def _outproj(ret, rwo, xs, modv, w, fw, n_ctx, final):
    b, t, d_ret = ret.shape
    d = xs[0].shape[-1]
    d_rw = rwo.shape[-1]
    skip = (n_ctx // ROW_TILE) if final else 0
    n_tiles = t // ROW_TILE - skip
    row = lambda w_: pl.BlockSpec((1, ROW_TILE, w_), lambda i, j: (i, j + skip, 0))
    kern = functools.partial(_outproj_kernel, n_x=len(xs), final=final)
    return pl.pallas_call(
        kern,
        grid=(b, n_tiles),
        in_specs=[row(d_ret), row(d_rw)] + _row_specs(xs, skip) + [
            pl.BlockSpec((1, 1, 8, d), lambda i, j: (i, jnp.minimum(j + skip, 1), 0, 0)),
            pl.BlockSpec(w.shape, lambda i, j: (0, 0), pipeline_mode=pl.Buffered(1)),
            pl.BlockSpec((1, d), lambda i, j: (0, 0))],
        out_specs=pl.BlockSpec((1, ROW_TILE, d), lambda i, j: (i, j, 0)),
        out_shape=jax.ShapeDtypeStruct((b, n_tiles * ROW_TILE, d), f32),
        compiler_params=pltpu.CompilerParams(vmem_limit_bytes=VMEM_LIMIT),
        name="out_proj",
    )(ret, rwo, *xs, modv, w, fw)


def _rope_tables(n_ctx, seq, d_ret):
    nf = HEAD // 4
    inv = ROPE_BASE ** (-jnp.arange(nf, dtype=f32) / nf)
    pos = jnp.arange(seq)
    row_pos = (pos // GRID_W).astype(f32)
    col_pos = (pos % GRID_W).astype(f32)
    ang_r = row_pos[:, None] * inv[None, :]
    ang_c = col_pos[:, None] * inv[None, :]
    cos_h = jnp.concatenate([jnp.cos(ang_r)] * 2 + [jnp.cos(ang_c)] * 2, axis=-1)
    sin_h = jnp.concatenate([-jnp.sin(ang_r), jnp.sin(ang_r), -jnp.sin(ang_c), jnp.sin(ang_c)], axis=-1)
    reps = d_ret // HEAD
    cos_l = jnp.tile(cos_h, (1, reps))
    sin_l = jnp.tile(sin_h, (1, reps))
    cos_t = jnp.concatenate([jnp.ones((n_ctx, d_ret), f32), cos_l], axis=0)
    sin_t = jnp.concatenate([jnp.zeros((n_ctx, d_ret), f32), sin_l], axis=0)
    return cos_t, sin_t


def kernel(x, c, ctx, c_ctx, norm_w, w_mod, b_mod, w_in, ret_log_gamma, ret_norm_w, rwkv_shift_w, rwkv_w0, rwkv_w2, rwkv_a0, rwkv_a2, rwkv_k_k, rwkv_k_a, rwkv_r_k, rwkv_ln_w, rwkv_ln_b, w_out, final_norm_w):
    b, seq, d = x.shape
    n_ctx = ctx.shape[1]
    depth = w_in.shape[0]
    d_ret = ret_norm_w.shape[-1]
    d_rwkv = rwkv_w0.shape[-1]
    d_shift = rwkv_shift_w.shape[-1]

    cond = jnp.zeros((16, d), f32).at[:b].set(c.astype(f32)).at[b].set(c_ctx.astype(f32))
    mods = _modulation(cond, w_mod, b_mod)
    cos_t, sin_t = _rope_tables(n_ctx, seq, d_ret)
    if depth > 1 and n_ctx == ROW_TILE:
        xs = (ctx.astype(f32), x.astype(f32))
    else:
        xs = (jnp.concatenate([ctx.astype(f32), x.astype(f32)], axis=1),)

    out = None
    for layer in range(depth):
        m = mods[layer].reshape(16, 3, d)
        m = jnp.stack([m[:, 1], m[:, 0], m[:, 2]], axis=1)
        lat = m[:b]
        cx = jnp.broadcast_to(m[b][None], (b, 3, d))
        modv = jnp.pad(jnp.stack([cx, lat], axis=1), ((0, 0), (0, 0), (0, 5), (0, 0)))

        qkv, rw, gates = _inproj(xs, modv, norm_w[layer][None], w_in[layer].astype(bf16), cos_t, sin_t,
                                 d_ret, d_shift, d_rwkv)
        ret = _retention(qkv, gates, ret_log_gamma[layer], ret_norm_w[layer][None], n_ctx)
        rwo = _rwkv(rw, gates, rwkv_shift_w[layer], rwkv_w0[layer], rwkv_w2[layer], rwkv_a0[layer],
                    rwkv_a2[layer], rwkv_k_k[layer][None], rwkv_k_a[layer][None],
                    rwkv_r_k[layer].reshape(1, d_rwkv), rwkv_ln_w[layer][None],
                    rwkv_ln_b[layer][None], n_ctx)
        final = layer == depth - 1
        res = _outproj(ret, rwo, xs, modv, w_out[layer].astype(bf16), final_norm_w[None], n_ctx, final)
        if final:
            out = res
        else:
            xs = (res,)
    return out
```

```python
import functools
import math

import jax
import jax.numpy as jnp
from jax import lax
from jax.experimental import pallas as pl
from jax.experimental.pallas import tpu as pltpu

f32 = jnp.float32
bf16 = jnp.bfloat16

HEAD = 64
PAIR = 2 * HEAD
RET_C = 128
RW_C = 64
RW_UNROLL = 4
SCAN_UNROLL = 4
GRID_W = 64
ROPE_BASE = 10000.0
NORM_EPS = 1e-6
GN_EPS = 64e-5
ROW_TILE = 256
HALF = 256
RET_W = 512
NP = RET_W // PAIR
RW_W = 128
RW_NP = HALF // RW_W
VMEM_LIMIT = 56 * 1024 * 1024

NN = ((1,), (0,))
NT = ((1,), (1,))
TN = ((0,), (0,))

P_MOD = 3
P_RET = 1
P_RW = 1


def _dg(a, b, dims):
    return lax.dot_general(a, b, (dims, ((), ())), preferred_element_type=f32)


def _split2(a):
    hi = a.astype(bf16)
    lo = (a - hi.astype(f32)).astype(bf16)
    return hi, lo


def _mm(a, b, dims=NN, passes=3):
    if passes == 1:
        return _dg(a.astype(bf16), b.astype(bf16), dims)
    ah, al = _split2(a)
    bh, bl = _split2(b)
    return _dg(ah, bh, dims) + (_dg(ah, bl, dims) + _dg(al, bh, dims))


def _mm_sel(sel, x):
    h, l = _split2(x)
    return _dg(sel, h, NN) + _dg(sel, l, NN)


def _group_sum(x, gmat):
    rows, width = x.shape
    cols = width // PAIR
    h, l = _split2(jnp.concatenate([x[:, c * PAIR:(c + 1) * PAIR] for c in range(cols)], axis=0))
    s = _dg(h, gmat, NN) + _dg(l, gmat, NN)
    return jnp.concatenate([s[c * rows:(c + 1) * rows] for c in range(cols)], axis=1)


def _group_mat():
    ii = lax.broadcasted_iota(jnp.int32, (PAIR, PAIR), 0) // HEAD
    jj = lax.broadcasted_iota(jnp.int32, (PAIR, PAIR), 1) // HEAD
    return jnp.where(ii == jj, 1.0, 0.0).astype(bf16)


def _head_masks(width):
    blk = lax.broadcasted_iota(jnp.int32, (1, width), 1) // HEAD
    return [blk == h for h in range(width // HEAD)]


def _bd(x, masks):
    return jnp.concatenate([jnp.where(m, x, 0.0) for m in masks], axis=0)


def _heads_to_rows(x):
    w = x.shape[1]
    return jnp.concatenate([x[:, :HEAD]] + [pltpu.roll(x, w - h * HEAD, 1)[:, :HEAD]
                                            for h in range(1, w // HEAD)], axis=0)


def _mod_kernel(c_ref, w_ref, b_ref, o_ref):
    cnd = c_ref[...]
    s = cnd * jax.nn.sigmoid(cnd)
    o_ref[0] = _mm(s, w_ref[0], NN, P_MOD) + b_ref[0]


def _modulation(cond, w_mod, b_mod):
    depth, d, d3 = w_mod.shape
    rows = cond.shape[0]
    tn = 1024
    return pl.pallas_call(
        _mod_kernel,
        grid=(depth, d3 // tn),
        in_specs=[pl.BlockSpec((rows, d), lambda l, j: (0, 0)),
                  pl.BlockSpec((1, d, tn), lambda l, j: (l, 0, j)),
                  pl.BlockSpec((1, 1, tn), lambda l, j: (l, 0, j))],
        out_specs=pl.BlockSpec((1, rows, tn), lambda l, j: (l, 0, j)),
        out_shape=jax.ShapeDtypeStruct((depth, rows, d3), f32),
        compiler_params=pltpu.CompilerParams(vmem_limit_bytes=VMEM_LIMIT),
        name="adaln_mod",
    )(cond, w_mod, b_mod.reshape(depth, 1, d3))


def _rows_of(xs_refs):
    if len(xs_refs) == 1:
        return xs_refs[0][0]
    return jnp.where(pl.program_id(1) == 0, xs_refs[0][0], xs_refs[1][0])


def _row_specs(xs, skip=0):
    d = xs[0].shape[-1]
    if len(xs) == 1:
        return [pl.BlockSpec((1, ROW_TILE, d), lambda i, j: (i, j + skip, 0))]
    assert skip == 0 and xs[0].shape[1] == ROW_TILE
    return [pl.BlockSpec((1, ROW_TILE, d), lambda i, j: (i, 0, 0)),
            pl.BlockSpec((1, ROW_TILE, d), lambda i, j: (i, jnp.maximum(j - 1, 0), 0))]


def _inproj_kernel(*refs, n_x, d_ret, d_shift):
    xs_refs = refs[:n_x]
    mod_ref, nw_ref, w_ref, cos_ref, sin_ref, qkv_ref, rw_ref, g_ref = refs[n_x:]
    x = _rows_of(xs_refs)
    ms = jnp.mean(x * x, axis=-1, keepdims=True)
    m = mod_ref[0, 0]
    h = x * lax.rsqrt(ms + NORM_EPS) * nw_ref[...]
    h = (h * (1.0 + m[0:1]) + m[1:2]).astype(bf16)

    def proj(lo, hi):
        return _dg(h, w_ref[:, lo:hi], NN)

    cos = cos_ref[...]
    sin = sin_ref[...]
    lane = lax.broadcasted_iota(jnp.int32, cos.shape, 1)
    first = (lane % 32) < 16

    def rope(t):
        nl = t.shape[-1]
        swapped = jnp.where(first, pltpu.roll(t, nl - 16, 1), pltpu.roll(t, 16, 1))
        return t * cos + swapped * sin

    qkv_ref[0, :, 0:d_ret] = rope(proj(0, d_ret))
    qkv_ref[0, :, d_ret:2 * d_ret] = rope(proj(d_ret, 2 * d_ret)) * (HEAD ** -0.5)
    qkv_ref[0, :, 2 * d_ret:3 * d_ret] = proj(2 * d_ret, 3 * d_ret)
    g_ref[0, :, 0:d_ret] = proj(3 * d_ret, 4 * d_ret)
    o = 4 * d_ret
    rw_ref[0] = proj(o, o + d_shift)
    g_ref[0, :, d_ret:] = proj(o + d_shift, w_ref.shape[1])


def _inproj(xs, modv, nw, w, cos_t, sin_t, d_ret, d_shift, d_rwkv):
    b, d = xs[0].shape[0], xs[0].shape[-1]
    t = sum(a.shape[1] for a in xs)
    n_tiles = t // ROW_TILE
    kern = functools.partial(_inproj_kernel, n_x=len(xs), d_ret=d_ret, d_shift=d_shift)
    return pl.pallas_call(
        kern,
        grid=(b, n_tiles),
        in_specs=_row_specs(xs) + [
            pl.BlockSpec((1, 1, 8, d), lambda i, j: (i, jnp.minimum(j, 1), 0, 0)),
            pl.BlockSpec((1, d), lambda i, j: (0, 0)),
            pl.BlockSpec(w.shape, lambda i, j: (0, 0), pipeline_mode=pl.Buffered(1)),
            pl.BlockSpec((ROW_TILE, d_ret), lambda i, j: (j, 0)),
            pl.BlockSpec((ROW_TILE, d_ret), lambda i, j: (j, 0))],
        out_specs=[pl.BlockSpec((1, ROW_TILE, 3 * d_ret), lambda i, j: (i, j, 0)),
                   pl.BlockSpec((1, ROW_TILE, d_shift), lambda i, j: (i, j, 0)),
                   pl.BlockSpec((1, ROW_TILE, d_ret + d_rwkv), lambda i, j: (i, j, 0))],
        out_shape=[jax.ShapeDtypeStruct((b, t, 3 * d_ret), f32),
                   jax.ShapeDtypeStruct((b, t, d_shift), f32),
                   jax.ShapeDtypeStruct((b, t, d_ret + d_rwkv), f32)],
        compiler_params=pltpu.CompilerParams(vmem_limit_bytes=VMEM_LIMIT),
        name="in_proj",
    )(*xs, modv, nw, w, cos_t, sin_t)


def _scan_chunk(idx, forward, n_ctx_chunks, n_chunks):
    if forward:
        return idx
    return jnp.where(idx < n_ctx_chunks, n_ctx_chunks - 1 - idx, n_chunks + n_ctx_chunks - 1 - idx)


def _ret_kernel(lg_ref, q_ref, k_ref, v_ref, gate_ref, nw_ref, o_ref,
                acc_ref, s_ref, mask_ref, qd_ref, kd_ref, cd_ref, *, n_ctx_chunks, n_chunks):
    hh = pl.program_id(1)
    c = RET_C
    masks = _head_masks(PAIR)
    first = masks[0]
    ii = lax.broadcasted_iota(jnp.int32, (c, 2 * c), 0)
    nn = lax.broadcasted_iota(jnp.int32, (c, 2 * c), 1)
    ri = lax.broadcasted_iota(jnp.int32, (c, PAIR), 0).astype(f32)
    for d in range(2):
        rel = ((ii - nn % c) if d == 0 else (nn % c - ii)).astype(f32)
        for j in range(NP):
            lg0 = lg_ref[d, hh * 2 * NP + 2 * j]
            lg1 = lg_ref[d, hh * 2 * NP + 2 * j + 1]
            lgm = jnp.where(nn < c, lg0, lg1)
            lgv = jnp.where(first, lg0, lg1)
            mask_ref[d, j] = jnp.where(rel >= 0.0, jnp.exp(lgm * jnp.maximum(rel, 0.0)), 0.0)
            qpow = (ri + 1.0) if d == 0 else (c - ri)
            kpow = (c - 1.0 - ri) if d == 0 else ri
            qd_ref[d, j] = jnp.exp(lgv * qpow)
            kd_ref[d, j] = jnp.exp(lgv * kpow)
            cd_ref[d, j] = jnp.exp(jnp.broadcast_to(lgv * c, (HEAD, PAIR)))
    s_ref[...] = jnp.zeros(s_ref.shape, f32)
    acc_ref[...] = jnp.zeros(acc_ref.shape, f32)
    chains = [(d, j) for d in range(2) for j in range(NP)]

    def step(s, carry):
        rows, qp, kp, vp = {}, {}, {}, {}
        for d in range(2):
            ch = _scan_chunk(s, d == 0, n_ctx_chunks, n_chunks)
            r0 = pl.multiple_of(ch * c, c)
            rows[d] = pl.ds(r0, c)
            q = q_ref[0, rows[d], :]
            k = k_ref[0, rows[d], :]
            v = v_ref[0, rows[d], :]
            for j in range(NP):
                col = slice(j * PAIR, (j + 1) * PAIR)
                qp[d, j], kp[d, j], vp[d, j] = q[:, col], k[:, col], v[:, col]
        sc, st, vb = {}, {}, {}
        for ck in chains:
            sc[ck] = _mm(qp[ck], _bd(kp[ck], masks), NT, P_RET) * mask_ref[ck[0], ck[1]]
            st[ck] = s_ref[ck[0], ck[1]]
            vb[ck] = _bd(vp[ck], masks)
        for (d, j) in chains:
            ck = (d, j)
            o = _mm(sc[ck], vb[ck], NN, P_RET) + _mm(qp[ck] * qd_ref[d, j], _bd(st[ck], masks), NN, P_RET)
            s_ref[d, j] = st[ck] * cd_ref[d, j] + _mm(_heads_to_rows(kp[ck] * kd_ref[d, j]), vb[ck], TN, P_RET)
            acc_ref[rows[d], j * PAIR:(j + 1) * PAIR] += o
        return carry

    lax.fori_loop(0, n_chunks, step, 0)

    gmat = _group_mat()
    nw = nw_ref[...]

    def norm_tile(i, carry):
        r0 = pl.multiple_of(i * ROW_TILE, ROW_TILE)
        o = acc_ref[pl.ds(r0, ROW_TILE), :]
        ms = _group_sum(o * o, gmat) * (1.0 / HEAD)
        g = gate_ref[0, pl.ds(r0, ROW_TILE), :]
        o_ref[0, pl.ds(r0, ROW_TILE), :] = ((o * lax.rsqrt(ms + NORM_EPS) * nw)
                                            * (g * jax.nn.sigmoid(g))).astype(bf16)
        return carry

    lax.fori_loop(0, (n_chunks * c) // ROW_TILE, norm_tile, 0)


def _retention(qkv, gates, log_gamma, norm_w, n_ctx):
    b, t, w3 = qkv.shape
    d_ret = w3 // 3
    nhalf = d_ret // RET_W
    kern = functools.partial(_ret_kernel, n_ctx_chunks=n_ctx // RET_C, n_chunks=t // RET_C)
    blk = lambda off: pl.BlockSpec((1, t, RET_W), lambda i, j, off=off: (i, 0, off * nhalf + j))
    return pl.pallas_call(
        kern,
        grid=(b, nhalf),
        in_specs=[pl.BlockSpec(memory_space=pltpu.SMEM),
                  blk(0), blk(1), blk(2),
                  pl.BlockSpec((1, t, RET_W), lambda i, j: (i, 0, j)),
                  pl.BlockSpec((1, RET_W), lambda i, j: (0, j))],
        out_specs=pl.BlockSpec((1, t, RET_W), lambda i, j: (i, 0, j)),
        out_shape=jax.ShapeDtypeStruct((b, t, d_ret), bf16),
        scratch_shapes=[pltpu.VMEM((t, RET_W), f32),
                        pltpu.VMEM((2, NP, HEAD, PAIR), f32),
                        pltpu.VMEM((2, NP, RET_C, 2 * RET_C), f32),
                        pltpu.VMEM((2, NP, RET_C, PAIR), f32),
                        pltpu.VMEM((2, NP, RET_C, PAIR), f32),
                        pltpu.VMEM((2, NP, HEAD, PAIR), f32)],
        compiler_params=pltpu.CompilerParams(vmem_limit_bytes=VMEM_LIMIT),
        name="retention",
    )(log_gamma, qkv, qkv, qkv, gates, norm_w)


class _Pair:
    def __init__(self, x, masks):
        self.x = x
        self.masks = masks
        self._parts = None
        self._lhs = None
        self._bd = None

    def parts(self):
        if self._parts is None:
            hi = self.x.astype(bf16).astype(f32)
            self._parts = (hi, self.x - hi)
        return self._parts

    def lhs(self):
        if self._lhs is None:
            if P_RW == 1:
                self._lhs = self.x.astype(bf16)
            else:
                hi, lo = self.parts()
                self._lhs = jnp.concatenate([hi, hi, lo], axis=1).astype(bf16)
        return self._lhs

    def _bd_parts(self):
        if self._bd is None:
            if P_RW == 1:
                self._bd = (_bd(self.x, self.masks).astype(bf16),)
            else:
                hi, lo = self.parts()
                bh = _bd(hi, self.masks).astype(bf16)
                self._bd = (bh, _bd(lo, self.masks).astype(bf16), bh)
        return self._bd

    def rhs(self):
        return jnp.concatenate(self._bd_parts(), axis=0)

    def rhs_nt(self):
        return jnp.concatenate(self._bd_parts(), axis=1)

    def lhs_rows(self):
        if P_RW == 1:
            return _heads_to_rows(self.x).astype(bf16)
        hi, lo = self.parts()
        th = _heads_to_rows(hi)
        return jnp.concatenate([th, th, _heads_to_rows(lo)], axis=0).astype(bf16)


def _lhs_rows(x):
    if P_RW == 1:
        return x.astype(bf16)
    hi = x.astype(bf16).astype(f32)
    return jnp.concatenate([hi, hi, x - hi], axis=0).astype(bf16)


def _mmp(a, b):
    return _dg(a.lhs(), b.rhs(), NN)


def _rwkv_kernel(r_ref, k_ref, v_ref, xwa_ref, swr_ref, swk_ref, swv_ref, swx_ref,
                 w0_ref, w2_ref, a0_ref, a2_ref, kk_ref, ka_ref, rk_ref, lnw_ref, lnb_ref, gate_ref,
                 o_ref, acc_ref, bon_ref,
                 g_ref, h_ref, q_ref, y_ref, pc_ref,
                 *, n_ctx_chunks, n_chunks):
    c = RW_C
    t_rows = n_chunks * c
    gmat = _group_mat()
    rows = lax.broadcasted_iota(jnp.int32, (c, 1), 0)

    def conv(ref, sw_ref, ch, r0):
        x = ref[0, pl.ds(r0, c), :]
        p0 = pl.multiple_of(jnp.maximum(r0 - 8, 0), 8)
        n0 = pl.multiple_of(jnp.minimum(r0 + c, t_rows - 8), 8)
        prev8 = ref[0, pl.ds(p0, 8), :]
        next8 = ref[0, pl.ds(n0, 8), :]
        has_prev = jnp.logical_and(ch != 0, ch != n_ctx_chunks).astype(f32)
        has_next = jnp.logical_and(ch != n_ctx_chunks - 1, ch != n_chunks - 1).astype(f32)
        xd = jnp.where(rows == 0, prev8[7:8, :] * has_prev, pltpu.roll(x, 1, 0))
        xu = jnp.where(rows == c - 1, next8[0:1, :] * has_next, pltpu.roll(x, c - 1, 0))
        sw = sw_ref[...]
        return sw[0:1] * xd + sw[1:2] * x + sw[2:3] * xu

    masks = _head_masks(RW_W)
    ii = lax.broadcasted_iota(jnp.int32, (c, RW_W), 0)
    jj = lax.broadcasted_iota(jnp.int32, (c, RW_W), 1) % HEAD
    eye = jnp.where(ii == jj, 1.0, 0.0)
    bd16 = (ii // 16) == (jj // 16)
    same32 = (ii // 32) == (jj // 32)
    off32 = jnp.logical_and(same32, jnp.logical_not(bd16))
    off64 = jnp.logical_not(same32)
    strict = (jj < ii, jj > ii)
    incl = (jj <= ii, jj >= ii)
    both = tuple(jnp.concatenate([strict[d], incl[d]], axis=0) for d in range(2))
    i2 = lax.broadcasted_iota(jnp.int32, (c, c), 0)
    j2 = lax.broadcasted_iota(jnp.int32, (c, c), 1)
    tri = (jnp.where(j2 <= i2, 1.0, 0.0).astype(bf16), jnp.where(j2 >= i2, 1.0, 0.0).astype(bf16))
    lora = xwa_ref.shape[2] // 2
    wscale = -math.exp(-0.5)
    mk = lambda x: _Pair(x, masks)
    col = lambda j: slice(j * RW_W, (j + 1) * RW_W)

    n_groups = n_chunks // RW_UNROLL

    def features(grp):
        out = []
        rs, ks, vs, xs = [], [], [], []
        for u in range(RW_UNROLL):
            ch = grp * RW_UNROLL + u
            r0 = pl.multiple_of(ch * c, c)
            rs.append(conv(r_ref, swr_ref, ch, r0))
            ks.append(conv(k_ref, swk_ref, ch, r0))
            vs.append(conv(v_ref, swv_ref, ch, r0))
            xs.append(conv(xwa_ref, swx_ref, ch, r0))
        kks = [k * kk_ref[...] for k in ks]
        sums = _group_sum(jnp.concatenate([kk * kk for kk in kks]
                                          + [r * k * rk_ref[...] for r, k in zip(rs, ks)], axis=0), gmat)
        xall = jnp.concatenate(xs, axis=0)
        txw = jnp.tanh(xall[:, :lora])
        kaps = []
        for u in range(RW_UNROLL):
            ch = grp * RW_UNROLL + u
            kaps.append(kks[u] / jnp.maximum(jnp.sqrt(sums[u * c:(u + 1) * c]), 1e-12))
            bon_ref[pl.ds(pl.multiple_of(ch * c, c), c), :] = sums[(RW_UNROLL + u) * c:(RW_UNROLL + u + 1) * c] * vs[u]
            out.append(vs[u])
        for d in range(2):
            lw_all = wscale * jax.nn.sigmoid(w0_ref[d:d + 1, :] + _mm(txw, w2_ref[d], NN, P_RW))
            a_all = jax.nn.sigmoid(a0_ref[d:d + 1, :] + _mm(xall[:, lora:], a2_ref[d], NN, P_RW))
            for u in range(RW_UNROLL):
                ch = grp * RW_UNROLL + u
                lw = lw_all[u * c:(u + 1) * c]
                a = a_all[u * c:(u + 1) * c]
                kmod = ks[u] * (1.0 + (a - 1.0) * ka_ref[...])
                b = kaps[u] * a
                lp = _mm_sel(tri[d], lw)
                lpc = lp[c - 1:c, :] if d == 0 else lp[0:1, :]
                pinv = jnp.exp(-lp)
                ptail = jnp.exp(lpc - lp)
                pc_ref[d, pl.ds(pl.multiple_of(ch * 8, 8), 8), :] = jnp.broadcast_to(jnp.exp(lpc), (8, lp.shape[1]))
                out += [jnp.concatenate([kaps[u] * jnp.exp(lp - lw), rs[u] * jnp.exp(lp)], axis=0),
                        kmod * pinv, b * pinv, kmod * ptail, b * ptail]
        return out

    def prepare(it, carry):
        cur = features(it)
        feat = {}
        for u in range(RW_UNROLL):
            sl = pl.ds(pl.multiple_of((it * RW_UNROLL + u) * c, c), c)
            for d in range(2):
                kr, kd, bd, kh, bh = cur[RW_UNROLL + 5 * (d * RW_UNROLL + u):RW_UNROLL + 5 * (d * RW_UNROLL + u + 1)]
                feat[u, d] = dict(sl=sl, v=cur[u], kr=kr, rq=kr[c:], kd=kd, bd=bd, kh=kh, bh=bh)

        chains = [(u, d, j) for u in range(RW_UNROLL) for d in range(2) for j in range(RW_NP)]
        mkv, m_b, n_b, vp = {}, {}, {}, {}
        for ck in chains:
            u, d, j = ck
            f = feat[u, d]
            kr = mk(f["kr"][:, col(j)])
            mk_ = _dg(kr.lhs(), mk(f["kd"][:, col(j)]).rhs_nt(), NT)
            mb_ = _dg(kr.lhs(), mk(f["bd"][:, col(j)]).rhs_nt(), NT)
            vp[ck] = mk(f["v"][:, col(j)])
            mkv[ck] = _mmp(mk(jnp.where(both[d], mk_, 0.0)), vp[ck])
            m_b[ck] = jnp.where(strict[d], mb_[:c], 0.0)
            n_b[ck] = mk(jnp.where(incl[d], mb_[c:], 0.0))
        p, npow = {}, {}
        for ck in chains:
            nd = mk(jnp.where(bd16, -m_b[ck], 0.0))
            p[ck] = eye + nd.x
            npow[ck] = mk(_mmp(nd, nd))
        for level in range(2):
            for ck in chains:
                both_ = _mmp(mk(jnp.concatenate([p[ck], npow[ck].x], axis=0)), npow[ck])
                p[ck] = p[ck] + both_[:c]
                npow[ck] = mk(both_[c:])
        for ck in chains:
            p[ck] = p[ck] + _mmp(mk(p[ck]), npow[ck])
        for off in (off32, off64):
            tmp, pm = {}, {}
            for ck in chains:
                pm[ck] = mk(p[ck])
                tmp[ck] = mk(_mmp(pm[ck], mk(jnp.where(off, m_b[ck], 0.0))))
            for ck in chains:
                p[ck] = p[ck] - _mmp(tmp[ck], pm[ck])
        ta = {}
        for ck in chains:
            u, d, j = ck
            kq = feat[u, d]["kr"][:c, col(j)]
            ta[ck] = _dg(mk(p[ck]).lhs(), jnp.concatenate([mk(kq).rhs(), mk(mkv[ck][:c]).rhs()], axis=1), NN)
        for ck in chains:
            u, d, j = ck
            f = feat[u, d]
            a1, z0 = ta[ck][:, :RW_W], ta[ck][:, RW_W:]
            nb = _dg(n_b[ck].lhs(), jnp.concatenate([mk(a1).rhs(), mk(z0).rhs()], axis=1), NN)
            az = jnp.concatenate([_heads_to_rows(a1), _heads_to_rows(z0)], axis=1)
            bh = mk(f["bh"][:, col(j)])
            ab = _dg(_lhs_rows(az), bh.rhs(), TN)
            vk = _dg(vp[ck].lhs_rows(), mk(f["kh"][:, col(j)]).rhs(), TN)
            g_ref[d, f["sl"], col(j)] = -ab[:HEAD]
            h_ref[d, f["sl"], col(j)] = vk - ab[HEAD:]
            q_ref[d, f["sl"], col(j)] = f["rq"][:, col(j)] - nb[:, :RW_W]
            y_ref[d, f["sl"], col(j)] = mkv[ck][c:] - nb[:, RW_W:]
        return carry

    lax.fori_loop(0, n_groups, prepare, 0)

    acc_ref[...] = jnp.zeros(acc_ref.shape, f32)

    def scan(it, st):
        st = list(st)
        for u in range(SCAN_UNROLL):
            for d in range(2):
                ch = _scan_chunk(it * SCAN_UNROLL + u, d == 0, n_ctx_chunks, n_chunks)
                sl = pl.ds(pl.multiple_of(ch * c, c), c)
                pc = pc_ref[d, pl.ds(pl.multiple_of(ch * 8, 8), 8), :][0:1]
                g, h, q, y0 = g_ref[d, sl, :], h_ref[d, sl, :], q_ref[d, sl, :], y_ref[d, sl, :]
                for j in range(RW_NP):
                    s0 = st[d * RW_NP + j]
                    sp = mk(s0)
                    y = _dg(mk(q[:, col(j)]).lhs(), sp.rhs_nt(), NT) + y0[:, col(j)]
                    st[d * RW_NP + j] = (s0 * pc[:, col(j)] + _mmp(sp, mk(g[:, col(j)]))) + h[:, col(j)]
                    acc_ref[sl, col(j)] += y
        return tuple(st)

    lax.fori_loop(0, n_chunks // SCAN_UNROLL, scan,
                  tuple(jnp.zeros((HEAD, RW_W), f32) for _ in range(2 * RW_NP)))

    lnw = lnw_ref[...]
    lnb = lnb_ref[...]

    n_tiles = t_rows // ROW_TILE
    tiles_per_step = 3 if n_tiles % 3 == 0 else 1

    def readout(i, carry):
        sls = [pl.ds(pl.multiple_of((i * tiles_per_step + u) * ROW_TILE, ROW_TILE), ROW_TILE)
               for u in range(tiles_per_step)]
        ys = [acc_ref[sl, :] for sl in sls]
        ycs = [y - _group_sum(y, gmat) * (1.0 / HEAD) for y in ys]
        vrs = [_group_sum(yc * yc, gmat) * (1.0 / HEAD) for yc in ycs]
        for sl, yc, var in zip(sls, ycs, vrs):
            g = gate_ref[0, sl, :]
            o_ref[0, sl, :] = ((yc * lax.rsqrt(var + GN_EPS) * lnw + lnb + bon_ref[sl, :])
                               * (g * jax.nn.sigmoid(g))).astype(bf16)
        return carry

    lax.fori_loop(0, n_tiles // tiles_per_step, readout, 0)


def _rwkv(rw, gates, shift_w, w0, w2, a0, a2, k_k, k_a, r_k, ln_w, ln_b, n_ctx):
    b, t, d_shift = rw.shape
    d_rwkv = w0.shape[-1]
    lora2 = d_shift - 3 * d_rwkv
    nhalf = d_rwkv // HALF
    xcol = (3 * d_rwkv) // lora2
    assert (t // RW_C) % RW_UNROLL == 0 and (t // RW_C) % SCAN_UNROLL == 0
    kern = functools.partial(_rwkv_kernel, n_ctx_chunks=n_ctx // RW_C, n_chunks=t // RW_C)
    seq = lambda off: pl.BlockSpec((1, t, HALF), lambda i, j, off=off: (i, 0, off * nhalf + j))
    swb = lambda off: pl.BlockSpec((3, HALF), lambda i, j, off=off: (0, off * nhalf + j))
    vec = lambda rows: pl.BlockSpec((rows, HALF), lambda i, j: (0, j))
    lor = pl.BlockSpec((2, lora2 // 2, HALF), lambda i, j: (0, 0, j))
    return pl.pallas_call(
        kern,
        grid=(b, nhalf),
        in_specs=[seq(0), seq(1), seq(2),
                  pl.BlockSpec((1, t, lora2), lambda i, j: (i, 0, xcol)),
                  swb(0), swb(1), swb(2),
                  pl.BlockSpec((3, lora2), lambda i, j: (0, xcol)),
                  vec(2), lor, vec(2), lor, vec(1), vec(1), vec(1), vec(1), vec(1),
                  pl.BlockSpec((1, t, HALF), lambda i, j: (i, 0, gates.shape[-1] // HALF - nhalf + j))],
        out_specs=pl.BlockSpec((1, t, HALF), lambda i, j: (i, 0, j)),
        out_shape=jax.ShapeDtypeStruct((b, t, d_rwkv), bf16),
        scratch_shapes=[pltpu.VMEM((t, HALF), f32)] * 2 + [pltpu.VMEM((2, t, HALF), f32)] * 4
        + [pltpu.VMEM((2, (t // RW_C) * 8, HALF), f32)],
        compiler_params=pltpu.CompilerParams(vmem_limit_bytes=VMEM_LIMIT),
        name="rwkv7",
    )(rw, rw, rw, rw, shift_w, shift_w, shift_w, shift_w,
      w0, w2, a0, a2, k_k, k_a, r_k, ln_w, ln_b, gates)


def _outproj_kernel(*refs, n_x, final):
    ret_ref, rwo_ref = refs[:2]
    xs_refs = refs[2:2 + n_x]
    mod_ref, w_ref, fw_ref, o_ref = refs[2 + n_x:]
    mix = _dg(jnp.concatenate([ret_ref[0], rwo_ref[0]], axis=1), w_ref[...], NN)
    xn = _rows_of(xs_refs) + mod_ref[0, 0][2:3] * mix
    if final:
        ms = jnp.mean(xn * xn, axis=-1, keepdims=True)
        xn = xn * lax.rsqrt(ms + NORM_EPS) * fw_ref[...]
    o_ref[0] = xn


def _outproj(ret, rwo, xs, modv, w, fw, n_ctx, final):
    b, t, d_ret = ret.shape
    d = xs[0].shape[-1]
    d_rw = rwo.shape[-1]
    skip = (n_ctx // ROW_TILE) if final else 0
    n_tiles = t // ROW_TILE - skip
    row = lambda w_: pl.BlockSpec((1, ROW_TILE, w_), lambda i, j: (i, j + skip, 0))
    kern = functools.partial(_outproj_kernel, n_x=len(xs), final=final)
    return pl.pallas_call(
        kern,
        grid=(b, n_tiles),
        in_specs=[row(d_ret), row(d_rw)] + _row_specs(xs, skip) + [
            pl.BlockSpec((1, 1, 8, d), lambda i, j: (i, jnp.minimum(j + skip, 1), 0, 0)),
            pl.BlockSpec(w.shape, lambda i, j: (0, 0), pipeline_mode=pl.Buffered(1)),
            pl.BlockSpec((1, d), lambda i, j: (0, 0))],
        out_specs=pl.BlockSpec((1, ROW_TILE, d), lambda i, j: (i, j, 0)),
        out_shape=jax.ShapeDtypeStruct((b, n_tiles * ROW_TILE, d), f32),
        compiler_params=pltpu.CompilerParams(vmem_limit_bytes=VMEM_LIMIT),
        name="out_proj",
    )(ret, rwo, *xs, modv, w, fw)


def _rope_tables(n_ctx, seq, d_ret):
    nf = HEAD // 4
    inv = ROPE_BASE ** (-jnp.arange(nf, dtype=f32) / nf)
    pos = jnp.arange(seq)
    row_pos = (pos // GRID_W).astype(f32)
    col_pos = (pos % GRID_W).astype(f32)
    ang_r = row_pos[:, None] * inv[None, :]
    ang_c = col_pos[:, None] * inv[None, :]
    cos_h = jnp.concatenate([jnp.cos(ang_r)] * 2 + [jnp.cos(ang_c)] * 2, axis=-1)
    sin_h = jnp.concatenate([-jnp.sin(ang_r), jnp.sin(ang_r), -jnp.sin(ang_c), jnp.sin(ang_c)], axis=-1)
    reps = d_ret // HEAD
    cos_l = jnp.tile(cos_h, (1, reps))
    sin_l = jnp.tile(sin_h, (1, reps))
    cos_t = jnp.concatenate([jnp.ones((n_ctx, d_ret), f32), cos_l], axis=0)
    sin_t = jnp.concatenate([jnp.zeros((n_ctx, d_ret), f32), sin_l], axis=0)
    return cos_t, sin_t


def kernel(x, c, ctx, c_ctx, norm_w, w_mod, b_mod, w_in, ret_log_gamma, ret_norm_w, rwkv_shift_w, rwkv_w0, rwkv_w2, rwkv_a0, rwkv_a2, rwkv_k_k, rwkv_k_a, rwkv_r_k, rwkv_ln_w, rwkv_ln_b, w_out, final_norm_w):
    b, seq, d = x.shape
    n_ctx = ctx.shape[1]
    depth = w_in.shape[0]
    d_ret = ret_norm_w.shape[-1]
    d_rwkv = rwkv_w0.shape[-1]
    d_shift = rwkv_shift_w.shape[-1]

    cond = jnp.zeros((16, d), f32).at[:b].set(c.astype(f32)).at[b].set(c_ctx.astype(f32))
    mods = _modulation(cond, w_mod, b_mod)
    cos_t, sin_t = _rope_tables(n_ctx, seq, d_ret)
    if depth > 1 and n_ctx == ROW_TILE:
        xs = (ctx.astype(f32), x.astype(f32))
    else:
        xs = (jnp.concatenate([ctx.astype(f32), x.astype(f32)], axis=1),)

    out = None
    for layer in range(depth):
        m = mods[layer].reshape(16, 3, d)
        m = jnp.stack([m[:, 1], m[:, 0], m[:, 2]], axis=1)
        lat = m[:b]
        cx = jnp.broadcast_to(m[b][None], (b, 3, d))
        modv = jnp.pad(jnp.stack([cx, lat], axis=1), ((0, 0), (0, 0), (0, 5), (0, 0)))

        qkv, rw, gates = _inproj(xs, modv, norm_w[layer][None], w_in[layer].astype(bf16), cos_t, sin_t,
                                 d_ret, d_shift, d_rwkv)
        ret = _retention(qkv, gates, ret_log_gamma[layer], ret_norm_w[layer][None], n_ctx)
        rwo = _rwkv(rw, gates, rwkv_shift_w[layer], rwkv_w0[layer], rwkv_w2[layer], rwkv_a0[layer],
                    rwkv_a2[layer], rwkv_k_k[layer][None], rwkv_k_a[layer][None],
                    rwkv_r_k[layer].reshape(1, d_rwkv), rwkv_ln_w[layer][None],
                    rwkv_ln_b[layer][None], n_ctx)
        final = layer == depth - 1
        res = _outproj(ret, rwo, xs, modv, w_out[layer].astype(bf16), final_norm_w[None], n_ctx, final)
        if final:
            out = res
        else:
            xs = (res,)
    return out
```

```python
import functools
import math

import jax
import jax.numpy as jnp
from jax import lax
from jax.experimental import pallas as pl
from jax.experimental.pallas import tpu as pltpu

f32 = jnp.float32
bf16 = jnp.bfloat16

HEAD = 64
PAIR = 2 * HEAD
RET_C = 128
RW_C = 64
RW_UNROLL = 4
SCAN_UNROLL = 12
GRID_W = 64
ROPE_BASE = 10000.0
NORM_EPS = 1e-6
GN_EPS = 64e-5
ROW_TILE = 256
HALF = 256
RET_W = 512
NP = RET_W // PAIR
RW_W = 128
RW_NP = HALF // RW_W
VMEM_LIMIT = 56 * 1024 * 1024
NN = ((1,), (0,))
NT = ((1,), (1,))
TN = ((0,), (0,))

P_MOD = 3
P_RET = 1
P_RW = 1


def _dg(a, b, dims):
    return lax.dot_general(a, b, (dims, ((), ())), preferred_element_type=f32)


def _split2(a):
    hi = a.astype(bf16)
    lo = (a - hi.astype(f32)).astype(bf16)
    return hi, lo


def _mm(a, b, dims=NN, passes=3):
    if passes == 1:
        return _dg(a.astype(bf16), b.astype(bf16), dims)
    ah, al = _split2(a)
    bh, bl = _split2(b)
    return _dg(ah, bh, dims) + (_dg(ah, bl, dims) + _dg(al, bh, dims))


def _cumsum_rows(x, reverse):
    n = x.shape[0]
    row = lax.broadcasted_iota(jnp.int32, (n, 1), 0)
    s = 1
    while s < n:
        if reverse:
            x = x + jnp.where(row < n - s, pltpu.roll(x, n - s, 0), 0.0)
        else:
            x = x + jnp.where(row >= s, pltpu.roll(x, s, 0), 0.0)
        s *= 2
    return x


def _group_sum(x, gmat):
    rows, width = x.shape
    cols = width // PAIR
    h, l = _split2(jnp.concatenate([x[:, c * PAIR:(c + 1) * PAIR] for c in range(cols)], axis=0))
    s = _dg(h, gmat, NN) + _dg(l, gmat, NN)
    return jnp.concatenate([s[c * rows:(c + 1) * rows] for c in range(cols)], axis=1)


def _group_mat():
    ii = lax.broadcasted_iota(jnp.int32, (PAIR, PAIR), 0) // HEAD
    jj = lax.broadcasted_iota(jnp.int32, (PAIR, PAIR), 1) // HEAD
    return jnp.where(ii == jj, 1.0, 0.0).astype(bf16)


def _head_masks(width):
    blk = lax.broadcasted_iota(jnp.int32, (1, width), 1) // HEAD
    return [blk == h for h in range(width // HEAD)]


def _bd(x, masks):
    return jnp.concatenate([jnp.where(m, x, 0.0) for m in masks], axis=0)


def _heads_to_rows(x):
    w = x.shape[1]
    return jnp.concatenate([x[:, :HEAD]] + [pltpu.roll(x, w - h * HEAD, 1)[:, :HEAD]
                                            for h in range(1, w // HEAD)], axis=0)


def _mod_kernel(c_ref, w_ref, b_ref, o_ref):
    cnd = c_ref[...]
    s = cnd * jax.nn.sigmoid(cnd)
    o_ref[0] = _mm(s, w_ref[0], NN, P_MOD) + b_ref[0]


def _modulation(cond, w_mod, b_mod):
    depth, d, d3 = w_mod.shape
    rows = cond.shape[0]
    tn = 1024
    return pl.pallas_call(
        _mod_kernel,
        grid=(depth, d3 // tn),
        in_specs=[pl.BlockSpec((rows, d), lambda l, j: (0, 0)),
                  pl.BlockSpec((1, d, tn), lambda l, j: (l, 0, j)),
                  pl.BlockSpec((1, 1, tn), lambda l, j: (l, 0, j))],
        out_specs=pl.BlockSpec((1, rows, tn), lambda l, j: (l, 0, j)),
        out_shape=jax.ShapeDtypeStruct((depth, rows, d3), f32),
        compiler_params=pltpu.CompilerParams(vmem_limit_bytes=VMEM_LIMIT),
        name="adaln_mod",
    )(cond, w_mod, b_mod.reshape(depth, 1, d3))


def _rows_of(xs_refs):
    if len(xs_refs) == 1:
        return xs_refs[0][0]
    return jnp.where(pl.program_id(1) == 0, xs_refs[0][0], xs_refs[1][0])


def _row_specs(xs, skip=0):
    d = xs[0].shape[-1]
    if len(xs) == 1:
        return [pl.BlockSpec((1, ROW_TILE, d), lambda i, j: (i, j + skip, 0))]
    assert skip == 0 and xs[0].shape[1] == ROW_TILE
    return [pl.BlockSpec((1, ROW_TILE, d), lambda i, j: (i, 0, 0)),
            pl.BlockSpec((1, ROW_TILE, d), lambda i, j: (i, jnp.maximum(j - 1, 0), 0))]


def _inproj_kernel(*refs, n_x, d_ret, d_shift):
    xs_refs = refs[:n_x]
    mod_ref, nw_ref, w_ref, cos_ref, sin_ref, qkv_ref, rw_ref, g_ref = refs[n_x:]
    x = _rows_of(xs_refs)
    ms = jnp.mean(x * x, axis=-1, keepdims=True)
    m = mod_ref[0, 0]
    h = x * lax.rsqrt(ms + NORM_EPS) * nw_ref[...]
    h = (h * (1.0 + m[0:1]) + m[1:2]).astype(bf16)

    def proj(lo, hi):
        return _dg(h, w_ref[:, lo:hi], NN)

    cos = cos_ref[...]
    sin = sin_ref[...]
    lane = lax.broadcasted_iota(jnp.int32, cos.shape, 1)
    first = (lane % 32) < 16

    def rope(t):
        nl = t.shape[-1]
        swapped = jnp.where(first, pltpu.roll(t, nl - 16, 1), pltpu.roll(t, 16, 1))
        return t * cos + swapped * sin

    qkv_ref[0, :, 0:d_ret] = rope(proj(0, d_ret))
    qkv_ref[0, :, d_ret:2 * d_ret] = rope(proj(d_ret, 2 * d_ret)) * (HEAD ** -0.5)
    qkv_ref[0, :, 2 * d_ret:3 * d_ret] = proj(2 * d_ret, 3 * d_ret)
    g_ref[0, :, 0:d_ret] = proj(3 * d_ret, 4 * d_ret)
    o = 4 * d_ret
    rw_ref[0] = proj(o, o + d_shift)
    g_ref[0, :, d_ret:] = proj(o + d_shift, w_ref.shape[1])


def _inproj(xs, modv, nw, w, cos_t, sin_t, d_ret, d_shift, d_rwkv):
    b, d = xs[0].shape[0], xs[0].shape[-1]
    t = sum(a.shape[1] for a in xs)
    n_tiles = t // ROW_TILE
    kern = functools.partial(_inproj_kernel, n_x=len(xs), d_ret=d_ret, d_shift=d_shift)
    return pl.pallas_call(
        kern,
        grid=(b, n_tiles),
        in_specs=_row_specs(xs) + [
            pl.BlockSpec((1, 1, 8, d), lambda i, j: (i, jnp.minimum(j, 1), 0, 0)),
            pl.BlockSpec((1, d), lambda i, j: (0, 0)),
            pl.BlockSpec(w.shape, lambda i, j: (0, 0), pipeline_mode=pl.Buffered(1)),
            pl.BlockSpec((ROW_TILE, d_ret), lambda i, j: (j, 0)),
            pl.BlockSpec((ROW_TILE, d_ret), lambda i, j: (j, 0))],
        out_specs=[pl.BlockSpec((1, ROW_TILE, 3 * d_ret), lambda i, j: (i, j, 0)),
                   pl.BlockSpec((1, ROW_TILE, d_shift), lambda i, j: (i, j, 0)),
                   pl.BlockSpec((1, ROW_TILE, d_ret + d_rwkv), lambda i, j: (i, j, 0))],
        out_shape=[jax.ShapeDtypeStruct((b, t, 3 * d_ret), f32),
                   jax.ShapeDtypeStruct((b, t, d_shift), f32),
                   jax.ShapeDtypeStruct((b, t, d_ret + d_rwkv), f32)],
        compiler_params=pltpu.CompilerParams(vmem_limit_bytes=VMEM_LIMIT),
        name="in_proj",
    )(*xs, modv, nw, w, cos_t, sin_t)


def _scan_chunk(idx, forward, n_ctx_chunks, n_chunks):
    if forward:
        return idx
    return jnp.where(idx < n_ctx_chunks, n_ctx_chunks - 1 - idx, n_chunks + n_ctx_chunks - 1 - idx)


def _ret_kernel(lg_ref, q_ref, k_ref, v_ref, gate_ref, nw_ref, o_ref,
                acc_ref, s_ref, mask_ref, qd_ref, kd_ref, cd_ref, *, n_ctx_chunks, n_chunks):
    hh = pl.program_id(1)
    c = RET_C
    masks = _head_masks(PAIR)
    first = masks[0]
    ii = lax.broadcasted_iota(jnp.int32, (c, 2 * c), 0)
    nn = lax.broadcasted_iota(jnp.int32, (c, 2 * c), 1)
    ri = lax.broadcasted_iota(jnp.int32, (c, PAIR), 0).astype(f32)
    for d in range(2):
        rel = ((ii - nn % c) if d == 0 else (nn % c - ii)).astype(f32)
        for j in range(NP):
            lg0 = lg_ref[d, hh * 2 * NP + 2 * j]
            lg1 = lg_ref[d, hh * 2 * NP + 2 * j + 1]
            lgm = jnp.where(nn < c, lg0, lg1)
            lgv = jnp.where(first, lg0, lg1)
            mask_ref[d, j] = jnp.where(rel >= 0.0, jnp.exp(lgm * jnp.maximum(rel, 0.0)), 0.0)
            qpow = (ri + 1.0) if d == 0 else (c - ri)
            kpow = (c - 1.0 - ri) if d == 0 else ri
            qd_ref[d, j] = jnp.exp(lgv * qpow)
            kd_ref[d, j] = jnp.exp(lgv * kpow)
            cd_ref[d, j] = jnp.exp(jnp.broadcast_to(lgv * c, (HEAD, PAIR)))
    s_ref[...] = jnp.zeros(s_ref.shape, f32)
    acc_ref[...] = jnp.zeros(acc_ref.shape, f32)
    chains = [(d, j) for d in range(2) for j in range(NP)]

    def step(s, carry):
        rows, qp, kp, vp = {}, {}, {}, {}
        for d in range(2):
            ch = _scan_chunk(s, d == 0, n_ctx_chunks, n_chunks)
            r0 = pl.multiple_of(ch * c, c)
            rows[d] = pl.ds(r0, c)
            q = q_ref[0, rows[d], :]
            k = k_ref[0, rows[d], :]
            v = v_ref[0, rows[d], :]
            for j in range(NP):
                col = slice(j * PAIR, (j + 1) * PAIR)
                qp[d, j], kp[d, j], vp[d, j] = q[:, col], k[:, col], v[:, col]
        sc, st, vb = {}, {}, {}
        for ck in chains:
            sc[ck] = _mm(qp[ck], _bd(kp[ck], masks), NT, P_RET) * mask_ref[ck[0], ck[1]]
            st[ck] = s_ref[ck[0], ck[1]]
            vb[ck] = _bd(vp[ck], masks)
        for (d, j) in chains:
            ck = (d, j)
            o = _mm(sc[ck], vb[ck], NN, P_RET) + _mm(qp[ck] * qd_ref[d, j], _bd(st[ck], masks), NN, P_RET)
            s_ref[d, j] = st[ck] * cd_ref[d, j] + _mm(_heads_to_rows(kp[ck] * kd_ref[d, j]), vb[ck], TN, P_RET)
            acc_ref[rows[d], j * PAIR:(j + 1) * PAIR] += o
        return carry

    lax.fori_loop(0, n_chunks, step, 0)

    gmat = _group_mat()
    nw = nw_ref[...]

    def norm_tile(i, carry):
        r0 = pl.multiple_of(i * ROW_TILE, ROW_TILE)
        o = acc_ref[pl.ds(r0, ROW_TILE), :]
        ms = _group_sum(o * o, gmat) * (1.0 / HEAD)
        g = gate_ref[0, pl.ds(r0, ROW_TILE), :]
        o_ref[0, pl.ds(r0, ROW_TILE), :] = ((o * lax.rsqrt(ms + NORM_EPS) * nw)
                                            * (g * jax.nn.sigmoid(g))).astype(bf16)
        return carry

    lax.fori_loop(0, (n_chunks * c) // ROW_TILE, norm_tile, 0)


def _retention(qkv, gates, log_gamma, norm_w, n_ctx):
    b, t, w3 = qkv.shape
    d_ret = w3 // 3
    nhalf = d_ret // RET_W
    kern = functools.partial(_ret_kernel, n_ctx_chunks=n_ctx // RET_C, n_chunks=t // RET_C)
    blk = lambda off: pl.BlockSpec((1, t, RET_W), lambda i, j, off=off: (i, 0, off * nhalf + j))
    return pl.pallas_call(
        kern,
        grid=(b, nhalf),
        in_specs=[pl.BlockSpec(memory_space=pltpu.SMEM),
                  blk(0), blk(1), blk(2),
                  pl.BlockSpec((1, t, RET_W), lambda i, j: (i, 0, j)),
                  pl.BlockSpec((1, RET_W), lambda i, j: (0, j))],
        out_specs=pl.BlockSpec((1, t, RET_W), lambda i, j: (i, 0, j)),
        out_shape=jax.ShapeDtypeStruct((b, t, d_ret), bf16),
        scratch_shapes=[pltpu.VMEM((t, RET_W), f32),
                        pltpu.VMEM((2, NP, HEAD, PAIR), f32),
                        pltpu.VMEM((2, NP, RET_C, 2 * RET_C), f32),
                        pltpu.VMEM((2, NP, RET_C, PAIR), f32),
                        pltpu.VMEM((2, NP, RET_C, PAIR), f32),
                        pltpu.VMEM((2, NP, HEAD, PAIR), f32)],
        compiler_params=pltpu.CompilerParams(vmem_limit_bytes=VMEM_LIMIT),
        name="retention",
    )(log_gamma, qkv, qkv, qkv, gates, norm_w)


class _Pair:
    def __init__(self, x, masks):
        self.x = x
        self.masks = masks
        self._parts = None
        self._lhs = None
        self._bd = None

    def parts(self):
        if self._parts is None:
            hi = self.x.astype(bf16).astype(f32)
            self._parts = (hi, self.x - hi)
        return self._parts

    def lhs(self):
        if self._lhs is None:
            if P_RW == 1:
                self._lhs = self.x.astype(bf16)
            else:
                hi, lo = self.parts()
                self._lhs = jnp.concatenate([hi, hi, lo], axis=1).astype(bf16)
        return self._lhs

    def _bd_parts(self):
        if self._bd is None:
            if P_RW == 1:
                self._bd = (_bd(self.x, self.masks).astype(bf16),)
            else:
                hi, lo = self.parts()
                bh = _bd(hi, self.masks).astype(bf16)
                self._bd = (bh, _bd(lo, self.masks).astype(bf16), bh)
        return self._bd

    def rhs(self):
        return jnp.concatenate(self._bd_parts(), axis=0)

    def rhs_nt(self):
        return jnp.concatenate(self._bd_parts(), axis=1)

    def lhs_rows(self):
        if P_RW == 1:
            return _heads_to_rows(self.x).astype(bf16)
        hi, lo = self.parts()
        th = _heads_to_rows(hi)
        return jnp.concatenate([th, th, _heads_to_rows(lo)], axis=0).astype(bf16)


def _lhs_rows(x):
    if P_RW == 1:
        return x.astype(bf16)
    hi = x.astype(bf16).astype(f32)
    return jnp.concatenate([hi, hi, x - hi], axis=0).astype(bf16)


def _mmp(a, b):
    return _dg(a.lhs(), b.rhs(), NN)


def _rwkv_kernel(r_ref, k_ref, v_ref, xwa_ref, swr_ref, swk_ref, swv_ref, swx_ref,
                 w0_ref, w2_ref, a0_ref, a2_ref, kk_ref, ka_ref, rk_ref, lnw_ref, lnb_ref, gate_ref,
                 o_ref, acc_ref, bon_ref,
                 g_ref, h_ref, q_ref, y_ref, pc_ref,
                 *, n_ctx_chunks, n_chunks):
    c = RW_C
    t_rows = n_chunks * c
    gmat = _group_mat()
    rows = lax.broadcasted_iota(jnp.int32, (c, 1), 0)

    def conv(ref, sw_ref, ch, r0):
        x = ref[0, pl.ds(r0, c), :]
        p0 = pl.multiple_of(jnp.maximum(r0 - 8, 0), 8)
        n0 = pl.multiple_of(jnp.minimum(r0 + c, t_rows - 8), 8)
        prev8 = ref[0, pl.ds(p0, 8), :]
        next8 = ref[0, pl.ds(n0, 8), :]
        has_prev = jnp.logical_and(ch != 0, ch != n_ctx_chunks).astype(f32)
        has_next = jnp.logical_and(ch != n_ctx_chunks - 1, ch != n_chunks - 1).astype(f32)
        xd = jnp.where(rows == 0, prev8[7:8, :] * has_prev, pltpu.roll(x, 1, 0))
        xu = jnp.where(rows == c - 1, next8[0:1, :] * has_next, pltpu.roll(x, c - 1, 0))
        sw = sw_ref[...]
        return sw[0:1] * xd + sw[1:2] * x + sw[2:3] * xu

    masks = _head_masks(RW_W)
    ii = lax.broadcasted_iota(jnp.int32, (c, RW_W), 0)
    jj = lax.broadcasted_iota(jnp.int32, (c, RW_W), 1) % HEAD
    eye = jnp.where(ii == jj, 1.0, 0.0)
    bd16 = (ii // 16) == (jj // 16)
    same32 = (ii // 32) == (jj // 32)
    off32 = jnp.logical_and(same32, jnp.logical_not(bd16))
    off64 = jnp.logical_not(same32)
    strict = (jj < ii, jj > ii)
    incl = (jj <= ii, jj >= ii)
    both = tuple(jnp.concatenate([strict[d], incl[d]], axis=0) for d in range(2))
    lora = xwa_ref.shape[2] // 2
    wscale = -math.exp(-0.5)
    mk = lambda x: _Pair(x, masks)
    col = lambda j: slice(j * RW_W, (j + 1) * RW_W)

    n_groups = n_chunks // RW_UNROLL

    def features(grp):
        out = []
        rs, ks, vs, xs = [], [], [], []
        for u in range(RW_UNROLL):
            ch = grp * RW_UNROLL + u
            r0 = pl.multiple_of(ch * c, c)
            rs.append(conv(r_ref, swr_ref, ch, r0))
            ks.append(conv(k_ref, swk_ref, ch, r0))
            vs.append(conv(v_ref, swv_ref, ch, r0))
            xs.append(conv(xwa_ref, swx_ref, ch, r0))
        kks = [k * kk_ref[...] for k in ks]
        sums = _group_sum(jnp.concatenate([kk * kk for kk in kks]
                                          + [r * k * rk_ref[...] for r, k in zip(rs, ks)], axis=0), gmat)
        xall = jnp.concatenate(xs, axis=0)
        txw = jnp.tanh(xall[:, :lora])
        kaps = []
        for u in range(RW_UNROLL):
            ch = grp * RW_UNROLL + u
            kaps.append(kks[u] / jnp.maximum(jnp.sqrt(sums[u * c:(u + 1) * c]), 1e-12))
            bon_ref[pl.ds(pl.multiple_of(ch * c, c), c), :] = sums[(RW_UNROLL + u) * c:(RW_UNROLL + u + 1) * c] * vs[u]
            out.append(vs[u])
        for d in range(2):
            lw_all = wscale * jax.nn.sigmoid(w0_ref[d:d + 1, :] + _mm(txw, w2_ref[d], NN, P_RW))
            a_all = jax.nn.sigmoid(a0_ref[d:d + 1, :] + _mm(xall[:, lora:], a2_ref[d], NN, P_RW))
            for u in range(RW_UNROLL):
                ch = grp * RW_UNROLL + u
                lw = lw_all[u * c:(u + 1) * c]
                a = a_all[u * c:(u + 1) * c]
                kmod = ks[u] * (1.0 + (a - 1.0) * ka_ref[...])
                b = kaps[u] * a
                lp = _cumsum_rows(lw, reverse=(d == 1))
                lpc = lp[c - 1:c, :] if d == 0 else lp[0:1, :]
                pinv = jnp.exp(-lp)
                ptail = jnp.exp(lpc - lp)
                pc_ref[d, pl.ds(pl.multiple_of(ch * 8, 8), 8), :] = jnp.broadcast_to(jnp.exp(lpc), (8, lp.shape[1]))
                out += [jnp.concatenate([kaps[u] * jnp.exp(lp - lw), rs[u] * jnp.exp(lp)], axis=0),
                        kmod * pinv, b * pinv, kmod * ptail, b * ptail]
        return out

    def prepare(it, carry):
        cur = features(it)
        feat = {}
        for u in range(RW_UNROLL):
            sl = pl.ds(pl.multiple_of((it * RW_UNROLL + u) * c, c), c)
            for d in range(2):
                kr, kd, bd, kh, bh = cur[RW_UNROLL + 5 * (d * RW_UNROLL + u):RW_UNROLL + 5 * (d * RW_UNROLL + u + 1)]
                feat[u, d] = dict(sl=sl, v=cur[u], kr=kr, rq=kr[c:], kd=kd, bd=bd, kh=kh, bh=bh)

        chains = [(u, d, j) for u in range(RW_UNROLL) for d in range(2) for j in range(RW_NP)]
        mkv, m_b, n_b, vp = {}, {}, {}, {}
        for ck in chains:
            u, d, j = ck
            f = feat[u, d]
            kr = mk(f["kr"][:, col(j)])
            mk_ = _dg(kr.lhs(), mk(f["kd"][:, col(j)]).rhs_nt(), NT)
            mb_ = _dg(kr.lhs(), mk(f["bd"][:, col(j)]).rhs_nt(), NT)
            vp[ck] = mk(f["v"][:, col(j)])
            mkv[ck] = _mmp(mk(jnp.where(both[d], mk_, 0.0)), vp[ck])
            m_b[ck] = jnp.where(strict[d], mb_[:c], 0.0)
            n_b[ck] = mk(jnp.where(incl[d], mb_[c:], 0.0))
        p, npow = {}, {}
        for ck in chains:
            nd = mk(jnp.where(bd16, -m_b[ck], 0.0))
            p[ck] = eye + nd.x
            npow[ck] = mk(_mmp(nd, nd))
        for level in range(2):
            for ck in chains:
                both_ = _mmp(mk(jnp.concatenate([p[ck], npow[ck].x], axis=0)), npow[ck])
                p[ck] = p[ck] + both_[:c]
                npow[ck] = mk(both_[c:])
        for ck in chains:
            p[ck] = p[ck] + _mmp(mk(p[ck]), npow[ck])
        for off in (off32, off64):
            tmp, pm = {}, {}
            for ck in chains:
                pm[ck] = mk(p[ck])
                tmp[ck] = mk(_mmp(pm[ck], mk(jnp.where(off, m_b[ck], 0.0))))
            for ck in chains:
                p[ck] = p[ck] - _mmp(tmp[ck], pm[ck])
        ta = {}
        for ck in chains:
            u, d, j = ck
            kq = feat[u, d]["kr"][:c, col(j)]
            ta[ck] = _dg(mk(p[ck]).lhs(), jnp.concatenate([mk(kq).rhs(), mk(mkv[ck][:c]).rhs()], axis=1), NN)
        for ck in chains:
            u, d, j = ck
            f = feat[u, d]
            a1, z0 = ta[ck][:, :RW_W], ta[ck][:, RW_W:]
            nb = _dg(n_b[ck].lhs(), jnp.concatenate([mk(a1).rhs(), mk(z0).rhs()], axis=1), NN)
            az = jnp.concatenate([_heads_to_rows(a1), _heads_to_rows(z0)], axis=1)
            bh = mk(f["bh"][:, col(j)])
            ab = _dg(_lhs_rows(az), bh.rhs(), TN)
            vk = _dg(vp[ck].lhs_rows(), mk(f["kh"][:, col(j)]).rhs(), TN)
            g_ref[d, f["sl"], col(j)] = -ab[:HEAD]
            h_ref[d, f["sl"], col(j)] = vk - ab[HEAD:]
            q_ref[d, f["sl"], col(j)] = f["rq"][:, col(j)] - nb[:, :RW_W]
            y_ref[d, f["sl"], col(j)] = mkv[ck][c:] - nb[:, RW_W:]
        return carry

    lax.fori_loop(0, n_groups, prepare, 0)

    acc_ref[...] = jnp.zeros(acc_ref.shape, f32)

    def scan(it, st):
        st = list(st)
        for u in range(SCAN_UNROLL):
            for d in range(2):
                ch = _scan_chunk(it * SCAN_UNROLL + u, d == 0, n_ctx_chunks, n_chunks)
                sl = pl.ds(pl.multiple_of(ch * c, c), c)
                pc = pc_ref[d, pl.ds(pl.multiple_of(ch * 8, 8), 8), :][0:1]
                g, h, q, y0 = g_ref[d, sl, :], h_ref[d, sl, :], q_ref[d, sl, :], y_ref[d, sl, :]
                for j in range(RW_NP):
                    s0 = st[d * RW_NP + j]
                    sp = mk(s0)
                    y = _dg(mk(q[:, col(j)]).lhs(), sp.rhs_nt(), NT) + y0[:, col(j)]
                    st[d * RW_NP + j] = (s0 * pc[:, col(j)] + _mmp(sp, mk(g[:, col(j)]))) + h[:, col(j)]
                    acc_ref[sl, col(j)] += y
        return tuple(st)

    lax.fori_loop(0, n_chunks // SCAN_UNROLL, scan,
                  tuple(jnp.zeros((HEAD, RW_W), f32) for _ in range(2 * RW_NP)))

    lnw = lnw_ref[...]
    lnb = lnb_ref[...]

    n_tiles = t_rows // ROW_TILE
    tiles_per_step = 3 if n_tiles % 3 == 0 else 1

    def readout(i, carry):
        sls = [pl.ds(pl.multiple_of((i * tiles_per_step + u) * ROW_TILE, ROW_TILE), ROW_TILE)
               for u in range(tiles_per_step)]
        ys = [acc_ref[sl, :] for sl in sls]
        ycs = [y - _group_sum(y, gmat) * (1.0 / HEAD) for y in ys]
        vrs = [_group_sum(yc * yc, gmat) * (1.0 / HEAD) for yc in ycs]
        for sl, yc, var in zip(sls, ycs, vrs):
            g = gate_ref[0, sl, :]
            o_ref[0, sl, :] = ((yc * lax.rsqrt(var + GN_EPS) * lnw + lnb + bon_ref[sl, :])
                               * (g * jax.nn.sigmoid(g))).astype(bf16)
        return carry

    lax.fori_loop(0, n_tiles // tiles_per_step, readout, 0)


def _rwkv(rw, gates, shift_w, w0, w2, a0, a2, k_k, k_a, r_k, ln_w, ln_b, n_ctx):
    b, t, d_shift = rw.shape
    d_rwkv = w0.shape[-1]
    lora2 = d_shift - 3 * d_rwkv
    nhalf = d_rwkv // HALF
    xcol = (3 * d_rwkv) // lora2
    assert (t // RW_C) % RW_UNROLL == 0 and (t // RW_C) % SCAN_UNROLL == 0
    kern = functools.partial(_rwkv_kernel, n_ctx_chunks=n_ctx // RW_C, n_chunks=t // RW_C)
    seq = lambda off: pl.BlockSpec((1, t, HALF), lambda i, j, off=off: (i, 0, off * nhalf + j))
    swb = lambda off: pl.BlockSpec((3, HALF), lambda i, j, off=off: (0, off * nhalf + j))
    vec = lambda rows: pl.BlockSpec((rows, HALF), lambda i, j: (0, j))
    lor = pl.BlockSpec((2, lora2 // 2, HALF), lambda i, j: (0, 0, j))
    return pl.pallas_call(
        kern,
        grid=(b, nhalf),
        in_specs=[seq(0), seq(1), seq(2),
                  pl.BlockSpec((1, t, lora2), lambda i, j: (i, 0, xcol)),
                  swb(0), swb(1), swb(2),
                  pl.BlockSpec((3, lora2), lambda i, j: (0, xcol)),
                  vec(2), lor, vec(2), lor, vec(1), vec(1), vec(1), vec(1), vec(1),
                  pl.BlockSpec((1, t, HALF), lambda i, j: (i, 0, gates.shape[-1] // HALF - nhalf + j))],
        out_specs=pl.BlockSpec((1, t, HALF), lambda i, j: (i, 0, j)),
        out_shape=jax.ShapeDtypeStruct((b, t, d_rwkv), bf16),
        scratch_shapes=[pltpu.VMEM((t, HALF), f32)] * 2 + [pltpu.VMEM((2, t, HALF), f32)] * 4
        + [pltpu.VMEM((2, (t // RW_C) * 8, HALF), f32)],
        compiler_params=pltpu.CompilerParams(vmem_limit_bytes=VMEM_LIMIT),
        name="rwkv7",
    )(rw, rw, rw, rw, shift_w, shift_w, shift_w, shift_w,
      w0, w2, a0, a2, k_k, k_a, r_k, ln_w, ln_b, gates)


def _outproj_kernel(*refs, n_x, final):
    ret_ref, rwo_ref = refs[:2]
    xs_refs = refs[2:2 + n_x]
    mod_ref, w_ref, fw_ref, o_ref = refs[2 + n_x:]
    mix = _dg(jnp.concatenate([ret_ref[0], rwo_ref[0]], axis=1), w_ref[...], NN)
    xn = _rows_of(xs_refs) + mod_ref[0, 0][2:3] * mix
    if final:
        ms = jnp.mean(xn * xn, axis=-1, keepdims=True)
        xn = xn * lax.rsqrt(ms + NORM_EPS) * fw_ref[...]
    o_ref[0] = xn


def _outproj(ret, rwo, xs, modv, w, fw, n_ctx, final):
    b, t, d_ret = ret.shape
    d = xs[0].shape[-1]
    d_rw = rwo.shape[-1]
    skip = (n_ctx // ROW_TILE) if final else 0
    n_tiles = t // ROW_TILE - skip
    row = lambda w_: pl.BlockSpec((1, ROW_TILE, w_), lambda i, j: (i, j + skip, 0))
    kern = functools.partial(_outproj_kernel, n_x=len(xs), final=final)
    return pl.pallas_call(
        kern,
        grid=(b, n_tiles),
        in_specs=[row(d_ret), row(d_rw)] + _row_specs(xs, skip) + [
            pl.BlockSpec((1, 1, 8, d), lambda i, j: (i, jnp.minimum(j + skip, 1), 0, 0)),
            pl.BlockSpec(w.shape, lambda i, j: (0, 0), pipeline_mode=pl.Buffered(1)),
            pl.BlockSpec((1, d), lambda i, j: (0, 0))],
        out_specs=pl.BlockSpec((1, ROW_TILE, d), lambda i, j: (i, j, 0)),
        out_shape=jax.ShapeDtypeStruct((b, n_tiles * ROW_TILE, d), f32),
        compiler_params=pltpu.CompilerParams(vmem_limit_bytes=VMEM_LIMIT),
        name="out_proj",
    )(ret, rwo, *xs, modv, w, fw)


def _rope_tables(n_ctx, seq, d_ret):
    nf = HEAD // 4
    inv = ROPE_BASE ** (-jnp.arange(nf, dtype=f32) / nf)
    pos = jnp.arange(seq)
    row_pos = (pos // GRID_W).astype(f32)
    col_pos = (pos % GRID_W).astype(f32)
    ang_r = row_pos[:, None] * inv[None, :]
    ang_c = col_pos[:, None] * inv[None, :]
    cos_h = jnp.concatenate([jnp.cos(ang_r)] * 2 + [jnp.cos(ang_c)] * 2, axis=-1)
    sin_h = jnp.concatenate([-jnp.sin(ang_r), jnp.sin(ang_r), -jnp.sin(ang_c), jnp.sin(ang_c)], axis=-1)
    reps = d_ret // HEAD
    cos_l = jnp.tile(cos_h, (1, reps))
    sin_l = jnp.tile(sin_h, (1, reps))
    cos_t = jnp.concatenate([jnp.ones((n_ctx, d_ret), f32), cos_l], axis=0)
    sin_t = jnp.concatenate([jnp.zeros((n_ctx, d_ret), f32), sin_l], axis=0)
    return cos_t, sin_t


def kernel(x, c, ctx, c_ctx, norm_w, w_mod, b_mod, w_in, ret_log_gamma, ret_norm_w, rwkv_shift_w, rwkv_w0, rwkv_w2, rwkv_a0, rwkv_a2, rwkv_k_k, rwkv_k_a, rwkv_r_k, rwkv_ln_w, rwkv_ln_b, w_out, final_norm_w):
    b, seq, d = x.shape
    n_ctx = ctx.shape[1]
    depth = w_in.shape[0]
    d_ret = ret_norm_w.shape[-1]
    d_rwkv = rwkv_w0.shape[-1]
    d_shift = rwkv_shift_w.shape[-1]

    cond = jnp.zeros((16, d), f32).at[:b].set(c.astype(f32)).at[b].set(c_ctx.astype(f32))
    mods = _modulation(cond, w_mod, b_mod)
    cos_t, sin_t = _rope_tables(n_ctx, seq, d_ret)
    if depth > 1 and n_ctx == ROW_TILE:
        xs = (ctx.astype(f32), x.astype(f32))
    else:
        xs = (jnp.concatenate([ctx.astype(f32), x.astype(f32)], axis=1),)

    out = None
    for layer in range(depth):
        m = mods[layer].reshape(16, 3, d)
        m = jnp.stack([m[:, 1], m[:, 0], m[:, 2]], axis=1)
        lat = m[:b]
        cx = jnp.broadcast_to(m[b][None], (b, 3, d))
        modv = jnp.pad(jnp.stack([cx, lat], axis=1), ((0, 0), (0, 0), (0, 5), (0, 0)))

        qkv, rw, gates = _inproj(xs, modv, norm_w[layer][None], w_in[layer].astype(bf16), cos_t, sin_t,
                                 d_ret, d_shift, d_rwkv)
        ret = _retention(qkv, gates, ret_log_gamma[layer], ret_norm_w[layer][None], n_ctx)
        rwo = _rwkv(rw, gates, rwkv_shift_w[layer], rwkv_w0[layer], rwkv_w2[layer], rwkv_a0[layer],
                    rwkv_a2[layer], rwkv_k_k[layer][None], rwkv_k_a[layer][None],
                    rwkv_r_k[layer].reshape(1, d_rwkv), rwkv_ln_w[layer][None],
                    rwkv_ln_b[layer][None], n_ctx)
        final = layer == depth - 1
        res = _outproj(ret, rwo, xs, modv, w_out[layer].astype(bf16), final_norm_w[None], n_ctx, final)
        if final:
            out = res
        else:
            xs = (res,)
    return out
```

```python
import functools
import math

import jax
import jax.numpy as jnp
from jax import lax
from jax.experimental import pallas as pl
from jax.experimental.pallas import tpu as pltpu

f32 = jnp.float32
bf16 = jnp.bfloat16

HEAD = 64
PAIR = 2 * HEAD
RET_C = 128
RW_C = 64
RW_UNROLL = 4
SCAN_UNROLL = 12
GRID_W = 64
ROPE_BASE = 10000.0
NORM_EPS = 1e-6
GN_EPS = 64e-5
ROW_TILE = 256
HALF = 256
RET_W = 512
NP = RET_W // PAIR
RW_W = 128
RW_NP = HALF // RW_W
VMEM_LIMIT = 56 * 1024 * 1024

NN = ((1,), (0,))
NT = ((1,), (1,))
TN = ((0,), (0,))

P_MOD = 3
P_RET = 1
P_RW = 1
P_SUM = 1


def _dg(a, b, dims):
    return lax.dot_general(a, b, (dims, ((), ())), preferred_element_type=f32)


def _split2(a):
    hi = a.astype(bf16)
    lo = (a - hi.astype(f32)).astype(bf16)
    return hi, lo


def _mm(a, b, dims=NN, passes=3):
    if passes == 1:
        return _dg(a.astype(bf16), b.astype(bf16), dims)
    ah, al = _split2(a)
    bh, bl = _split2(b)
    return _dg(ah, bh, dims) + (_dg(ah, bl, dims) + _dg(al, bh, dims))


def _cumsum_rows(x, reverse):
    n = x.shape[0]
    row = lax.broadcasted_iota(jnp.int32, (n, 1), 0)
    s = 1
    while s < n:
        if reverse:
            x = x + jnp.where(row < n - s, pltpu.roll(x, n - s, 0), 0.0)
        else:
            x = x + jnp.where(row >= s, pltpu.roll(x, s, 0), 0.0)
        s *= 2
    return x


def _group_sum(x, gmat, pieces=P_SUM):
    rows, width = x.shape
    gw = gmat.shape[0]
    cols = width // gw
    xs = jnp.concatenate([x[:, c * gw:(c + 1) * gw] for c in range(cols)], axis=0)
    if pieces == 1:
        s = _dg(xs.astype(bf16), gmat, NN)
    else:
        h, l = _split2(xs)
        s = _dg(h, gmat, NN) + _dg(l, gmat, NN)
    return jnp.concatenate([s[c * rows:(c + 1) * rows] for c in range(cols)], axis=1)


def _group_mat():
    ii = lax.broadcasted_iota(jnp.int32, (PAIR, PAIR), 0) // HEAD
    jj = lax.broadcasted_iota(jnp.int32, (PAIR, PAIR), 1) // HEAD
    return jnp.where(ii == jj, 1.0, 0.0).astype(bf16)


def _head_masks(width):
    blk = lax.broadcasted_iota(jnp.int32, (1, width), 1) // HEAD
    return [blk == h for h in range(width // HEAD)]


def _bd(x, masks):
    return jnp.concatenate([jnp.where(m, x, 0.0) for m in masks], axis=0)


def _heads_to_rows(x):
    w = x.shape[1]
    return jnp.concatenate([x[:, :HEAD]] + [pltpu.roll(x, w - h * HEAD, 1)[:, :HEAD]
                                            for h in range(1, w // HEAD)], axis=0)


def _mod_kernel(c_ref, w_ref, b_ref, o_ref):
    cnd = c_ref[...]
    s = cnd * jax.nn.sigmoid(cnd)
    o_ref[0] = _mm(s, w_ref[0], NN, P_MOD) + b_ref[0]


def _modulation(cond, w_mod, b_mod):
    depth, d, d3 = w_mod.shape
    rows = cond.shape[0]
    tn = 1024
    return pl.pallas_call(
        _mod_kernel,
        grid=(depth, d3 // tn),
        in_specs=[pl.BlockSpec((rows, d), lambda l, j: (0, 0)),
                  pl.BlockSpec((1, d, tn), lambda l, j: (l, 0, j)),
                  pl.BlockSpec((1, 1, tn), lambda l, j: (l, 0, j))],
        out_specs=pl.BlockSpec((1, rows, tn), lambda l, j: (l, 0, j)),
        out_shape=jax.ShapeDtypeStruct((depth, rows, d3), f32),
        compiler_params=pltpu.CompilerParams(vmem_limit_bytes=VMEM_LIMIT),
        name="adaln_mod",
    )(cond, w_mod, b_mod.reshape(depth, 1, d3))


def _rows_of(xs_refs):
    if len(xs_refs) == 1:
        return xs_refs[0][0]
    return jnp.where(pl.program_id(1) == 0, xs_refs[0][0], xs_refs[1][0])


def _row_specs(xs, skip=0):
    d = xs[0].shape[-1]
    if len(xs) == 1:
        return [pl.BlockSpec((1, ROW_TILE, d), lambda i, j: (i, j + skip, 0))]
    assert skip == 0 and xs[0].shape[1] == ROW_TILE
    return [pl.BlockSpec((1, ROW_TILE, d), lambda i, j: (i, 0, 0)),
            pl.BlockSpec((1, ROW_TILE, d), lambda i, j: (i, jnp.maximum(j - 1, 0), 0))]


def _inproj_kernel(*refs, n_x, d_ret, d_shift):
    xs_refs = refs[:n_x]
    mod_ref, nw_ref, w_ref, cos_ref, sin_ref, qkv_ref, rw_ref, g_ref = refs[n_x:]
    x = _rows_of(xs_refs)
    ms = jnp.mean(x * x, axis=-1, keepdims=True)
    m = mod_ref[0, 0]
    h = x * lax.rsqrt(ms + NORM_EPS) * nw_ref[...]
    h = (h * (1.0 + m[0:1]) + m[1:2]).astype(bf16)

    def proj(lo, hi):
        return _dg(h, w_ref[:, lo:hi], NN)

    cos = cos_ref[...]
    sin = sin_ref[...]
    lane = lax.broadcasted_iota(jnp.int32, cos.shape, 1)
    first = (lane % 32) < 16

    def rope(t):
        nl = t.shape[-1]
        swapped = jnp.where(first, pltpu.roll(t, nl - 16, 1), pltpu.roll(t, 16, 1))
        return t * cos + swapped * sin

    qkv_ref[0, :, 0:d_ret] = rope(proj(0, d_ret))
    qkv_ref[0, :, d_ret:2 * d_ret] = rope(proj(d_ret, 2 * d_ret)) * (HEAD ** -0.5)
    qkv_ref[0, :, 2 * d_ret:3 * d_ret] = proj(2 * d_ret, 3 * d_ret)
    g_ref[0, :, 0:d_ret] = proj(3 * d_ret, 4 * d_ret)
    o = 4 * d_ret
    rw_ref[0] = proj(o, o + d_shift)
    g_ref[0, :, d_ret:] = proj(o + d_shift, w_ref.shape[1])


def _inproj(xs, modv, nw, w, cos_t, sin_t, d_ret, d_shift, d_rwkv):
    b, d = xs[0].shape[0], xs[0].shape[-1]
    t = sum(a.shape[1] for a in xs)
    n_tiles = t // ROW_TILE
    kern = functools.partial(_inproj_kernel, n_x=len(xs), d_ret=d_ret, d_shift=d_shift)
    return pl.pallas_call(
        kern,
        grid=(b, n_tiles),
        in_specs=_row_specs(xs) + [
            pl.BlockSpec((1, 1, 8, d), lambda i, j: (i, jnp.minimum(j, 1), 0, 0)),
            pl.BlockSpec((1, d), lambda i, j: (0, 0)),
            pl.BlockSpec(w.shape, lambda i, j: (0, 0), pipeline_mode=pl.Buffered(1)),
            pl.BlockSpec((ROW_TILE, d_ret), lambda i, j: (j, 0)),
            pl.BlockSpec((ROW_TILE, d_ret), lambda i, j: (j, 0))],
        out_specs=[pl.BlockSpec((1, ROW_TILE, 3 * d_ret), lambda i, j: (i, j, 0)),
                   pl.BlockSpec((1, ROW_TILE, d_shift), lambda i, j: (i, j, 0)),
                   pl.BlockSpec((1, ROW_TILE, d_ret + d_rwkv), lambda i, j: (i, j, 0))],
        out_shape=[jax.ShapeDtypeStruct((b, t, 3 * d_ret), f32),
                   jax.ShapeDtypeStruct((b, t, d_shift), f32),
                   jax.ShapeDtypeStruct((b, t, d_ret + d_rwkv), f32)],
        compiler_params=pltpu.CompilerParams(vmem_limit_bytes=VMEM_LIMIT),
        name="in_proj",
    )(*xs, modv, nw, w, cos_t, sin_t)


def _scan_chunk(idx, forward, n_ctx_chunks, n_chunks):
    if forward:
        return idx
    return jnp.where(idx < n_ctx_chunks, n_ctx_chunks - 1 - idx, n_chunks + n_ctx_chunks - 1 - idx)


def _ret_kernel(lg_ref, q_ref, k_ref, v_ref, gate_ref, nw_ref, o_ref,
                acc_ref, s_ref, mask_ref, qd_ref, kd_ref, cd_ref, *, n_ctx_chunks, n_chunks):
    hh = pl.program_id(1)
    c = RET_C
    masks = _head_masks(PAIR)
    first = masks[0]
    ii = lax.broadcasted_iota(jnp.int32, (c, 2 * c), 0)
    nn = lax.broadcasted_iota(jnp.int32, (c, 2 * c), 1)
    ri = lax.broadcasted_iota(jnp.int32, (c, PAIR), 0).astype(f32)
    for d in range(2):
        rel = ((ii - nn % c) if d == 0 else (nn % c - ii)).astype(f32)
        for j in range(NP):
            lg0 = lg_ref[d, hh * 2 * NP + 2 * j]
            lg1 = lg_ref[d, hh * 2 * NP + 2 * j + 1]
            lgm = jnp.where(nn < c, lg0, lg1)
            lgv = jnp.where(first, lg0, lg1)
            mask_ref[d, j] = jnp.where(rel >= 0.0, jnp.exp(lgm * jnp.maximum(rel, 0.0)), 0.0)
            qpow = (ri + 1.0) if d == 0 else (c - ri)
            kpow = (c - 1.0 - ri) if d == 0 else ri
            qd_ref[d, j] = jnp.exp(lgv * qpow)
            kd_ref[d, j] = jnp.exp(lgv * kpow)
            cd_ref[d, j] = jnp.exp(jnp.broadcast_to(lgv * c, (HEAD, PAIR)))
    s_ref[...] = jnp.zeros(s_ref.shape, f32)
    acc_ref[...] = jnp.zeros(acc_ref.shape, f32)
    chains = [(d, j) for d in range(2) for j in range(NP)]

    def step(s, carry):
        rows, qp, kp, vp = {}, {}, {}, {}
        for d in range(2):
            ch = _scan_chunk(s, d == 0, n_ctx_chunks, n_chunks)
            r0 = pl.multiple_of(ch * c, c)
            rows[d] = pl.ds(r0, c)
            q = q_ref[0, rows[d], :]
            k = k_ref[0, rows[d], :]
            v = v_ref[0, rows[d], :]
            for j in range(NP):
                col = slice(j * PAIR, (j + 1) * PAIR)
                qp[d, j], kp[d, j], vp[d, j] = q[:, col], k[:, col], v[:, col]
        sc, st, vb = {}, {}, {}
        for ck in chains:
            sc[ck] = _mm(qp[ck], _bd(kp[ck], masks), NT, P_RET) * mask_ref[ck[0], ck[1]]
            st[ck] = s_ref[ck[0], ck[1]]
            vb[ck] = _bd(vp[ck], masks)
        for (d, j) in chains:
            ck = (d, j)
            o = _mm(sc[ck], vb[ck], NN, P_RET) + _mm(qp[ck] * qd_ref[d, j], _bd(st[ck], masks), NN, P_RET)
            s_ref[d, j] = st[ck] * cd_ref[d, j] + _mm(_heads_to_rows(kp[ck] * kd_ref[d, j]), vb[ck], TN, P_RET)
            acc_ref[rows[d], j * PAIR:(j + 1) * PAIR] += o
        return carry

    lax.fori_loop(0, n_chunks, step, 0)

    gmat = _group_mat()
    nw = nw_ref[...]

    def norm_tile(i, carry):
        r0 = pl.multiple_of(i * ROW_TILE, ROW_TILE)
        o = acc_ref[pl.ds(r0, ROW_TILE), :]
        ms = _group_sum(o * o, gmat) * (1.0 / HEAD)
        g = gate_ref[0, pl.ds(r0, ROW_TILE), :]
        o_ref[0, pl.ds(r0, ROW_TILE), :] = ((o * lax.rsqrt(ms + NORM_EPS) * nw)
                                            * (g * jax.nn.sigmoid(g))).astype(bf16)
        return carry

    lax.fori_loop(0, (n_chunks * c) // ROW_TILE, norm_tile, 0)


def _retention(qkv, gates, log_gamma, norm_w, n_ctx):
    b, t, w3 = qkv.shape
    d_ret = w3 // 3
    nhalf = d_ret // RET_W
    kern = functools.partial(_ret_kernel, n_ctx_chunks=n_ctx // RET_C, n_chunks=t // RET_C)
    blk = lambda off: pl.BlockSpec((1, t, RET_W), lambda i, j, off=off: (i, 0, off * nhalf + j))
    return pl.pallas_call(
        kern,
        grid=(b, nhalf),
        in_specs=[pl.BlockSpec(memory_space=pltpu.SMEM),
                  blk(0), blk(1), blk(2),
                  pl.BlockSpec((1, t, RET_W), lambda i, j: (i, 0, j)),
                  pl.BlockSpec((1, RET_W), lambda i, j: (0, j))],
        out_specs=pl.BlockSpec((1, t, RET_W), lambda i, j: (i, 0, j)),
        out_shape=jax.ShapeDtypeStruct((b, t, d_ret), bf16),
        scratch_shapes=[pltpu.VMEM((t, RET_W), f32),
                        pltpu.VMEM((2, NP, HEAD, PAIR), f32),
                        pltpu.VMEM((2, NP, RET_C, 2 * RET_C), f32),
                        pltpu.VMEM((2, NP, RET_C, PAIR), f32),
                        pltpu.VMEM((2, NP, RET_C, PAIR), f32),
                        pltpu.VMEM((2, NP, HEAD, PAIR), f32)],
        compiler_params=pltpu.CompilerParams(vmem_limit_bytes=VMEM_LIMIT),
        name="retention",
    )(log_gamma, qkv, qkv, qkv, gates, norm_w)


class _Pair:
    def __init__(self, x, masks):
        self.x = x
        self.masks = masks
        self._parts = None
        self._lhs = None
        self._bd = None

    def parts(self):
        if self._parts is None:
            hi = self.x.astype(bf16).astype(f32)
            self._parts = (hi, self.x - hi)
        return self._parts

    def lhs(self):
        if self._lhs is None:
            if P_RW == 1:
                self._lhs = self.x.astype(bf16)
            else:
                hi, lo = self.parts()
                self._lhs = jnp.concatenate([hi, hi, lo], axis=1).astype(bf16)
        return self._lhs

    def _bd_parts(self):
        if self._bd is None:
            if P_RW == 1:
                self._bd = (_bd(self.x, self.masks).astype(bf16),)
            else:
                hi, lo = self.parts()
                bh = _bd(hi, self.masks).astype(bf16)
                self._bd = (bh, _bd(lo, self.masks).astype(bf16), bh)
        return self._bd

    def rhs(self):
        return jnp.concatenate(self._bd_parts(), axis=0)

    def rhs_nt(self):
        return jnp.concatenate(self._bd_parts(), axis=1)

    def lhs_rows(self):
        if P_RW == 1:
            return _heads_to_rows(self.x).astype(bf16)
        hi, lo = self.parts()
        th = _heads_to_rows(hi)
        return jnp.concatenate([th, th, _heads_to_rows(lo)], axis=0).astype(bf16)


def _lhs_rows(x):
    if P_RW == 1:
        return x.astype(bf16)
    hi = x.astype(bf16).astype(f32)
    return jnp.concatenate([hi, hi, x - hi], axis=0).astype(bf16)


def _mmp(a, b):
    return _dg(a.lhs(), b.rhs(), NN)


def _rwkv_kernel(r_ref, k_ref, v_ref, xwa_ref, swr_ref, swk_ref, swv_ref, swx_ref,
                 w0_ref, w2_ref, a0_ref, a2_ref, kk_ref, ka_ref, rk_ref, lnw_ref, lnb_ref, gate_ref,
                 o_ref, acc_ref, bon_ref,
                 g_ref, h_ref, q_ref, y_ref, pc_ref,
                 *, n_ctx_chunks, n_chunks):
    c = RW_C
    t_rows = n_chunks * c
    gmat = _group_mat()
    rows = lax.broadcasted_iota(jnp.int32, (c, 1), 0)

    def conv(ref, sw_ref, ch, r0):
        x = ref[0, pl.ds(r0, c), :]
        p0 = pl.multiple_of(jnp.maximum(r0 - 8, 0), 8)
        n0 = pl.multiple_of(jnp.minimum(r0 + c, t_rows - 8), 8)
        prev8 = ref[0, pl.ds(p0, 8), :]
        next8 = ref[0, pl.ds(n0, 8), :]
        has_prev = jnp.logical_and(ch != 0, ch != n_ctx_chunks).astype(f32)
        has_next = jnp.logical_and(ch != n_ctx_chunks - 1, ch != n_chunks - 1).astype(f32)
        xd = jnp.where(rows == 0, prev8[7:8, :] * has_prev, pltpu.roll(x, 1, 0))
        xu = jnp.where(rows == c - 1, next8[0:1, :] * has_next, pltpu.roll(x, c - 1, 0))
        sw = sw_ref[...]
        return sw[0:1] * xd + sw[1:2] * x + sw[2:3] * xu

    masks = _head_masks(RW_W)
    ii = lax.broadcasted_iota(jnp.int32, (c, RW_W), 0)
    jj = lax.broadcasted_iota(jnp.int32, (c, RW_W), 1) % HEAD
    eye = jnp.where(ii == jj, 1.0, 0.0)
    bd16 = (ii // 16) == (jj // 16)
    same32 = (ii // 32) == (jj // 32)
    off32 = jnp.logical_and(same32, jnp.logical_not(bd16))
    off64 = jnp.logical_not(same32)
    strict = (jj < ii, jj > ii)
    incl = (jj <= ii, jj >= ii)
    both = tuple(jnp.concatenate([strict[d], incl[d]], axis=0) for d in range(2))
    lora = xwa_ref.shape[2] // 2
    wscale = -math.exp(-0.5)
    mk = lambda x: _Pair(x, masks)
    col = lambda j: slice(j * RW_W, (j + 1) * RW_W)

    n_groups = n_chunks // RW_UNROLL

    def features(grp):
        out = []
        rs, ks, vs, xs = [], [], [], []
        for u in range(RW_UNROLL):
            ch = grp * RW_UNROLL + u
            r0 = pl.multiple_of(ch * c, c)
            rs.append(conv(r_ref, swr_ref, ch, r0))
            ks.append(conv(k_ref, swk_ref, ch, r0))
            vs.append(conv(v_ref, swv_ref, ch, r0))
            xs.append(conv(xwa_ref, swx_ref, ch, r0))
        kks = [k * kk_ref[...] for k in ks]
        sums = _group_sum(jnp.concatenate([kk * kk for kk in kks]
                                          + [r * k * rk_ref[...] for r, k in zip(rs, ks)], axis=0), gmat)
        xall = jnp.concatenate(xs, axis=0)
        txw = jnp.tanh(xall[:, :lora])
        kaps = []
        for u in range(RW_UNROLL):
            ch = grp * RW_UNROLL + u
            kaps.append(kks[u] / jnp.maximum(jnp.sqrt(sums[u * c:(u + 1) * c]), 1e-12))
            bon_ref[pl.ds(pl.multiple_of(ch * c, c), c), :] = sums[(RW_UNROLL + u) * c:(RW_UNROLL + u + 1) * c] * vs[u]
            out.append(vs[u])
        for d in range(2):
            lw_all = wscale * jax.nn.sigmoid(w0_ref[d:d + 1, :] + _mm(txw, w2_ref[d], NN, P_RW))
            a_all = jax.nn.sigmoid(a0_ref[d:d + 1, :] + _mm(xall[:, lora:], a2_ref[d], NN, P_RW))
            for u in range(RW_UNROLL):
                ch = grp * RW_UNROLL + u
                lw = lw_all[u * c:(u + 1) * c]
                a = a_all[u * c:(u + 1) * c]
                kmod = ks[u] * (1.0 + (a - 1.0) * ka_ref[...])
                b = kaps[u] * a
                lp = _cumsum_rows(lw, reverse=(d == 1))
                lpc = lp[c - 1:c, :] if d == 0 else lp[0:1, :]
                pinv = jnp.exp(-lp)
                ptail = jnp.exp(lpc - lp)
                pc_ref[d, pl.ds(pl.multiple_of(ch * 8, 8), 8), :] = jnp.broadcast_to(jnp.exp(lpc), (8, lp.shape[1]))
                out += [jnp.concatenate([kaps[u] * jnp.exp(lp - lw), rs[u] * jnp.exp(lp)], axis=0),
                        kmod * pinv, b * pinv, kmod * ptail, b * ptail]
        return out

    def prepare(it, carry):
        cur = features(it)
        feat = {}
        for u in range(RW_UNROLL):
            sl = pl.ds(pl.multiple_of((it * RW_UNROLL + u) * c, c), c)
            for d in range(2):
                kr, kd, bd, kh, bh = cur[RW_UNROLL + 5 * (d * RW_UNROLL + u):RW_UNROLL + 5 * (d * RW_UNROLL + u + 1)]
                feat[u, d] = dict(sl=sl, v=cur[u], kr=kr, rq=kr[c:], kd=kd, bd=bd, kh=kh, bh=bh)

        chains = [(u, d, j) for u in range(RW_UNROLL) for d in range(2) for j in range(RW_NP)]
        mkv, m_b, n_b, vp = {}, {}, {}, {}
        for ck in chains:
            u, d, j = ck
            f = feat[u, d]
            kr = mk(f["kr"][:, col(j)])
            mk_ = _dg(kr.lhs(), mk(f["kd"][:, col(j)]).rhs_nt(), NT)
            mb_ = _dg(kr.lhs(), mk(f["bd"][:, col(j)]).rhs_nt(), NT)
            vp[ck] = mk(f["v"][:, col(j)])
            mkv[ck] = _mmp(mk(jnp.where(both[d], mk_, 0.0)), vp[ck])
            m_b[ck] = jnp.where(strict[d], mb_[:c], 0.0)
            n_b[ck] = mk(jnp.where(incl[d], mb_[c:], 0.0))
        p, npow = {}, {}
        for ck in chains:
            nd = mk(jnp.where(bd16, -m_b[ck], 0.0))
            p[ck] = eye + nd.x
            npow[ck] = mk(_mmp(nd, nd))
        for level in range(2):
            for ck in chains:
                both_ = _mmp(mk(jnp.concatenate([p[ck], npow[ck].x], axis=0)), npow[ck])
                p[ck] = p[ck] + both_[:c]
                npow[ck] = mk(both_[c:])
        for ck in chains:
            p[ck] = p[ck] + _mmp(mk(p[ck]), npow[ck])
        for off in (off32, off64):
            tmp, pm = {}, {}
            for ck in chains:
                pm[ck] = mk(p[ck])
                tmp[ck] = mk(_mmp(pm[ck], mk(jnp.where(off, m_b[ck], 0.0))))
            for ck in chains:
                p[ck] = p[ck] - _mmp(tmp[ck], pm[ck])
        ta = {}
        for ck in chains:
            u, d, j = ck
            kq = feat[u, d]["kr"][:c, col(j)]
            ta[ck] = _dg(mk(p[ck]).lhs(), jnp.concatenate([mk(kq).rhs(), mk(mkv[ck][:c]).rhs()], axis=1), NN)
        for ck in chains:
            u, d, j = ck
            f = feat[u, d]
            a1, z0 = ta[ck][:, :RW_W], ta[ck][:, RW_W:]
            nb = _dg(n_b[ck].lhs(), jnp.concatenate([mk(a1).rhs(), mk(z0).rhs()], axis=1), NN)
            az = jnp.concatenate([_heads_to_rows(a1), _heads_to_rows(z0)], axis=1)
            bh = mk(f["bh"][:, col(j)])
            ab = _dg(_lhs_rows(az), bh.rhs(), TN)
            vk = _dg(vp[ck].lhs_rows(), mk(f["kh"][:, col(j)]).rhs(), TN)
            g_ref[d, f["sl"], col(j)] = -ab[:HEAD]
            h_ref[d, f["sl"], col(j)] = vk - ab[HEAD:]
            q_ref[d, f["sl"], col(j)] = f["rq"][:, col(j)] - nb[:, :RW_W]
            y_ref[d, f["sl"], col(j)] = mkv[ck][c:] - nb[:, RW_W:]
        return carry

    lax.fori_loop(0, n_groups, prepare, 0)

    acc_ref[...] = jnp.zeros(acc_ref.shape, f32)

    def scan(it, st):
        st = list(st)
        for u in range(SCAN_UNROLL):
            for d in range(2):
                ch = _scan_chunk(it * SCAN_UNROLL + u, d == 0, n_ctx_chunks, n_chunks)
                sl = pl.ds(pl.multiple_of(ch * c, c), c)
                pc = pc_ref[d, pl.ds(pl.multiple_of(ch * 8, 8), 8), :][0:1]
                g, h, q, y0 = g_ref[d, sl, :], h_ref[d, sl, :], q_ref[d, sl, :], y_ref[d, sl, :]
                for j in range(RW_NP):
                    s0 = st[d * RW_NP + j]
                    sp = mk(s0)
                    y = _dg(mk(q[:, col(j)]).lhs(), sp.rhs_nt(), NT) + y0[:, col(j)]
                    st[d * RW_NP + j] = (s0 * pc[:, col(j)] + _mmp(sp, mk(g[:, col(j)]))) + h[:, col(j)]
                    acc_ref[sl, col(j)] += y
        return tuple(st)

    lax.fori_loop(0, n_chunks // SCAN_UNROLL, scan,
                  tuple(jnp.zeros((HEAD, RW_W), f32) for _ in range(2 * RW_NP)))

    lnw = lnw_ref[...]
    lnb = lnb_ref[...]

    n_tiles = t_rows // ROW_TILE
    tiles_per_step = 3 if n_tiles % 3 == 0 else 1

    def readout(i, carry):
        sls = [pl.ds(pl.multiple_of((i * tiles_per_step + u) * ROW_TILE, ROW_TILE), ROW_TILE)
               for u in range(tiles_per_step)]
        ys = [acc_ref[sl, :] for sl in sls]
        ycs = [y - _group_sum(y, gmat) * (1.0 / HEAD) for y in ys]
        vrs = [_group_sum(yc * yc, gmat) * (1.0 / HEAD) for yc in ycs]
        for sl, yc, var in zip(sls, ycs, vrs):
            g = gate_ref[0, sl, :]
            o_ref[0, sl, :] = ((yc * lax.rsqrt(var + GN_EPS) * lnw + lnb + bon_ref[sl, :])
                               * (g * jax.nn.sigmoid(g))).astype(bf16)
        return carry

    lax.fori_loop(0, n_tiles // tiles_per_step, readout, 0)


def _rwkv(rw, gates, shift_w, w0, w2, a0, a2, k_k, k_a, r_k, ln_w, ln_b, n_ctx):
    b, t, d_shift = rw.shape
    d_rwkv = w0.shape[-1]
    lora2 = d_shift - 3 * d_rwkv
    nhalf = d_rwkv // HALF
    xcol = (3 * d_rwkv) // lora2
    assert (t // RW_C) % RW_UNROLL == 0 and (t // RW_C) % SCAN_UNROLL == 0
    kern = functools.partial(_rwkv_kernel, n_ctx_chunks=n_ctx // RW_C, n_chunks=t // RW_C)
    seq = lambda off: pl.BlockSpec((1, t, HALF), lambda i, j, off=off: (i, 0, off * nhalf + j))
    swb = lambda off: pl.BlockSpec((3, HALF), lambda i, j, off=off: (0, off * nhalf + j))
    vec = lambda rows: pl.BlockSpec((rows, HALF), lambda i, j: (0, j))
    lor = pl.BlockSpec((2, lora2 // 2, HALF), lambda i, j: (0, 0, j))
    return pl.pallas_call(
        kern,
        grid=(b, nhalf),
        in_specs=[seq(0), seq(1), seq(2),
                  pl.BlockSpec((1, t, lora2), lambda i, j: (i, 0, xcol)),
                  swb(0), swb(1), swb(2),
                  pl.BlockSpec((3, lora2), lambda i, j: (0, xcol)),
                  vec(2), lor, vec(2), lor, vec(1), vec(1), vec(1), vec(1), vec(1),
                  pl.BlockSpec((1, t, HALF), lambda i, j: (i, 0, gates.shape[-1] // HALF - nhalf + j))],
        out_specs=pl.BlockSpec((1, t, HALF), lambda i, j: (i, 0, j)),
        out_shape=jax.ShapeDtypeStruct((b, t, d_rwkv), bf16),
        scratch_shapes=[pltpu.VMEM((t, HALF), f32)] * 2 + [pltpu.VMEM((2, t, HALF), f32)] * 4
        + [pltpu.VMEM((2, (t // RW_C) * 8, HALF), f32)],
        compiler_params=pltpu.CompilerParams(vmem_limit_bytes=VMEM_LIMIT),
        name="rwkv7",
    )(rw, rw, rw, rw, shift_w, shift_w, shift_w, shift_w,
      w0, w2, a0, a2, k_k, k_a, r_k, ln_w, ln_b, gates)


def _outproj_kernel(*refs, n_x, final):
    ret_ref, rwo_ref = refs[:2]
    xs_refs = refs[2:2 + n_x]
    mod_ref, w_ref, fw_ref, o_ref = refs[2 + n_x:]
    mix = _dg(jnp.concatenate([ret_ref[0], rwo_ref[0]], axis=1), w_ref[...], NN)
    xn = _rows_of(xs_refs) + mod_ref[0, 0][2:3] * mix
    if final:
        ms = jnp.mean(xn * xn, axis=-1, keepdims=True)
        xn = xn * lax.rsqrt(ms + NORM_EPS) * fw_ref[...]
    o_ref[0] = xn


def _outproj(ret, rwo, xs, modv, w, fw, n_ctx, final):
    b, t, d_ret = ret.shape
    d = xs[0].shape[-1]
    d_rw = rwo.shape[-1]
    skip = (n_ctx // ROW_TILE) if final else 0
    n_tiles = t // ROW_TILE - skip
    row = lambda w_: pl.BlockSpec((1, ROW_TILE, w_), lambda i, j: (i, j + skip, 0))
    kern = functools.partial(_outproj_kernel, n_x=len(xs), final=final)
    return pl.pallas_call(
        kern,
        grid=(b, n_tiles),
        in_specs=[row(d_ret), row(d_rw)] + _row_specs(xs, skip) + [
            pl.BlockSpec((1, 1, 8, d), lambda i, j: (i, jnp.minimum(j + skip, 1), 0, 0)),
            pl.BlockSpec(w.shape, lambda i, j: (0, 0), pipeline_mode=pl.Buffered(1)),
            pl.BlockSpec((1, d), lambda i, j: (0, 0))],
        out_specs=pl.BlockSpec((1, ROW_TILE, d), lambda i, j: (i, j, 0)),
        out_shape=jax.ShapeDtypeStruct((b, n_tiles * ROW_TILE, d), f32),
        compiler_params=pltpu.CompilerParams(vmem_limit_bytes=VMEM_LIMIT),
        name="out_proj",
    )(ret, rwo, *xs, modv, w, fw)


def _rope_tables(n_ctx, seq, d_ret):
    nf = HEAD // 4
    inv = ROPE_BASE ** (-jnp.arange(nf, dtype=f32) / nf)
    pos = jnp.arange(seq)
    row_pos = (pos // GRID_W).astype(f32)
    col_pos = (pos % GRID_W).astype(f32)
    ang_r = row_pos[:, None] * inv[None, :]
    ang_c = col_pos[:, None] * inv[None, :]
    cos_h = jnp.concatenate([jnp.cos(ang_r)] * 2 + [jnp.cos(ang_c)] * 2, axis=-1)
    sin_h = jnp.concatenate([-jnp.sin(ang_r), jnp.sin(ang_r), -jnp.sin(ang_c), jnp.sin(ang_c)], axis=-1)
    reps = d_ret // HEAD
    cos_l = jnp.tile(cos_h, (1, reps))
    sin_l = jnp.tile(sin_h, (1, reps))
    cos_t = jnp.concatenate([jnp.ones((n_ctx, d_ret), f32), cos_l], axis=0)
    sin_t = jnp.concatenate([jnp.zeros((n_ctx, d_ret), f32), sin_l], axis=0)
    return cos_t, sin_t


def kernel(x, c, ctx, c_ctx, norm_w, w_mod, b_mod, w_in, ret_log_gamma, ret_norm_w, rwkv_shift_w, rwkv_w0, rwkv_w2, rwkv_a0, rwkv_a2, rwkv_k_k, rwkv_k_a, rwkv_r_k, rwkv_ln_w, rwkv_ln_b, w_out, final_norm_w):
    b, seq, d = x.shape
    n_ctx = ctx.shape[1]
    depth = w_in.shape[0]
    d_ret = ret_norm_w.shape[-1]
    d_rwkv = rwkv_w0.shape[-1]
    d_shift = rwkv_shift_w.shape[-1]

    cond = jnp.zeros((16, d), f32).at[:b].set(c.astype(f32)).at[b].set(c_ctx.astype(f32))
    mods = _modulation(cond, w_mod, b_mod)
    cos_t, sin_t = _rope_tables(n_ctx, seq, d_ret)
    if depth > 1 and n_ctx == ROW_TILE:
        xs = (ctx.astype(f32), x.astype(f32))
    else:
        xs = (jnp.concatenate([ctx.astype(f32), x.astype(f32)], axis=1),)

    out = None
    for layer in range(depth):
        m = mods[layer].reshape(16, 3, d)
        m = jnp.stack([m[:, 1], m[:, 0], m[:, 2]], axis=1)
        lat = m[:b]
        cx = jnp.broadcast_to(m[b][None], (b, 3, d))
        modv = jnp.pad(jnp.stack([cx, lat], axis=1), ((0, 0), (0, 0), (0, 5), (0, 0)))

        qkv, rw, gates = _inproj(xs, modv, norm_w[layer][None], w_in[layer].astype(bf16), cos_t, sin_t,
                                 d_ret, d_shift, d_rwkv)
        ret = _retention(qkv, gates, ret_log_gamma[layer], ret_norm_w[layer][None], n_ctx)
        rwo = _rwkv(rw, gates, rwkv_shift_w[layer], rwkv_w0[layer], rwkv_w2[layer], rwkv_a0[layer],
                    rwkv_a2[layer], rwkv_k_k[layer][None], rwkv_k_a[layer][None],
                    rwkv_r_k[layer].reshape(1, d_rwkv), rwkv_ln_w[layer][None],
                    rwkv_ln_b[layer][None], n_ctx)
        final = layer == depth - 1
        res = _outproj(ret, rwo, xs, modv, w_out[layer].astype(bf16), final_norm_w[None], n_ctx, final)
        if final:
            out = res
        else:
            xs = (res,)
    return out
```

```python
import functools
import math

import jax
import jax.numpy as jnp
from jax import lax
from jax.experimental import pallas as pl
from jax.experimental.pallas import tpu as pltpu

f32 = jnp.float32
bf16 = jnp.bfloat16

HEAD = 64
PAIR = 2 * HEAD
RET_C = 128
RW_C = 64
RW_UNROLL = 4
SCAN_UNROLL = 12
GRID_W = 64
ROPE_BASE = 10000.0
NORM_EPS = 1e-6
GN_EPS = 64e-5
ROW_TILE = 256
HALF = 256
RET_W = 512
NP = RET_W // PAIR
RW_W = 128
RW_NP = HALF // RW_W
VMEM_LIMIT = 56 * 1024 * 1024
SUBLANES = 8
COND_ROWS = 16
MOD_TILE = 1024
ROPE_NF = HEAD // 4

NN = ((1,), (0,))
NT = ((1,), (1,))
TN = ((0,), (0,))

P_MOD = 3
P_RET = 1
P_RW = 1
P_SUM = 1


def _dg(a, b, dims):
    return lax.dot_general(a, b, (dims, ((), ())), preferred_element_type=f32)


def _split2(a):
    hi = a.astype(bf16)
    lo = (a - hi.astype(f32)).astype(bf16)
    return hi, lo


def _mm(a, b, dims=NN, passes=3):
    if passes == 1:
        return _dg(a.astype(bf16), b.astype(bf16), dims)
    ah, al = _split2(a)
    bh, bl = _split2(b)
    return _dg(ah, bh, dims) + (_dg(ah, bl, dims) + _dg(al, bh, dims))


def _cumsum_rows(x, reverse):
    n = x.shape[0]
    row = lax.broadcasted_iota(jnp.int32, (n, 1), 0)
    s = 1
    while s < n:
        if reverse:
            x = x + jnp.where(row < n - s, pltpu.roll(x, n - s, 0), 0.0)
        else:
            x = x + jnp.where(row >= s, pltpu.roll(x, s, 0), 0.0)
        s *= 2
    return x


def _group_sum(x, gmat, pieces=P_SUM):
    rows, width = x.shape
    gw = gmat.shape[0]
    cols = width // gw
    xs = jnp.concatenate([x[:, c * gw:(c + 1) * gw] for c in range(cols)], axis=0)
    if pieces == 1:
        s = _dg(xs.astype(bf16), gmat, NN)
    else:
        h, l = _split2(xs)
        s = _dg(h, gmat, NN) + _dg(l, gmat, NN)
    return jnp.concatenate([s[c * rows:(c + 1) * rows] for c in range(cols)], axis=1)


def _group_mat():
    ii = lax.broadcasted_iota(jnp.int32, (PAIR, PAIR), 0) // HEAD
    jj = lax.broadcasted_iota(jnp.int32, (PAIR, PAIR), 1) // HEAD
    return jnp.where(ii == jj, 1.0, 0.0).astype(bf16)


def _head_masks(width):
    blk = lax.broadcasted_iota(jnp.int32, (1, width), 1) // HEAD
    return [blk == h for h in range(width // HEAD)]


def _bd(x, masks):
    return jnp.concatenate([jnp.where(m, x, 0.0) for m in masks], axis=0)


def _heads_to_rows(x):
    w = x.shape[1]
    return jnp.concatenate([x[:, :HEAD]] + [pltpu.roll(x, w - h * HEAD, 1)[:, :HEAD]
                                            for h in range(1, w // HEAD)], axis=0)


def _mod_kernel(c_ref, w_ref, b_ref, o_ref):
    cnd = c_ref[...]
    s = cnd * jax.nn.sigmoid(cnd)
    o_ref[0] = _mm(s, w_ref[0], NN, P_MOD) + b_ref[0]


def _modulation(cond, w_mod, b_mod):
    depth, d, d3 = w_mod.shape
    rows = cond.shape[0]
    tn = MOD_TILE
    return pl.pallas_call(
        _mod_kernel,
        grid=(depth, d3 // tn),
        in_specs=[pl.BlockSpec((rows, d), lambda l, j: (0, 0)),
                  pl.BlockSpec((1, d, tn), lambda l, j: (l, 0, j)),
                  pl.BlockSpec((1, 1, tn), lambda l, j: (l, 0, j))],
        out_specs=pl.BlockSpec((1, rows, tn), lambda l, j: (l, 0, j)),
        out_shape=jax.ShapeDtypeStruct((depth, rows, d3), f32),
        compiler_params=pltpu.CompilerParams(vmem_limit_bytes=VMEM_LIMIT),
        name="adaln_mod",
    )(cond, w_mod, b_mod.reshape(depth, 1, d3))


def _rows_of(xs_refs):
    if len(xs_refs) == 1:
        return xs_refs[0][0]
    return jnp.where(pl.program_id(1) == 0, xs_refs[0][0], xs_refs[1][0])


def _row_specs(xs, skip=0):
    d = xs[0].shape[-1]
    if len(xs) == 1:
        return [pl.BlockSpec((1, ROW_TILE, d), lambda i, j: (i, j + skip, 0))]
    assert skip == 0 and xs[0].shape[1] == ROW_TILE
    return [pl.BlockSpec((1, ROW_TILE, d), lambda i, j: (i, 0, 0)),
            pl.BlockSpec((1, ROW_TILE, d), lambda i, j: (i, jnp.maximum(j - 1, 0), 0))]


def _inproj_kernel(*refs, n_x, d_ret, d_shift):
    xs_refs = refs[:n_x]
    mod_ref, nw_ref, w_ref, cos_ref, sin_ref, qkv_ref, rw_ref, g_ref = refs[n_x:]
    x = _rows_of(xs_refs)
    ms = jnp.mean(x * x, axis=-1, keepdims=True)
    m = mod_ref[0, 0]
    h = x * lax.rsqrt(ms + NORM_EPS) * nw_ref[...]
    h = (h * (1.0 + m[0:1]) + m[1:2]).astype(bf16)

    def proj(lo, hi):
        return _dg(h, w_ref[:, lo:hi], NN)

    cos = cos_ref[...]
    sin = sin_ref[...]
    lane = lax.broadcasted_iota(jnp.int32, cos.shape, 1)
    first = (lane % (2 * ROPE_NF)) < ROPE_NF

    def rope(t):
        nl = t.shape[-1]
        swapped = jnp.where(first, pltpu.roll(t, nl - ROPE_NF, 1), pltpu.roll(t, ROPE_NF, 1))
        return t * cos + swapped * sin

    qkv_ref[0, :, 0:d_ret] = rope(proj(0, d_ret))
    qkv_ref[0, :, d_ret:2 * d_ret] = rope(proj(d_ret, 2 * d_ret)) * (HEAD ** -0.5)
    qkv_ref[0, :, 2 * d_ret:3 * d_ret] = proj(2 * d_ret, 3 * d_ret)
    g_ref[0, :, 0:d_ret] = proj(3 * d_ret, 4 * d_ret)
    o = 4 * d_ret
    rw_ref[0] = proj(o, o + d_shift)
    g_ref[0, :, d_ret:] = proj(o + d_shift, w_ref.shape[1])


def _inproj(xs, modv, nw, w, cos_t, sin_t, d_ret, d_shift, d_rwkv):
    b, d = xs[0].shape[0], xs[0].shape[-1]
    t = sum(a.shape[1] for a in xs)
    n_tiles = t // ROW_TILE
    kern = functools.partial(_inproj_kernel, n_x=len(xs), d_ret=d_ret, d_shift=d_shift)
    return pl.pallas_call(
        kern,
        grid=(b, n_tiles),
        in_specs=_row_specs(xs) + [
            pl.BlockSpec((1, 1, SUBLANES, d), lambda i, j: (i, jnp.minimum(j, 1), 0, 0)),
            pl.BlockSpec((1, d), lambda i, j: (0, 0)),
            pl.BlockSpec(w.shape, lambda i, j: (0, 0), pipeline_mode=pl.Buffered(1)),
            pl.BlockSpec((ROW_TILE, d_ret), lambda i, j: (j, 0)),
            pl.BlockSpec((ROW_TILE, d_ret), lambda i, j: (j, 0))],
        out_specs=[pl.BlockSpec((1, ROW_TILE, 3 * d_ret), lambda i, j: (i, j, 0)),
                   pl.BlockSpec((1, ROW_TILE, d_shift), lambda i, j: (i, j, 0)),
                   pl.BlockSpec((1, ROW_TILE, d_ret + d_rwkv), lambda i, j: (i, j, 0))],
        out_shape=[jax.ShapeDtypeStruct((b, t, 3 * d_ret), f32),
                   jax.ShapeDtypeStruct((b, t, d_shift), f32),
                   jax.ShapeDtypeStruct((b, t, d_ret + d_rwkv), f32)],
        compiler_params=pltpu.CompilerParams(vmem_limit_bytes=VMEM_LIMIT),
        name="in_proj",
    )(*xs, modv, nw, w, cos_t, sin_t)


def _scan_chunk(idx, forward, n_ctx_chunks, n_chunks):
    if forward:
        return idx
    return jnp.where(idx < n_ctx_chunks, n_ctx_chunks - 1 - idx, n_chunks + n_ctx_chunks - 1 - idx)


def _ret_kernel(lg_ref, q_ref, k_ref, v_ref, gate_ref, nw_ref, o_ref,
                acc_ref, s_ref, mask_ref, qd_ref, kd_ref, cd_ref, *, n_ctx_chunks, n_chunks):
    hh = pl.program_id(1)
    c = RET_C
    masks = _head_masks(PAIR)
    first = masks[0]
    ii = lax.broadcasted_iota(jnp.int32, (c, 2 * c), 0)
    nn = lax.broadcasted_iota(jnp.int32, (c, 2 * c), 1)
    ri = lax.broadcasted_iota(jnp.int32, (c, PAIR), 0).astype(f32)
    for d in range(2):
        rel = ((ii - nn % c) if d == 0 else (nn % c - ii)).astype(f32)
        for j in range(NP):
            lg0 = lg_ref[d, hh * 2 * NP + 2 * j]
            lg1 = lg_ref[d, hh * 2 * NP + 2 * j + 1]
            lgm = jnp.where(nn < c, lg0, lg1)
            lgv = jnp.where(first, lg0, lg1)
            mask_ref[d, j] = jnp.where(rel >= 0.0, jnp.exp(lgm * jnp.maximum(rel, 0.0)), 0.0)
            qpow = (ri + 1.0) if d == 0 else (c - ri)
            kpow = (c - 1.0 - ri) if d == 0 else ri
            qd_ref[d, j] = jnp.exp(lgv * qpow)
            kd_ref[d, j] = jnp.exp(lgv * kpow)
            cd_ref[d, j] = jnp.exp(jnp.broadcast_to(lgv * c, (HEAD, PAIR)))
    s_ref[...] = jnp.zeros(s_ref.shape, f32)
    acc_ref[...] = jnp.zeros(acc_ref.shape, f32)
    chains = [(d, j) for d in range(2) for j in range(NP)]

    def step(s, carry):
        rows, qp, kp, vp = {}, {}, {}, {}
        for d in range(2):
            ch = _scan_chunk(s, d == 0, n_ctx_chunks, n_chunks)
            r0 = pl.multiple_of(ch * c, c)
            rows[d] = pl.ds(r0, c)
            q = q_ref[0, rows[d], :]
            k = k_ref[0, rows[d], :]
            v = v_ref[0, rows[d], :]
            for j in range(NP):
                col = slice(j * PAIR, (j + 1) * PAIR)
                qp[d, j], kp[d, j], vp[d, j] = q[:, col], k[:, col], v[:, col]
        sc, st, vb = {}, {}, {}
        for ck in chains:
            sc[ck] = _mm(qp[ck], _bd(kp[ck], masks), NT, P_RET) * mask_ref[ck[0], ck[1]]
            st[ck] = s_ref[ck[0], ck[1]]
            vb[ck] = _bd(vp[ck], masks)
        for (d, j) in chains:
            ck = (d, j)
            o = _mm(sc[ck], vb[ck], NN, P_RET) + _mm(qp[ck] * qd_ref[d, j], _bd(st[ck], masks), NN, P_RET)
            s_ref[d, j] = st[ck] * cd_ref[d, j] + _mm(_heads_to_rows(kp[ck] * kd_ref[d, j]), vb[ck], TN, P_RET)
            acc_ref[rows[d], j * PAIR:(j + 1) * PAIR] += o
        return carry

    lax.fori_loop(0, n_chunks, step, 0)

    gmat = _group_mat()
    nw = nw_ref[...]

    n_tiles = (n_chunks * c) // ROW_TILE
    tiles_per_step = 3 if n_tiles % 3 == 0 else 1

    def norm_tiles(i, carry):
        sls = [pl.ds(pl.multiple_of((i * tiles_per_step + u) * ROW_TILE, ROW_TILE), ROW_TILE)
               for u in range(tiles_per_step)]
        os_ = [acc_ref[sl, :] for sl in sls]
        mss = [_group_sum(o * o, gmat) * (1.0 / HEAD) for o in os_]
        for sl, o, ms in zip(sls, os_, mss):
            g = gate_ref[0, sl, :]
            o_ref[0, sl, :] = ((o * lax.rsqrt(ms + NORM_EPS) * nw) * (g * jax.nn.sigmoid(g))).astype(bf16)
        return carry

    lax.fori_loop(0, n_tiles // tiles_per_step, norm_tiles, 0)


def _retention(qkv, gates, log_gamma, norm_w, n_ctx):
    b, t, w3 = qkv.shape
    d_ret = w3 // 3
    nhalf = d_ret // RET_W
    kern = functools.partial(_ret_kernel, n_ctx_chunks=n_ctx // RET_C, n_chunks=t // RET_C)
    blk = lambda off: pl.BlockSpec((1, t, RET_W), lambda i, j, off=off: (i, 0, off * nhalf + j))
    return pl.pallas_call(
        kern,
        grid=(b, nhalf),
        in_specs=[pl.BlockSpec(memory_space=pltpu.SMEM),
                  blk(0), blk(1), blk(2),
                  pl.BlockSpec((1, t, RET_W), lambda i, j: (i, 0, j)),
                  pl.BlockSpec((1, RET_W), lambda i, j: (0, j))],
        out_specs=pl.BlockSpec((1, t, RET_W), lambda i, j: (i, 0, j)),
        out_shape=jax.ShapeDtypeStruct((b, t, d_ret), bf16),
        scratch_shapes=[pltpu.VMEM((t, RET_W), f32),
                        pltpu.VMEM((2, NP, HEAD, PAIR), f32),
                        pltpu.VMEM((2, NP, RET_C, 2 * RET_C), f32),
                        pltpu.VMEM((2, NP, RET_C, PAIR), f32),
                        pltpu.VMEM((2, NP, RET_C, PAIR), f32),
                        pltpu.VMEM((2, NP, HEAD, PAIR), f32)],
        compiler_params=pltpu.CompilerParams(vmem_limit_bytes=VMEM_LIMIT),
        name="retention",
    )(log_gamma, qkv, qkv, qkv, gates, norm_w)


class _Pair:
    def __init__(self, x, masks):
        self.x = x
        self.masks = masks
        self._parts = None
        self._lhs = None
        self._bd = None

    def parts(self):
        if self._parts is None:
            hi = self.x.astype(bf16).astype(f32)
            self._parts = (hi, self.x - hi)
        return self._parts

    def lhs(self):
        if self._lhs is None:
            if P_RW == 1:
                self._lhs = self.x.astype(bf16)
            else:
                hi, lo = self.parts()
                self._lhs = jnp.concatenate([hi, hi, lo], axis=1).astype(bf16)
        return self._lhs

    def _bd_parts(self):
        if self._bd is None:
            if P_RW == 1:
                self._bd = (_bd(self.x, self.masks).astype(bf16),)
            else:
                hi, lo = self.parts()
                bh = _bd(hi, self.masks).astype(bf16)
                self._bd = (bh, _bd(lo, self.masks).astype(bf16), bh)
        return self._bd

    def rhs(self):
        return jnp.concatenate(self._bd_parts(), axis=0)

    def rhs_nt(self):
        return jnp.concatenate(self._bd_parts(), axis=1)

    def lhs_rows(self):
        if P_RW == 1:
            return _heads_to_rows(self.x).astype(bf16)
        hi, lo = self.parts()
        th = _heads_to_rows(hi)
        return jnp.concatenate([th, th, _heads_to_rows(lo)], axis=0).astype(bf16)


def _lhs_rows(x):
    if P_RW == 1:
        return x.astype(bf16)
    hi = x.astype(bf16).astype(f32)
    return jnp.concatenate([hi, hi, x - hi], axis=0).astype(bf16)


def _mmp(a, b):
    return _dg(a.lhs(), b.rhs(), NN)


def _rwkv_kernel(r_ref, k_ref, v_ref, xwa_ref, swr_ref, swk_ref, swv_ref, swx_ref,
                 w0_ref, w2_ref, a0_ref, a2_ref, kk_ref, ka_ref, rk_ref, lnw_ref, lnb_ref, gate_ref,
                 o_ref, acc_ref, bon_ref,
                 g_ref, h_ref, q_ref, y_ref, pc_ref,
                 *, n_ctx_chunks, n_chunks):
    c = RW_C
    t_rows = n_chunks * c
    gmat = _group_mat()
    rows = lax.broadcasted_iota(jnp.int32, (c, 1), 0)

    def conv(ref, sw_ref, ch, r0):
        x = ref[0, pl.ds(r0, c), :]
        p0 = pl.multiple_of(jnp.maximum(r0 - SUBLANES, 0), SUBLANES)
        n0 = pl.multiple_of(jnp.minimum(r0 + c, t_rows - SUBLANES), SUBLANES)
        prev8 = ref[0, pl.ds(p0, SUBLANES), :]
        next8 = ref[0, pl.ds(n0, SUBLANES), :]
        has_prev = jnp.logical_and(ch != 0, ch != n_ctx_chunks).astype(f32)
        has_next = jnp.logical_and(ch != n_ctx_chunks - 1, ch != n_chunks - 1).astype(f32)
        xd = jnp.where(rows == 0, prev8[SUBLANES - 1:SUBLANES, :] * has_prev, pltpu.roll(x, 1, 0))
        xu = jnp.where(rows == c - 1, next8[0:1, :] * has_next, pltpu.roll(x, c - 1, 0))
        sw = sw_ref[...]
        return sw[0:1] * xd + sw[1:2] * x + sw[2:3] * xu

    masks = _head_masks(RW_W)
    ii = lax.broadcasted_iota(jnp.int32, (c, RW_W), 0)
    jj = lax.broadcasted_iota(jnp.int32, (c, RW_W), 1) % HEAD
    eye = jnp.where(ii == jj, 1.0, 0.0)
    bd16 = (ii // 16) == (jj // 16)
    same32 = (ii // 32) == (jj // 32)
    off32 = jnp.logical_and(same32, jnp.logical_not(bd16))
    off64 = jnp.logical_not(same32)
    strict = (jj < ii, jj > ii)
    incl = (jj <= ii, jj >= ii)
    both = tuple(jnp.concatenate([strict[d], incl[d]], axis=0) for d in range(2))
    lora = xwa_ref.shape[2] // 2
    wscale = -math.exp(-0.5)
    mk = lambda x: _Pair(x, masks)
    col = lambda j: slice(j * RW_W, (j + 1) * RW_W)

    n_groups = n_chunks // RW_UNROLL

    def features(grp):
        out = []
        rs, ks, vs, xs = [], [], [], []
        for u in range(RW_UNROLL):
            ch = grp * RW_UNROLL + u
            r0 = pl.multiple_of(ch * c, c)
            rs.append(conv(r_ref, swr_ref, ch, r0))
            ks.append(conv(k_ref, swk_ref, ch, r0))
            vs.append(conv(v_ref, swv_ref, ch, r0))
            xs.append(conv(xwa_ref, swx_ref, ch, r0))
        kks = [k * kk_ref[...] for k in ks]
        sums = _group_sum(jnp.concatenate([kk * kk for kk in kks]
                                          + [r * k * rk_ref[...] for r, k in zip(rs, ks)], axis=0), gmat)
        xall = jnp.concatenate(xs, axis=0)
        txw = jnp.tanh(xall[:, :lora])
        kaps = []
        for u in range(RW_UNROLL):
            ch = grp * RW_UNROLL + u
            kaps.append(kks[u] / jnp.maximum(jnp.sqrt(sums[u * c:(u + 1) * c]), 1e-12))
            bon_ref[pl.ds(pl.multiple_of(ch * c, c), c), :] = sums[(RW_UNROLL + u) * c:(RW_UNROLL + u + 1) * c] * vs[u]
            out.append(vs[u])
        for d in range(2):
            lw_all = wscale * jax.nn.sigmoid(w0_ref[d:d + 1, :] + _mm(txw, w2_ref[d], NN, P_RW))
            a_all = jax.nn.sigmoid(a0_ref[d:d + 1, :] + _mm(xall[:, lora:], a2_ref[d], NN, P_RW))
            for u in range(RW_UNROLL):
                ch = grp * RW_UNROLL + u
                lw = lw_all[u * c:(u + 1) * c]
                a = a_all[u * c:(u + 1) * c]
                kmod = ks[u] * (1.0 + (a - 1.0) * ka_ref[...])
                b = kaps[u] * a
                lp = _cumsum_rows(lw, reverse=(d == 1))
                lpc = lp[c - 1:c, :] if d == 0 else lp[0:1, :]
                pinv = jnp.exp(-lp)
                ptail = jnp.exp(lpc - lp)
                pc_ref[d, pl.ds(pl.multiple_of(ch * SUBLANES, SUBLANES), SUBLANES), :] = jnp.broadcast_to(
                    jnp.exp(lpc), (SUBLANES, lp.shape[1]))
                out += [jnp.concatenate([kaps[u] * jnp.exp(lp - lw), rs[u] * jnp.exp(lp)], axis=0),
                        kmod * pinv, b * pinv, kmod * ptail, b * ptail]
        return out

    def prepare(it, carry):
        cur = features(it)
        feat = {}
        for u in range(RW_UNROLL):
            sl = pl.ds(pl.multiple_of((it * RW_UNROLL + u) * c, c), c)
            for d in range(2):
                kr, kd, bd, kh, bh = cur[RW_UNROLL + 5 * (d * RW_UNROLL + u):RW_UNROLL + 5 * (d * RW_UNROLL + u + 1)]
                feat[u, d] = dict(sl=sl, v=cur[u], kr=kr, rq=kr[c:], kd=kd, bd=bd, kh=kh, bh=bh)

        chains = [(u, d, j) for u in range(RW_UNROLL) for d in range(2) for j in range(RW_NP)]
        mkv, m_b, n_b, vp = {}, {}, {}, {}
        for ck in chains:
            u, d, j = ck
            f = feat[u, d]
            kr = mk(f["kr"][:, col(j)])
            mk_ = _dg(kr.lhs(), mk(f["kd"][:, col(j)]).rhs_nt(), NT)
            mb_ = _dg(kr.lhs(), mk(f["bd"][:, col(j)]).rhs_nt(), NT)
            vp[ck] = mk(f["v"][:, col(j)])
            mkv[ck] = _mmp(mk(jnp.where(both[d], mk_, 0.0)), vp[ck])
            m_b[ck] = jnp.where(strict[d], mb_[:c], 0.0)
            n_b[ck] = mk(jnp.where(incl[d], mb_[c:], 0.0))
        p, npow = {}, {}
        for ck in chains:
            nd = mk(jnp.where(bd16, -m_b[ck], 0.0))
            p[ck] = eye + nd.x
            npow[ck] = mk(_mmp(nd, nd))
        for level in range(2):
            for ck in chains:
                both_ = _mmp(mk(jnp.concatenate([p[ck], npow[ck].x], axis=0)), npow[ck])
                p[ck] = p[ck] + both_[:c]
                npow[ck] = mk(both_[c:])
        for ck in chains:
            p[ck] = p[ck] + _mmp(mk(p[ck]), npow[ck])
        for off in (off32, off64):
            tmp, pm = {}, {}
            for ck in chains:
                pm[ck] = mk(p[ck])
                tmp[ck] = mk(_mmp(pm[ck], mk(jnp.where(off, m_b[ck], 0.0))))
            for ck in chains:
                p[ck] = p[ck] - _mmp(tmp[ck], pm[ck])
        ta = {}
        for ck in chains:
            u, d, j = ck
            kq = feat[u, d]["kr"][:c, col(j)]
            ta[ck] = _dg(mk(p[ck]).lhs(), jnp.concatenate([mk(kq).rhs(), mk(mkv[ck][:c]).rhs()], axis=1), NN)
        for ck in chains:
            u, d, j = ck
            f = feat[u, d]
            a1, z0 = ta[ck][:, :RW_W], ta[ck][:, RW_W:]
            nb = _dg(n_b[ck].lhs(), jnp.concatenate([mk(a1).rhs(), mk(z0).rhs()], axis=1), NN)
            az = jnp.concatenate([_heads_to_rows(a1), _heads_to_rows(z0)], axis=1)
            bh = mk(f["bh"][:, col(j)])
            ab = _dg(_lhs_rows(az), bh.rhs(), TN)
            vk = _dg(vp[ck].lhs_rows(), mk(f["kh"][:, col(j)]).rhs(), TN)
            g_ref[d, f["sl"], col(j)] = -ab[:HEAD]
            h_ref[d, f["sl"], col(j)] = vk - ab[HEAD:]
            q_ref[d, f["sl"], col(j)] = f["rq"][:, col(j)] - nb[:, :RW_W]
            y_ref[d, f["sl"], col(j)] = mkv[ck][c:] - nb[:, RW_W:]
        return carry

    lax.fori_loop(0, n_groups, prepare, 0)

    acc_ref[...] = jnp.zeros(acc_ref.shape, f32)

    def scan(it, st):
        st = list(st)
        for u in range(SCAN_UNROLL):
            for d in range(2):
                ch = _scan_chunk(it * SCAN_UNROLL + u, d == 0, n_ctx_chunks, n_chunks)
                sl = pl.ds(pl.multiple_of(ch * c, c), c)
                pc = pc_ref[d, pl.ds(pl.multiple_of(ch * SUBLANES, SUBLANES), SUBLANES), :][0:1]
                g, h, q, y0 = g_ref[d, sl, :], h_ref[d, sl, :], q_ref[d, sl, :], y_ref[d, sl, :]
                for j in range(RW_NP):
                    s0 = st[d * RW_NP + j]
                    sp = mk(s0)
                    y = _dg(mk(q[:, col(j)]).lhs(), sp.rhs_nt(), NT) + y0[:, col(j)]
                    st[d * RW_NP + j] = (s0 * pc[:, col(j)] + _mmp(sp, mk(g[:, col(j)]))) + h[:, col(j)]
                    acc_ref[sl, col(j)] += y
        return tuple(st)

    lax.fori_loop(0, n_chunks // SCAN_UNROLL, scan,
                  tuple(jnp.zeros((HEAD, RW_W), f32) for _ in range(2 * RW_NP)))

    lnw = lnw_ref[...]
    lnb = lnb_ref[...]

    n_tiles = t_rows // ROW_TILE
    tiles_per_step = 3 if n_tiles % 3 == 0 else 1

    def readout(i, carry):
        sls = [pl.ds(pl.multiple_of((i * tiles_per_step + u) * ROW_TILE, ROW_TILE), ROW_TILE)
               for u in range(tiles_per_step)]
        ys = [acc_ref[sl, :] for sl in sls]
        ycs = [y - _group_sum(y, gmat) * (1.0 / HEAD) for y in ys]
        vrs = [_group_sum(yc * yc, gmat) * (1.0 / HEAD) for yc in ycs]
        for sl, yc, var in zip(sls, ycs, vrs):
            g = gate_ref[0, sl, :]
            o_ref[0, sl, :] = ((yc * lax.rsqrt(var + GN_EPS) * lnw + lnb + bon_ref[sl, :])
                               * (g * jax.nn.sigmoid(g))).astype(bf16)
        return carry

    lax.fori_loop(0, n_tiles // tiles_per_step, readout, 0)


def _rwkv(rw, gates, shift_w, w0, w2, a0, a2, k_k, k_a, r_k, ln_w, ln_b, n_ctx):
    b, t, d_shift = rw.shape
    d_rwkv = w0.shape[-1]
    lora2 = d_shift - 3 * d_rwkv
    nhalf = d_rwkv // HALF
    xcol = (3 * d_rwkv) // lora2
    assert (t // RW_C) % RW_UNROLL == 0 and (t // RW_C) % SCAN_UNROLL == 0
    kern = functools.partial(_rwkv_kernel, n_ctx_chunks=n_ctx // RW_C, n_chunks=t // RW_C)
    seq = lambda off: pl.BlockSpec((1, t, HALF), lambda i, j, off=off: (i, 0, off * nhalf + j))
    swb = lambda off: pl.BlockSpec((3, HALF), lambda i, j, off=off: (0, off * nhalf + j))
    vec = lambda rows: pl.BlockSpec((rows, HALF), lambda i, j: (0, j))
    lor = pl.BlockSpec((2, lora2 // 2, HALF), lambda i, j: (0, 0, j))
    return pl.pallas_call(
        kern,
        grid=(b, nhalf),
        in_specs=[seq(0), seq(1), seq(2),
                  pl.BlockSpec((1, t, lora2), lambda i, j: (i, 0, xcol)),
                  swb(0), swb(1), swb(2),
                  pl.BlockSpec((3, lora2), lambda i, j: (0, xcol)),
                  vec(2), lor, vec(2), lor, vec(1), vec(1), vec(1), vec(1), vec(1),
                  pl.BlockSpec((1, t, HALF), lambda i, j: (i, 0, gates.shape[-1] // HALF - nhalf + j))],
        out_specs=pl.BlockSpec((1, t, HALF), lambda i, j: (i, 0, j)),
        out_shape=jax.ShapeDtypeStruct((b, t, d_rwkv), bf16),
        scratch_shapes=[pltpu.VMEM((t, HALF), f32)] * 2 + [pltpu.VMEM((2, t, HALF), f32)] * 4
        + [pltpu.VMEM((2, (t // RW_C) * SUBLANES, HALF), f32)],
        compiler_params=pltpu.CompilerParams(vmem_limit_bytes=VMEM_LIMIT),
        name="rwkv7",
    )(rw, rw, rw, rw, shift_w, shift_w, shift_w, shift_w,
      w0, w2, a0, a2, k_k, k_a, r_k, ln_w, ln_b, gates)


def _outproj_kernel(*refs, n_x, final):
    ret_ref, rwo_ref = refs[:2]
    xs_refs = refs[2:2 + n_x]
    mod_ref, w_ref, fw_ref, o_ref = refs[2 + n_x:]
    mix = _dg(jnp.concatenate([ret_ref[0], rwo_ref[0]], axis=1), w_ref[...], NN)
    xn = _rows_of(xs_refs) + mod_ref[0, 0][2:3] * mix
    if final:
        ms = jnp.mean(xn * xn, axis=-1, keepdims=True)
        xn = xn * lax.rsqrt(ms + NORM_EPS) * fw_ref[...]
    o_ref[0] = xn


def _outproj(ret, rwo, xs, modv, w, fw, n_ctx, final):
    b, t, d_ret = ret.shape
    d = xs[0].shape[-1]
    d_rw = rwo.shape[-1]
    skip = (n_ctx // ROW_TILE) if final else 0
    n_tiles = t // ROW_TILE - skip
    row = lambda w_: pl.BlockSpec((1, ROW_TILE, w_), lambda i, j: (i, j + skip, 0))
    kern = functools.partial(_outproj_kernel, n_x=len(xs), final=final)
    return pl.pallas_call(
        kern,
        grid=(b, n_tiles),
        in_specs=[row(d_ret), row(d_rw)] + _row_specs(xs, skip) + [
            pl.BlockSpec((1, 1, SUBLANES, d), lambda i, j: (i, jnp.minimum(j + skip, 1), 0, 0)),
            pl.BlockSpec(w.shape, lambda i, j: (0, 0), pipeline_mode=pl.Buffered(1)),
            pl.BlockSpec((1, d), lambda i, j: (0, 0))],
        out_specs=pl.BlockSpec((1, ROW_TILE, d), lambda i, j: (i, j, 0)),
        out_shape=jax.ShapeDtypeStruct((b, n_tiles * ROW_TILE, d), f32),
        compiler_params=pltpu.CompilerParams(vmem_limit_bytes=VMEM_LIMIT),
        name="out_proj",
    )(ret, rwo, *xs, modv, w, fw)


def _rope_tables(n_ctx, seq, d_ret):
    nf = ROPE_NF
    inv = ROPE_BASE ** (-jnp.arange(nf, dtype=f32) / nf)
    pos = jnp.arange(seq)
    row_pos = (pos // GRID_W).astype(f32)
    col_pos = (pos % GRID_W).astype(f32)
    ang_r = row_pos[:, None] * inv[None, :]
    ang_c = col_pos[:, None] * inv[None, :]
    cos_h = jnp.concatenate([jnp.cos(ang_r)] * 2 + [jnp.cos(ang_c)] * 2, axis=-1)
    sin_h = jnp.concatenate([-jnp.sin(ang_r), jnp.sin(ang_r), -jnp.sin(ang_c), jnp.sin(ang_c)], axis=-1)
    reps = d_ret // HEAD
    cos_l = jnp.tile(cos_h, (1, reps))
    sin_l = jnp.tile(sin_h, (1, reps))
    cos_t = jnp.concatenate([jnp.ones((n_ctx, d_ret), f32), cos_l], axis=0)
    sin_t = jnp.concatenate([jnp.zeros((n_ctx, d_ret), f32), sin_l], axis=0)
    return cos_t, sin_t


def kernel(x, c, ctx, c_ctx, norm_w, w_mod, b_mod, w_in, ret_log_gamma, ret_norm_w, rwkv_shift_w, rwkv_w0, rwkv_w2, rwkv_a0, rwkv_a2, rwkv_k_k, rwkv_k_a, rwkv_r_k, rwkv_ln_w, rwkv_ln_b, w_out, final_norm_w):
    b, seq, d = x.shape
    n_ctx = ctx.shape[1]
    depth = w_in.shape[0]
    d_ret = ret_norm_w.shape[-1]
    d_rwkv = rwkv_w0.shape[-1]
    d_shift = rwkv_shift_w.shape[-1]

    assert b < COND_ROWS
    cond = jnp.zeros((COND_ROWS, d), f32).at[:b].set(c.astype(f32)).at[b].set(c_ctx.astype(f32))
    mods = _modulation(cond, w_mod, b_mod)
    cos_t, sin_t = _rope_tables(n_ctx, seq, d_ret)
    if depth > 1 and n_ctx == ROW_TILE:
        xs = (ctx.astype(f32), x.astype(f32))
    else:
        xs = (jnp.concatenate([ctx.astype(f32), x.astype(f32)], axis=1),)

    out = None
    for layer in range(depth):
        m = mods[layer].reshape(COND_ROWS, 3, d)
        m = jnp.stack([m[:, 1], m[:, 0], m[:, 2]], axis=1)
        lat = m[:b]
        cx = jnp.broadcast_to(m[b][None], (b, 3, d))
        modv = jnp.pad(jnp.stack([cx, lat], axis=1), ((0, 0), (0, 0), (0, SUBLANES - 3), (0, 0)))

        qkv, rw, gates = _inproj(xs, modv, norm_w[layer][None], w_in[layer].astype(bf16), cos_t, sin_t,
                                 d_ret, d_shift, d_rwkv)
        ret = _retention(qkv, gates, ret_log_gamma[layer], ret_norm_w[layer][None], n_ctx)
        rwo = _rwkv(rw, gates, rwkv_shift_w[layer], rwkv_w0[layer], rwkv_w2[layer], rwkv_a0[layer],
                    rwkv_a2[layer], rwkv_k_k[layer][None], rwkv_k_a[layer][None],
                    rwkv_r_k[layer].reshape(1, d_rwkv), rwkv_ln_w[layer][None],
                    rwkv_ln_b[layer][None], n_ctx)
        final = layer == depth - 1
        res = _outproj(ret, rwo, xs, modv, w_out[layer].astype(bf16), final_norm_w[None], n_ctx, final)
        if final:
            out = res
        else:
            xs = (res,)
    return out
```

```python
import functools
import math

import jax
import jax.numpy as jnp
from jax import lax
from jax.experimental import pallas as pl
from jax.experimental.pallas import tpu as pltpu

f32 = jnp.float32
bf16 = jnp.bfloat16

HEAD = 64
PAIR = 2 * HEAD
RET_C = 128
RW_C = 64
RW_UNROLL = 4
SCAN_UNROLL = 12
GRID_W = 64
ROPE_BASE = 10000.0
NORM_EPS = 1e-6
GN_EPS = 64e-5
ROW_TILE = 256
IN_TILES = 2
OUT_TILES = 4
HALF = 256
RET_W = 512
NP = RET_W // PAIR
RW_W = 128
RW_NP = HALF // RW_W
VMEM_LIMIT = 56 * 1024 * 1024
SUBLANES = 8
COND_ROWS = 16
MOD_TILE = 1024
ROPE_NF = HEAD // 4

NN = ((1,), (0,))
NT = ((1,), (1,))
TN = ((0,), (0,))

P_MOD = 3
P_RET = 1
P_RW = 1
P_SUM = 1


def _dg(a, b, dims):
    return lax.dot_general(a, b, (dims, ((), ())), preferred_element_type=f32)


def _split2(a):
    hi = a.astype(bf16)
    lo = (a - hi.astype(f32)).astype(bf16)
    return hi, lo


def _mm(a, b, dims=NN, passes=3):
    if passes == 1:
        return _dg(a.astype(bf16), b.astype(bf16), dims)
    ah, al = _split2(a)
    bh, bl = _split2(b)
    return _dg(ah, bh, dims) + (_dg(ah, bl, dims) + _dg(al, bh, dims))


def _cumsum_rows(x, reverse):
    n = x.shape[0]
    row = lax.broadcasted_iota(jnp.int32, (n, 1), 0)
    s = 1
    while s < n:
        if reverse:
            x = x + jnp.where(row < n - s, pltpu.roll(x, n - s, 0), 0.0)
        else:
            x = x + jnp.where(row >= s, pltpu.roll(x, s, 0), 0.0)
        s *= 2
    return x


def _group_sum(x, gmat, pieces=P_SUM):
    rows, width = x.shape
    gw = gmat.shape[0]
    cols = width // gw
    xs = jnp.concatenate([x[:, c * gw:(c + 1) * gw] for c in range(cols)], axis=0)
    if pieces == 1:
        s = _dg(xs.astype(bf16), gmat, NN)
    else:
        h, l = _split2(xs)
        s = _dg(h, gmat, NN) + _dg(l, gmat, NN)
    return jnp.concatenate([s[c * rows:(c + 1) * rows] for c in range(cols)], axis=1)


def _group_mat():
    ii = lax.broadcasted_iota(jnp.int32, (PAIR, PAIR), 0) // HEAD
    jj = lax.broadcasted_iota(jnp.int32, (PAIR, PAIR), 1) // HEAD
    return jnp.where(ii == jj, 1.0, 0.0).astype(bf16)


def _head_masks(width):
    blk = lax.broadcasted_iota(jnp.int32, (1, width), 1) // HEAD
    return [blk == h for h in range(width // HEAD)]


def _bd(x, masks):
    return jnp.concatenate([jnp.where(m, x, 0.0) for m in masks], axis=0)


def _heads_to_rows(x):
    w = x.shape[1]
    return jnp.concatenate([x[:, :HEAD]] + [pltpu.roll(x, w - h * HEAD, 1)[:, :HEAD]
                                            for h in range(1, w // HEAD)], axis=0)


def _mod_kernel(c_ref, w_ref, b_ref, o_ref):
    cnd = c_ref[...]
    s = cnd * jax.nn.sigmoid(cnd)
    o_ref[0] = _mm(s, w_ref[0], NN, P_MOD) + b_ref[0]


def _modulation(cond, w_mod, b_mod):
    depth, d, d3 = w_mod.shape
    rows = cond.shape[0]
    tn = MOD_TILE
    return pl.pallas_call(
        _mod_kernel,
        grid=(depth, d3 // tn),
        in_specs=[pl.BlockSpec((rows, d), lambda l, j: (0, 0)),
                  pl.BlockSpec((1, d, tn), lambda l, j: (l, 0, j)),
                  pl.BlockSpec((1, 1, tn), lambda l, j: (l, 0, j))],
        out_specs=pl.BlockSpec((1, rows, tn), lambda l, j: (l, 0, j)),
        out_shape=jax.ShapeDtypeStruct((depth, rows, d3), f32),
        compiler_params=pltpu.CompilerParams(vmem_limit_bytes=VMEM_LIMIT),
        name="adaln_mod",
    )(cond, w_mod, b_mod.reshape(depth, 1, d3))


def _inproj_kernel(*refs, n_x, n_tiles, d_ret, d_shift):
    g = IN_TILES
    xs_refs, refs = refs[:g * n_x], refs[g * n_x:]
    mod_refs, (nw_ref, w_ref), refs = refs[:g], refs[g:g + 2], refs[g + 2:]
    cos_refs, sin_refs, (qkv_ref, rw_ref, g_ref) = refs[:g], refs[g:2 * g], refs[2 * g:]
    hs = []
    for t in range(g):
        if n_x == 1:
            x = xs_refs[t][...]
        else:
            tile = lax.rem(pl.program_id(0) * g + t, n_tiles)
            x = jnp.where(tile == 0, xs_refs[2 * t][...], xs_refs[2 * t + 1][...])
        ms = jnp.mean(x * x, axis=-1, keepdims=True)
        m = mod_refs[t][0, 0]
        h = x * lax.rsqrt(ms + NORM_EPS) * nw_ref[...]
        hs.append((h * (1.0 + m[0:1]) + m[1:2]).astype(bf16))
    h = jnp.concatenate(hs, axis=0)

    def proj(lo, hi):
        return _dg(h, w_ref[:, lo:hi], NN)

    cos = jnp.concatenate([r[...] for r in cos_refs], axis=0)
    sin = jnp.concatenate([r[...] for r in sin_refs], axis=0)
    lane = lax.broadcasted_iota(jnp.int32, cos.shape, 1)
    first = (lane % (2 * ROPE_NF)) < ROPE_NF

    def rope(t):
        nl = t.shape[-1]
        swapped = jnp.where(first, pltpu.roll(t, nl - ROPE_NF, 1), pltpu.roll(t, ROPE_NF, 1))
        return t * cos + swapped * sin

    qkv_ref[:, 0:d_ret] = rope(proj(0, d_ret))
    qkv_ref[:, d_ret:2 * d_ret] = rope(proj(d_ret, 2 * d_ret)) * (HEAD ** -0.5)
    qkv_ref[:, 2 * d_ret:3 * d_ret] = proj(2 * d_ret, 3 * d_ret)
    g_ref[:, 0:d_ret] = proj(3 * d_ret, 4 * d_ret)
    o = 4 * d_ret
    rw_ref[...] = proj(o, o + d_shift)
    g_ref[:, d_ret:] = proj(o + d_shift, w_ref.shape[1])


def _inproj(xs, modv, nw, w, cos_t, sin_t, d_ret, d_shift, d_rwkv):
    b, d = xs[0].shape[0], xs[0].shape[-1]
    t = sum(a.shape[1] for a in xs)
    n_tiles = t // ROW_TILE
    g = IN_TILES
    assert (b * n_tiles) % g == 0
    batch = lambda s, k: lax.div(s * g + k, n_tiles)
    tile = lambda s, k: lax.rem(s * g + k, n_tiles)
    flat = [a.reshape(-1, d) for a in xs]
    x_specs = []
    for k in range(g):
        if len(xs) == 1:
            x_specs.append(pl.BlockSpec((ROW_TILE, d), lambda s, k=k: (s * g + k, 0)))
        else:
            assert xs[0].shape[1] == ROW_TILE
            x_specs += [pl.BlockSpec((ROW_TILE, d), lambda s, k=k: (batch(s, k), 0)),
                        pl.BlockSpec((ROW_TILE, d), lambda s, k=k: (
                            batch(s, k) * (n_tiles - 1) + jnp.maximum(tile(s, k) - 1, 0), 0))]
    per_tile = lambda shape, imap: [pl.BlockSpec(shape, functools.partial(imap, k=k)) for k in range(g)]
    kern = functools.partial(_inproj_kernel, n_x=len(xs), n_tiles=n_tiles, d_ret=d_ret, d_shift=d_shift)
    widths = (3 * d_ret, d_shift, d_ret + d_rwkv)
    outs = pl.pallas_call(
        kern,
        grid=(b * n_tiles // g,),
        in_specs=x_specs
        + per_tile((1, 1, SUBLANES, d), lambda s, k: (batch(s, k), jnp.minimum(tile(s, k), 1), 0, 0))
        + [pl.BlockSpec((1, d), lambda s: (0, 0)),
           pl.BlockSpec(w.shape, lambda s: (0, 0), pipeline_mode=pl.Buffered(1))]
        + per_tile((ROW_TILE, d_ret), lambda s, k: (tile(s, k), 0))
        + per_tile((ROW_TILE, d_ret), lambda s, k: (tile(s, k), 0)),
        out_specs=[pl.BlockSpec((g * ROW_TILE, n), lambda s: (s, 0)) for n in widths],
        out_shape=[jax.ShapeDtypeStruct((b * t, n), f32) for n in widths],
        compiler_params=pltpu.CompilerParams(vmem_limit_bytes=VMEM_LIMIT),
        name="in_proj",
    )(*[flat[i] for _ in range(g) for i in range(len(xs))], *([modv] * g), nw, w, *([cos_t] * g), *([sin_t] * g))
    return [o.reshape(b, t, n) for o, n in zip(outs, widths)]


def _scan_chunk(idx, forward, n_ctx_chunks, n_chunks):
    if forward:
        return idx
    return jnp.where(idx < n_ctx_chunks, n_ctx_chunks - 1 - idx, n_chunks + n_ctx_chunks - 1 - idx)


def _ret_kernel(lg_ref, q_ref, k_ref, v_ref, gate_ref, nw_ref, o_ref,
                acc_ref, s_ref, mask_ref, qd_ref, kd_ref, cd_ref, *, n_ctx_chunks, n_chunks):
    hh = pl.program_id(1)
    c = RET_C
    masks = _head_masks(PAIR)
    first = masks[0]
    ii = lax.broadcasted_iota(jnp.int32, (c, 2 * c), 0)
    nn = lax.broadcasted_iota(jnp.int32, (c, 2 * c), 1)
    ri = lax.broadcasted_iota(jnp.int32, (c, PAIR), 0).astype(f32)
    for d in range(2):
        rel = ((ii - nn % c) if d == 0 else (nn % c - ii)).astype(f32)
        for j in range(NP):
            lg0 = lg_ref[d, hh * 2 * NP + 2 * j]
            lg1 = lg_ref[d, hh * 2 * NP + 2 * j + 1]
            lgm = jnp.where(nn < c, lg0, lg1)
            lgv = jnp.where(first, lg0, lg1)
            mask_ref[d, j] = jnp.where(rel >= 0.0, jnp.exp(lgm * jnp.maximum(rel, 0.0)), 0.0)
            qpow = (ri + 1.0) if d == 0 else (c - ri)
            kpow = (c - 1.0 - ri) if d == 0 else ri
            qd_ref[d, j] = jnp.exp(lgv * qpow)
            kd_ref[d, j] = jnp.exp(lgv * kpow)
            cd_ref[d, j] = jnp.exp(jnp.broadcast_to(lgv * c, (HEAD, PAIR)))
    s_ref[...] = jnp.zeros(s_ref.shape, f32)
    acc_ref[...] = jnp.zeros(acc_ref.shape, f32)
    chains = [(d, j) for d in range(2) for j in range(NP)]

    def step(s, carry):
        rows, qp, kp, vp = {}, {}, {}, {}
        for d in range(2):
            ch = _scan_chunk(s, d == 0, n_ctx_chunks, n_chunks)
            r0 = pl.multiple_of(ch * c, c)
            rows[d] = pl.ds(r0, c)
            q = q_ref[0, rows[d], :]
            k = k_ref[0, rows[d], :]
            v = v_ref[0, rows[d], :]
            for j in range(NP):
                col = slice(j * PAIR, (j + 1) * PAIR)
                qp[d, j], kp[d, j], vp[d, j] = q[:, col], k[:, col], v[:, col]
        sc, st, vb = {}, {}, {}
        for ck in chains:
            sc[ck] = _mm(qp[ck], _bd(kp[ck], masks), NT, P_RET) * mask_ref[ck[0], ck[1]]
            st[ck] = s_ref[ck[0], ck[1]]
            vb[ck] = _bd(vp[ck], masks)
        for (d, j) in chains:
            ck = (d, j)
            o = _mm(sc[ck], vb[ck], NN, P_RET) + _mm(qp[ck] * qd_ref[d, j], _bd(st[ck], masks), NN, P_RET)
            s_ref[d, j] = st[ck] * cd_ref[d, j] + _mm(_heads_to_rows(kp[ck] * kd_ref[d, j]), vb[ck], TN, P_RET)
            acc_ref[rows[d], j * PAIR:(j + 1) * PAIR] += o
        return carry

    lax.fori_loop(0, n_chunks, step, 0)

    gmat = _group_mat()
    nw = nw_ref[...]

    n_tiles = (n_chunks * c) // ROW_TILE
    tiles_per_step = 3 if n_tiles % 3 == 0 else 1

    def norm_tiles(i, carry):
        sls = [pl.ds(pl.multiple_of((i * tiles_per_step + u) * ROW_TILE, ROW_TILE), ROW_TILE)
               for u in range(tiles_per_step)]
        os_ = [acc_ref[sl, :] for sl in sls]
        mss = [_group_sum(o * o, gmat) * (1.0 / HEAD) for o in os_]
        for sl, o, ms in zip(sls, os_, mss):
            g = gate_ref[0, sl, :]
            o_ref[0, sl, :] = ((o * lax.rsqrt(ms + NORM_EPS) * nw) * (g * jax.nn.sigmoid(g))).astype(bf16)
        return carry

    lax.fori_loop(0, n_tiles // tiles_per_step, norm_tiles, 0)


def _retention(qkv, gates, log_gamma, norm_w, n_ctx):
    b, t, w3 = qkv.shape
    d_ret = w3 // 3
    nhalf = d_ret // RET_W
    kern = functools.partial(_ret_kernel, n_ctx_chunks=n_ctx // RET_C, n_chunks=t // RET_C)
    blk = lambda off: pl.BlockSpec((1, t, RET_W), lambda i, j, off=off: (i, 0, off * nhalf + j))
    return pl.pallas_call(
        kern,
        grid=(b, nhalf),
        in_specs=[pl.BlockSpec(memory_space=pltpu.SMEM),
                  blk(0), blk(1), blk(2),
                  pl.BlockSpec((1, t, RET_W), lambda i, j: (i, 0, j)),
                  pl.BlockSpec((1, RET_W), lambda i, j: (0, j))],
        out_specs=pl.BlockSpec((1, t, RET_W), lambda i, j: (i, 0, j)),
        out_shape=jax.ShapeDtypeStruct((b, t, d_ret), bf16),
        scratch_shapes=[pltpu.VMEM((t, RET_W), f32),
                        pltpu.VMEM((2, NP, HEAD, PAIR), f32),
                        pltpu.VMEM((2, NP, RET_C, 2 * RET_C), f32),
                        pltpu.VMEM((2, NP, RET_C, PAIR), f32),
                        pltpu.VMEM((2, NP, RET_C, PAIR), f32),
                        pltpu.VMEM((2, NP, HEAD, PAIR), f32)],
        compiler_params=pltpu.CompilerParams(vmem_limit_bytes=VMEM_LIMIT),
        name="retention",
    )(log_gamma, qkv, qkv, qkv, gates, norm_w)


class _Pair:
    def __init__(self, x, masks):
        self.x = x
        self.masks = masks
        self._parts = None
        self._lhs = None
        self._bd = None

    def parts(self):
        if self._parts is None:
            hi = self.x.astype(bf16).astype(f32)
            self._parts = (hi, self.x - hi)
        return self._parts

    def lhs(self):
        if self._lhs is None:
            if P_RW == 1:
                self._lhs = self.x.astype(bf16)
            else:
                hi, lo = self.parts()
                self._lhs = jnp.concatenate([hi, hi, lo], axis=1).astype(bf16)
        return self._lhs

    def _bd_parts(self):
        if self._bd is None:
            if P_RW == 1:
                self._bd = (_bd(self.x, self.masks).astype(bf16),)
            else:
                hi, lo = self.parts()
                bh = _bd(hi, self.masks).astype(bf16)
                self._bd = (bh, _bd(lo, self.masks).astype(bf16), bh)
        return self._bd

    def rhs(self):
        return jnp.concatenate(self._bd_parts(), axis=0)

    def rhs_nt(self):
        return jnp.concatenate(self._bd_parts(), axis=1)

    def lhs_rows(self):
        if P_RW == 1:
            return _heads_to_rows(self.x).astype(bf16)
        hi, lo = self.parts()
        th = _heads_to_rows(hi)
        return jnp.concatenate([th, th, _heads_to_rows(lo)], axis=0).astype(bf16)


def _lhs_rows(x):
    if P_RW == 1:
        return x.astype(bf16)
    hi = x.astype(bf16).astype(f32)
    return jnp.concatenate([hi, hi, x - hi], axis=0).astype(bf16)


def _mmp(a, b):
    return _dg(a.lhs(), b.rhs(), NN)


def _rwkv_kernel(r_ref, k_ref, v_ref, xwa_ref, swr_ref, swk_ref, swv_ref, swx_ref,
                 w0_ref, w2_ref, a0_ref, a2_ref, kk_ref, ka_ref, rk_ref, lnw_ref, lnb_ref, gate_ref,
                 o_ref, acc_ref, bon_ref,
                 g_ref, h_ref, q_ref, y_ref, pc_ref,
                 *, n_ctx_chunks, n_chunks):
    c = RW_C
    t_rows = n_chunks * c
    gmat = _group_mat()
    rows = lax.broadcasted_iota(jnp.int32, (c, 1), 0)

    def conv(ref, sw_ref, ch, r0):
        x = ref[0, pl.ds(r0, c), :]
        p0 = pl.multiple_of(jnp.maximum(r0 - SUBLANES, 0), SUBLANES)
        n0 = pl.multiple_of(jnp.minimum(r0 + c, t_rows - SUBLANES), SUBLANES)
        prev8 = ref[0, pl.ds(p0, SUBLANES), :]
        next8 = ref[0, pl.ds(n0, SUBLANES), :]
        has_prev = jnp.logical_and(ch != 0, ch != n_ctx_chunks).astype(f32)
        has_next = jnp.logical_and(ch != n_ctx_chunks - 1, ch != n_chunks - 1).astype(f32)
        xd = jnp.where(rows == 0, prev8[SUBLANES - 1:SUBLANES, :] * has_prev, pltpu.roll(x, 1, 0))
        xu = jnp.where(rows == c - 1, next8[0:1, :] * has_next, pltpu.roll(x, c - 1, 0))
        sw = sw_ref[...]
        return sw[0:1] * xd + sw[1:2] * x + sw[2:3] * xu

    masks = _head_masks(RW_W)
    ii = lax.broadcasted_iota(jnp.int32, (c, RW_W), 0)
    jj = lax.broadcasted_iota(jnp.int32, (c, RW_W), 1) % HEAD
    eye = jnp.where(ii == jj, 1.0, 0.0)
    bd16 = (ii // 16) == (jj // 16)
    same32 = (ii // 32) == (jj // 32)
    off32 = jnp.logical_and(same32, jnp.logical_not(bd16))
    off64 = jnp.logical_not(same32)
    strict = (jj < ii, jj > ii)
    incl = (jj <= ii, jj >= ii)
    both = tuple(jnp.concatenate([strict[d], incl[d]], axis=0) for d in range(2))
    lora = xwa_ref.shape[2] // 2
    wscale = -math.exp(-0.5)
    mk = lambda x: _Pair(x, masks)
    col = lambda j: slice(j * RW_W, (j + 1) * RW_W)

    n_groups = n_chunks // RW_UNROLL

    def features(grp):
        out = []
        rs, ks, vs, xs = [], [], [], []
        for u in range(RW_UNROLL):
            ch = grp * RW_UNROLL + u
            r0 = pl.multiple_of(ch * c, c)
            rs.append(conv(r_ref, swr_ref, ch, r0))
            ks.append(conv(k_ref, swk_ref, ch, r0))
            vs.append(conv(v_ref, swv_ref, ch, r0))
            xs.append(conv(xwa_ref, swx_ref, ch, r0))
        kks = [k * kk_ref[...] for k in ks]
        sums = _group_sum(jnp.concatenate([kk * kk for kk in kks]
                                          + [r * k * rk_ref[...] for r, k in zip(rs, ks)], axis=0), gmat)
        xall = jnp.concatenate(xs, axis=0)
        txw = jnp.tanh(xall[:, :lora])
        kaps = []
        for u in range(RW_UNROLL):
            ch = grp * RW_UNROLL + u
            kaps.append(kks[u] / jnp.maximum(jnp.sqrt(sums[u * c:(u + 1) * c]), 1e-12))
            bon_ref[pl.ds(pl.multiple_of(ch * c, c), c), :] = sums[(RW_UNROLL + u) * c:(RW_UNROLL + u + 1) * c] * vs[u]
            out.append(vs[u])
        for d in range(2):
            lw_all = wscale * jax.nn.sigmoid(w0_ref[d:d + 1, :] + _mm(txw, w2_ref[d], NN, P_RW))
            a_all = jax.nn.sigmoid(a0_ref[d:d + 1, :] + _mm(xall[:, lora:], a2_ref[d], NN, P_RW))
            for u in range(RW_UNROLL):
                ch = grp * RW_UNROLL + u
                lw = lw_all[u * c:(u + 1) * c]
                a = a_all[u * c:(u + 1) * c]
                kmod = ks[u] * (1.0 + (a - 1.0) * ka_ref[...])
                b = kaps[u] * a
                lp = _cumsum_rows(lw, reverse=(d == 1))
                lpc = lp[c - 1:c, :] if d == 0 else lp[0:1, :]
                pinv = jnp.exp(-lp)
                ptail = jnp.exp(lpc - lp)
                pc_ref[d, pl.ds(pl.multiple_of(ch * SUBLANES, SUBLANES), SUBLANES), :] = jnp.broadcast_to(
                    jnp.exp(lpc), (SUBLANES, lp.shape[1]))
                out += [jnp.concatenate([kaps[u] * jnp.exp(lp - lw), rs[u] * jnp.exp(lp)], axis=0),
                        kmod * pinv, b * pinv, kmod * ptail, b * ptail]
        return out

    def prepare(it, carry):
        cur = features(it)
        feat = {}
        for u in range(RW_UNROLL):
            sl = pl.ds(pl.multiple_of((it * RW_UNROLL + u) * c, c), c)
            for d in range(2):
                kr, kd, bd, kh, bh = cur[RW_UNROLL + 5 * (d * RW_UNROLL + u):RW_UNROLL + 5 * (d * RW_UNROLL + u + 1)]
                feat[u, d] = dict(sl=sl, v=cur[u], kr=kr, rq=kr[c:], kd=kd, bd=bd, kh=kh, bh=bh)

        chains = [(u, d, j) for u in range(RW_UNROLL) for d in range(2) for j in range(RW_NP)]
        mkv, m_b, n_b, vp = {}, {}, {}, {}
        for ck in chains:
            u, d, j = ck
            f = feat[u, d]
            kr = mk(f["kr"][:, col(j)])
            mk_ = _dg(kr.lhs(), mk(f["kd"][:, col(j)]).rhs_nt(), NT)
            mb_ = _dg(kr.lhs(), mk(f["bd"][:, col(j)]).rhs_nt(), NT)
            vp[ck] = mk(f["v"][:, col(j)])
            mkv[ck] = _mmp(mk(jnp.where(both[d], mk_, 0.0)), vp[ck])
            m_b[ck] = jnp.where(strict[d], mb_[:c], 0.0)
            n_b[ck] = mk(jnp.where(incl[d], mb_[c:], 0.0))
        p, npow = {}, {}
        for ck in chains:
            nd = mk(jnp.where(bd16, -m_b[ck], 0.0))
            p[ck] = eye + nd.x
            npow[ck] = mk(_mmp(nd, nd))
        for level in range(2):
            for ck in chains:
                both_ = _mmp(mk(jnp.concatenate([p[ck], npow[ck].x], axis=0)), npow[ck])
                p[ck] = p[ck] + both_[:c]
                npow[ck] = mk(both_[c:])
        for ck in chains:
            p[ck] = p[ck] + _mmp(mk(p[ck]), npow[ck])
        for off in (off32, off64):
            tmp, pm = {}, {}
            for ck in chains:
                pm[ck] = mk(p[ck])
                tmp[ck] = mk(_mmp(pm[ck], mk(jnp.where(off, m_b[ck], 0.0))))
            for ck in chains:
                p[ck] = p[ck] - _mmp(tmp[ck], pm[ck])
        ta = {}
        for ck in chains:
            u, d, j = ck
            kq = feat[u, d]["kr"][:c, col(j)]
            ta[ck] = _dg(mk(p[ck]).lhs(), jnp.concatenate([mk(kq).rhs(), mk(mkv[ck][:c]).rhs()], axis=1), NN)
        for ck in chains:
            u, d, j = ck
            f = feat[u, d]
            a1, z0 = ta[ck][:, :RW_W], ta[ck][:, RW_W:]
            nb = _dg(n_b[ck].lhs(), jnp.concatenate([mk(a1).rhs(), mk(z0).rhs()], axis=1), NN)
            az = jnp.concatenate([_heads_to_rows(a1), _heads_to_rows(z0)], axis=1)
            bh = mk(f["bh"][:, col(j)])
            ab = _dg(_lhs_rows(az), bh.rhs(), TN)
            vk = _dg(vp[ck].lhs_rows(), mk(f["kh"][:, col(j)]).rhs(), TN)
            g_ref[d, f["sl"], col(j)] = -ab[:HEAD]
            h_ref[d, f["sl"], col(j)] = vk - ab[HEAD:]
            q_ref[d, f["sl"], col(j)] = f["rq"][:, col(j)] - nb[:, :RW_W]
            y_ref[d, f["sl"], col(j)] = mkv[ck][c:] - nb[:, RW_W:]
        return carry

    lax.fori_loop(0, n_groups, prepare, 0)

    acc_ref[...] = jnp.zeros(acc_ref.shape, f32)

    def scan(it, st):
        st = list(st)
        for u in range(SCAN_UNROLL):
            for d in range(2):
                ch = _scan_chunk(it * SCAN_UNROLL + u, d == 0, n_ctx_chunks, n_chunks)
                sl = pl.ds(pl.multiple_of(ch * c, c), c)
                pc = pc_ref[d, pl.ds(pl.multiple_of(ch * SUBLANES, SUBLANES), SUBLANES), :][0:1]
                g, h, q, y0 = g_ref[d, sl, :], h_ref[d, sl, :], q_ref[d, sl, :], y_ref[d, sl, :]
                for j in range(RW_NP):
                    s0 = st[d * RW_NP + j]
                    sp = mk(s0)
                    y = _dg(mk(q[:, col(j)]).lhs(), sp.rhs_nt(), NT) + y0[:, col(j)]
                    st[d * RW_NP + j] = (s0 * pc[:, col(j)] + _mmp(sp, mk(g[:, col(j)]))) + h[:, col(j)]
                    acc_ref[sl, col(j)] += y
        return tuple(st)

    lax.fori_loop(0, n_chunks // SCAN_UNROLL, scan,
                  tuple(jnp.zeros((HEAD, RW_W), f32) for _ in range(2 * RW_NP)))

    lnw = lnw_ref[...]
    lnb = lnb_ref[...]

    n_tiles = t_rows // ROW_TILE
    tiles_per_step = 3 if n_tiles % 3 == 0 else 1

    def readout(i, carry):
        sls = [pl.ds(pl.multiple_of((i * tiles_per_step + u) * ROW_TILE, ROW_TILE), ROW_TILE)
               for u in range(tiles_per_step)]
        ys = [acc_ref[sl, :] for sl in sls]
        ycs = [y - _group_sum(y, gmat) * (1.0 / HEAD) for y in ys]
        vrs = [_group_sum(yc * yc, gmat) * (1.0 / HEAD) for yc in ycs]
        for sl, yc, var in zip(sls, ycs, vrs):
            g = gate_ref[0, sl, :]
            o_ref[0, sl, :] = ((yc * lax.rsqrt(var + GN_EPS) * lnw + lnb + bon_ref[sl, :])
                               * (g * jax.nn.sigmoid(g))).astype(bf16)
        return carry

    lax.fori_loop(0, n_tiles // tiles_per_step, readout, 0)


def _rwkv(rw, gates, shift_w, w0, w2, a0, a2, k_k, k_a, r_k, ln_w, ln_b, n_ctx):
    b, t, d_shift = rw.shape
    d_rwkv = w0.shape[-1]
    lora2 = d_shift - 3 * d_rwkv
    nhalf = d_rwkv // HALF
    xcol = (3 * d_rwkv) // lora2
    assert (t // RW_C) % RW_UNROLL == 0 and (t // RW_C) % SCAN_UNROLL == 0
    kern = functools.partial(_rwkv_kernel, n_ctx_chunks=n_ctx // RW_C, n_chunks=t // RW_C)
    seq = lambda off: pl.BlockSpec((1, t, HALF), lambda i, j, off=off: (i, 0, off * nhalf + j))
    swb = lambda off: pl.BlockSpec((3, HALF), lambda i, j, off=off: (0, off * nhalf + j))
    vec = lambda rows: pl.BlockSpec((rows, HALF), lambda i, j: (0, j))
    lor = pl.BlockSpec((2, lora2 // 2, HALF), lambda i, j: (0, 0, j))
    return pl.pallas_call(
        kern,
        grid=(b, nhalf),
        in_specs=[seq(0), seq(1), seq(2),
                  pl.BlockSpec((1, t, lora2), lambda i, j: (i, 0, xcol)),
                  swb(0), swb(1), swb(2),
                  pl.BlockSpec((3, lora2), lambda i, j: (0, xcol)),
                  vec(2), lor, vec(2), lor, vec(1), vec(1), vec(1), vec(1), vec(1),
                  pl.BlockSpec((1, t, HALF), lambda i, j: (i, 0, gates.shape[-1] // HALF - nhalf + j))],
        out_specs=pl.BlockSpec((1, t, HALF), lambda i, j: (i, 0, j)),
        out_shape=jax.ShapeDtypeStruct((b, t, d_rwkv), bf16),
        scratch_shapes=[pltpu.VMEM((t, HALF), f32)] * 2 + [pltpu.VMEM((2, t, HALF), f32)] * 4
        + [pltpu.VMEM((2, (t // RW_C) * SUBLANES, HALF), f32)],
        compiler_params=pltpu.CompilerParams(vmem_limit_bytes=VMEM_LIMIT),
        name="rwkv7",
    )(rw, rw, rw, rw, shift_w, shift_w, shift_w, shift_w,
      w0, w2, a0, a2, k_k, k_a, r_k, ln_w, ln_b, gates)


def _outproj_kernel(*refs, n_x, n_out, skip, final):
    g = OUT_TILES
    ret_refs, rwo_refs, refs = refs[:g], refs[g:2 * g], refs[2 * g:]
    xs_refs, refs = refs[:g * n_x], refs[g * n_x:]
    mod_refs, (w_ref, fw_ref, o_ref) = refs[:g], refs[g:]
    act = jnp.concatenate([jnp.concatenate([ret_refs[t][...], rwo_refs[t][...]], axis=1) for t in range(g)], axis=0)
    mix = _dg(act, w_ref[...], NN)
    xs, gates = [], []
    for t in range(g):
        if n_x == 1:
            xs.append(xs_refs[t][...])
        else:
            tile = lax.rem(pl.program_id(0) * g + t, n_out) + skip
            xs.append(jnp.where(tile == 0, xs_refs[2 * t][...], xs_refs[2 * t + 1][...]))
        gates.append(jnp.broadcast_to(mod_refs[t][0, 0][2:3], xs[-1].shape))
    xn = jnp.concatenate(xs, axis=0) + jnp.concatenate(gates, axis=0) * mix
    if final:
        ms = jnp.mean(xn * xn, axis=-1, keepdims=True)
        xn = xn * lax.rsqrt(ms + NORM_EPS) * fw_ref[...]
    o_ref[...] = xn


def _outproj(ret, rwo, xs, modv, w, fw, n_ctx, final):
    b, t, d_ret = ret.shape
    d = xs[0].shape[-1]
    d_rw = rwo.shape[-1]
    skip = (n_ctx // ROW_TILE) if final else 0
    n_tiles = t // ROW_TILE
    n_out = n_tiles - skip
    g = OUT_TILES
    assert (b * n_out) % g == 0
    batch = lambda s, k: lax.div(s * g + k, n_out)
    tile = lambda s, k: lax.rem(s * g + k, n_out) + skip
    row_of = lambda s, k: batch(s, k) * n_tiles + tile(s, k)
    per_tile = lambda shape, imap: [pl.BlockSpec(shape, functools.partial(imap, k=k)) for k in range(g)]
    x_specs = []
    for k in range(g):
        if len(xs) == 1:
            x_specs.append(pl.BlockSpec((ROW_TILE, d), functools.partial(lambda s, k: (row_of(s, k), 0), k=k)))
        else:
            assert skip == 0 and xs[0].shape[1] == ROW_TILE
            x_specs += [pl.BlockSpec((ROW_TILE, d), lambda s, k=k: (batch(s, k), 0)),
                        pl.BlockSpec((ROW_TILE, d), lambda s, k=k: (
                            batch(s, k) * (n_tiles - 1) + jnp.maximum(tile(s, k) - 1, 0), 0))]
    flat = [a.reshape(-1, d) for a in xs]
    kern = functools.partial(_outproj_kernel, n_x=len(xs), n_out=n_out, skip=skip, final=final)
    out = pl.pallas_call(
        kern,
        grid=(b * n_out // g,),
        in_specs=per_tile((ROW_TILE, d_ret), lambda s, k: (row_of(s, k), 0))
        + per_tile((ROW_TILE, d_rw), lambda s, k: (row_of(s, k), 0))
        + x_specs
        + per_tile((1, 1, SUBLANES, d), lambda s, k: (batch(s, k), jnp.minimum(tile(s, k), 1), 0, 0))
        + [pl.BlockSpec(w.shape, lambda s: (0, 0), pipeline_mode=pl.Buffered(1)),
           pl.BlockSpec((1, d), lambda s: (0, 0))],
        out_specs=pl.BlockSpec((g * ROW_TILE, d), lambda s: (s, 0)),
        out_shape=jax.ShapeDtypeStruct((b * n_out * ROW_TILE, d), f32),
        compiler_params=pltpu.CompilerParams(vmem_limit_bytes=VMEM_LIMIT),
        name="out_proj",
    )(*([ret.reshape(-1, d_ret)] * g), *([rwo.reshape(-1, d_rw)] * g),
      *[flat[i] for _ in range(g) for i in range(len(xs))], *([modv] * g), w, fw)
    return out.reshape(b, n_out * ROW_TILE, d)


def _rope_tables(n_ctx, seq, d_ret):
    nf = ROPE_NF
    inv = ROPE_BASE ** (-jnp.arange(nf, dtype=f32) / nf)
    pos = jnp.arange(seq)
    row_pos = (pos // GRID_W).astype(f32)
    col_pos = (pos % GRID_W).astype(f32)
    ang_r = row_pos[:, None] * inv[None, :]
    ang_c = col_pos[:, None] * inv[None, :]
    cos_h = jnp.concatenate([jnp.cos(ang_r)] * 2 + [jnp.cos(ang_c)] * 2, axis=-1)
    sin_h = jnp.concatenate([-jnp.sin(ang_r), jnp.sin(ang_r), -jnp.sin(ang_c), jnp.sin(ang_c)], axis=-1)
    reps = d_ret // HEAD
    cos_l = jnp.tile(cos_h, (1, reps))
    sin_l = jnp.tile(sin_h, (1, reps))
    cos_t = jnp.concatenate([jnp.ones((n_ctx, d_ret), f32), cos_l], axis=0)
    sin_t = jnp.concatenate([jnp.zeros((n_ctx, d_ret), f32), sin_l], axis=0)
    return cos_t, sin_t


def kernel(x, c, ctx, c_ctx, norm_w, w_mod, b_mod, w_in, ret_log_gamma, ret_norm_w, rwkv_shift_w, rwkv_w0, rwkv_w2, rwkv_a0, rwkv_a2, rwkv_k_k, rwkv_k_a, rwkv_r_k, rwkv_ln_w, rwkv_ln_b, w_out, final_norm_w):
    b, seq, d = x.shape
    n_ctx = ctx.shape[1]
    depth = w_in.shape[0]
    d_ret = ret_norm_w.shape[-1]
    d_rwkv = rwkv_w0.shape[-1]
    d_shift = rwkv_shift_w.shape[-1]

    assert b < COND_ROWS
    cond = jnp.zeros((COND_ROWS, d), f32).at[:b].set(c.astype(f32)).at[b].set(c_ctx.astype(f32))
    mods = _modulation(cond, w_mod, b_mod)
    cos_t, sin_t = _rope_tables(n_ctx, seq, d_ret)
    if depth > 1 and n_ctx == ROW_TILE:
        xs = (ctx.astype(f32), x.astype(f32))
    else:
        xs = (jnp.concatenate([ctx.astype(f32), x.astype(f32)], axis=1),)

    out = None
    for layer in range(depth):
        m = mods[layer].reshape(COND_ROWS, 3, d)
        m = jnp.stack([m[:, 1], m[:, 0], m[:, 2]], axis=1)
        lat = m[:b]
        cx = jnp.broadcast_to(m[b][None], (b, 3, d))
        modv = jnp.pad(jnp.stack([cx, lat], axis=1), ((0, 0), (0, 0), (0, SUBLANES - 3), (0, 0)))

        qkv, rw, gates = _inproj(xs, modv, norm_w[layer][None], w_in[layer].astype(bf16), cos_t, sin_t,
                                 d_ret, d_shift, d_rwkv)
        ret = _retention(qkv, gates, ret_log_gamma[layer], ret_norm_w[layer][None], n_ctx)
        rwo = _rwkv(rw, gates, rwkv_shift_w[layer], rwkv_w0[layer], rwkv_w2[layer], rwkv_a0[layer],
                    rwkv_a2[layer], rwkv_k_k[layer][None], rwkv_k_a[layer][None],
                    rwkv_r_k[layer].reshape(1, d_rwkv), rwkv_ln_w[layer][None],
                    rwkv_ln_b[layer][None], n_ctx)
        final = layer == depth - 1
        res = _outproj(ret, rwo, xs, modv, w_out[layer].astype(bf16), final_norm_w[None], n_ctx, final)
        if final:
            out = res
        else:
            xs = (res,)
    return out
```

```python
import functools
import math

import jax
import jax.numpy as jnp
from jax import lax
from jax.experimental import pallas as pl
from jax.experimental.pallas import tpu as pltpu

f32 = jnp.float32
bf16 = jnp.bfloat16

HEAD = 64
PAIR = 2 * HEAD
RET_C = 128
RW_C = 64
RW_UNROLL = 4
SCAN_UNROLL = 12
GRID_W = 64
ROPE_BASE = 10000.0
NORM_EPS = 1e-6
GN_EPS = 64e-5
ROW_TILE = 256
IN_TILES = 2
OUT_TILES = 4
HALF = 256
RET_W = 512
NP = RET_W // PAIR
RW_W = 128
RW_NP = HALF // RW_W
VMEM_LIMIT = 56 * 1024 * 1024
SUBLANES = 8
COND_ROWS = 16
MOD_TILE = 1024
ROPE_NF = HEAD // 4

NN = ((1,), (0,))
NT = ((1,), (1,))
TN = ((0,), (0,))

P_MOD = 3
P_RET = 1
P_RW = 1
P_SUM = 1


def _dg(a, b, dims):
    return lax.dot_general(a, b, (dims, ((), ())), preferred_element_type=f32)


def _split2(a):
    hi = a.astype(bf16)
    lo = (a - hi.astype(f32)).astype(bf16)
    return hi, lo


def _mm(a, b, dims=NN, passes=3):
    if passes == 1:
        return _dg(a.astype(bf16), b.astype(bf16), dims)
    ah, al = _split2(a)
    bh, bl = _split2(b)
    return _dg(ah, bh, dims) + (_dg(ah, bl, dims) + _dg(al, bh, dims))


def _cumsum_rows(x, reverse):
    n = x.shape[0]
    row = lax.broadcasted_iota(jnp.int32, (n, 1), 0)
    s = 1
    while s < n:
        if reverse:
            x = x + jnp.where(row < n - s, pltpu.roll(x, n - s, 0), 0.0)
        else:
            x = x + jnp.where(row >= s, pltpu.roll(x, s, 0), 0.0)
        s *= 2
    return x


def _group_sum(x, gmat, pieces=P_SUM):
    rows, width = x.shape
    gw = gmat.shape[0]
    cols = width // gw
    xs = jnp.concatenate([x[:, c * gw:(c + 1) * gw] for c in range(cols)], axis=0)
    if pieces == 1:
        s = _dg(xs.astype(bf16), gmat, NN)
    else:
        h, l = _split2(xs)
        s = _dg(h, gmat, NN) + _dg(l, gmat, NN)
    return jnp.concatenate([s[c * rows:(c + 1) * rows] for c in range(cols)], axis=1)


def _group_mat():
    ii = lax.broadcasted_iota(jnp.int32, (PAIR, PAIR), 0) // HEAD
    jj = lax.broadcasted_iota(jnp.int32, (PAIR, PAIR), 1) // HEAD
    return jnp.where(ii == jj, 1.0, 0.0).astype(bf16)


def _head_masks(width):
    blk = lax.broadcasted_iota(jnp.int32, (1, width), 1) // HEAD
    return [blk == h for h in range(width // HEAD)]


def _bd(x, masks):
    return jnp.concatenate([jnp.where(m, x, 0.0) for m in masks], axis=0)


def _heads_to_rows(x):
    w = x.shape[1]
    return jnp.concatenate([x[:, :HEAD]] + [pltpu.roll(x, w - h * HEAD, 1)[:, :HEAD]
                                            for h in range(1, w // HEAD)], axis=0)


def _mod_kernel(c_ref, w_ref, b_ref, o_ref):
    cnd = c_ref[...]
    s = cnd * jax.nn.sigmoid(cnd)
    o_ref[0] = _mm(s, w_ref[0], NN, P_MOD) + b_ref[0]


def _modulation(cond, w_mod, b_mod):
    depth, d, d3 = w_mod.shape
    rows = cond.shape[0]
    tn = MOD_TILE
    return pl.pallas_call(
        _mod_kernel,
        grid=(depth, d3 // tn),
        in_specs=[pl.BlockSpec((rows, d), lambda l, j: (0, 0)),
                  pl.BlockSpec((1, d, tn), lambda l, j: (l, 0, j)),
                  pl.BlockSpec((1, 1, tn), lambda l, j: (l, 0, j))],
        out_specs=pl.BlockSpec((1, rows, tn), lambda l, j: (l, 0, j)),
        out_shape=jax.ShapeDtypeStruct((depth, rows, d3), f32),
        compiler_params=pltpu.CompilerParams(vmem_limit_bytes=VMEM_LIMIT),
        name="adaln_mod",
    )(cond, w_mod, b_mod.reshape(depth, 1, d3))


def _inproj_kernel(*refs, n_x, n_tiles, d_ret, d_shift):
    g = IN_TILES
    xs_refs, refs = refs[:g * n_x], refs[g * n_x:]
    mod_refs, (nw_ref, w_ref), refs = refs[:g], refs[g:g + 2], refs[g + 2:]
    cos_refs, sin_refs, (qkv_ref, rw_ref, g_ref) = refs[:g], refs[g:2 * g], refs[2 * g:]
    hs = []
    for t in range(g):
        if n_x == 1:
            x = xs_refs[t][...]
        else:
            tile = lax.rem(pl.program_id(0) * g + t, n_tiles)
            x = jnp.where(tile == 0, xs_refs[2 * t][...], xs_refs[2 * t + 1][...])
        ms = jnp.mean(x * x, axis=-1, keepdims=True)
        m = mod_refs[t][0, 0]
        h = x * lax.rsqrt(ms + NORM_EPS) * nw_ref[...]
        hs.append((h * (1.0 + m[0:1]) + m[1:2]).astype(bf16))
    def proj(lo, hi):
        return jnp.concatenate([_dg(h, w_ref[:, lo:hi], NN) for h in hs], axis=0)

    cos = jnp.concatenate([r[...] for r in cos_refs], axis=0)
    sin = jnp.concatenate([r[...] for r in sin_refs], axis=0)
    lane = lax.broadcasted_iota(jnp.int32, cos.shape, 1)
    first = (lane % (2 * ROPE_NF)) < ROPE_NF

    def rope(t):
        nl = t.shape[-1]
        swapped = jnp.where(first, pltpu.roll(t, nl - ROPE_NF, 1), pltpu.roll(t, ROPE_NF, 1))
        return t * cos + swapped * sin

    qkv_ref[:, 0:d_ret] = rope(proj(0, d_ret))
    qkv_ref[:, d_ret:2 * d_ret] = rope(proj(d_ret, 2 * d_ret)) * (HEAD ** -0.5)
    qkv_ref[:, 2 * d_ret:3 * d_ret] = proj(2 * d_ret, 3 * d_ret)
    g_ref[:, 0:d_ret] = proj(3 * d_ret, 4 * d_ret)
    o = 4 * d_ret
    rw_ref[...] = proj(o, o + d_shift)
    g_ref[:, d_ret:] = proj(o + d_shift, w_ref.shape[1])


def _inproj(xs, modv, nw, w, cos_t, sin_t, d_ret, d_shift, d_rwkv):
    b, d = xs[0].shape[0], xs[0].shape[-1]
    t = sum(a.shape[1] for a in xs)
    n_tiles = t // ROW_TILE
    g = IN_TILES
    assert (b * n_tiles) % g == 0
    batch = lambda s, k: lax.div(s * g + k, n_tiles)
    tile = lambda s, k: lax.rem(s * g + k, n_tiles)
    flat = [a.reshape(-1, d) for a in xs]
    x_specs = []
    for k in range(g):
        if len(xs) == 1:
            x_specs.append(pl.BlockSpec((ROW_TILE, d), lambda s, k=k: (s * g + k, 0)))
        else:
            assert xs[0].shape[1] == ROW_TILE
            x_specs += [pl.BlockSpec((ROW_TILE, d), lambda s, k=k: (batch(s, k), 0)),
                        pl.BlockSpec((ROW_TILE, d), lambda s, k=k: (
                            batch(s, k) * (n_tiles - 1) + jnp.maximum(tile(s, k) - 1, 0), 0))]
    per_tile = lambda shape, imap: [pl.BlockSpec(shape, functools.partial(imap, k=k)) for k in range(g)]
    kern = functools.partial(_inproj_kernel, n_x=len(xs), n_tiles=n_tiles, d_ret=d_ret, d_shift=d_shift)
    widths = (3 * d_ret, d_shift, d_ret + d_rwkv)
    outs = pl.pallas_call(
        kern,
        grid=(b * n_tiles // g,),
        in_specs=x_specs
        + per_tile((1, 1, SUBLANES, d), lambda s, k: (batch(s, k), jnp.minimum(tile(s, k), 1), 0, 0))
        + [pl.BlockSpec((1, d), lambda s: (0, 0)),
           pl.BlockSpec(w.shape, lambda s: (0, 0), pipeline_mode=pl.Buffered(1))]
        + per_tile((ROW_TILE, d_ret), lambda s, k: (tile(s, k), 0))
        + per_tile((ROW_TILE, d_ret), lambda s, k: (tile(s, k), 0)),
        out_specs=[pl.BlockSpec((g * ROW_TILE, n), lambda s: (s, 0)) for n in widths],
        out_shape=[jax.ShapeDtypeStruct((b * t, n), f32) for n in widths],
        compiler_params=pltpu.CompilerParams(vmem_limit_bytes=VMEM_LIMIT),
        name="in_proj",
    )(*[flat[i] for _ in range(g) for i in range(len(xs))], *([modv] * g), nw, w, *([cos_t] * g), *([sin_t] * g))
    return [o.reshape(b, t, n) for o, n in zip(outs, widths)]


def _scan_chunk(idx, forward, n_ctx_chunks, n_chunks):
    if forward:
        return idx
    return jnp.where(idx < n_ctx_chunks, n_ctx_chunks - 1 - idx, n_chunks + n_ctx_chunks - 1 - idx)


def _ret_kernel(lg_ref, q_ref, k_ref, v_ref, gate_ref, nw_ref, o_ref,
                acc_ref, s_ref, mask_ref, qd_ref, kd_ref, cd_ref, *, n_ctx_chunks, n_chunks):
    hh = pl.program_id(1)
    c = RET_C
    masks = _head_masks(PAIR)
    first = masks[0]
    ii = lax.broadcasted_iota(jnp.int32, (c, 2 * c), 0)
    nn = lax.broadcasted_iota(jnp.int32, (c, 2 * c), 1)
    ri = lax.broadcasted_iota(jnp.int32, (c, PAIR), 0).astype(f32)
    for d in range(2):
        rel = ((ii - nn % c) if d == 0 else (nn % c - ii)).astype(f32)
        for j in range(NP):
            lg0 = lg_ref[d, hh * 2 * NP + 2 * j]
            lg1 = lg_ref[d, hh * 2 * NP + 2 * j + 1]
            lgm = jnp.where(nn < c, lg0, lg1)
            lgv = jnp.where(first, lg0, lg1)
            mask_ref[d, j] = jnp.where(rel >= 0.0, jnp.exp(lgm * jnp.maximum(rel, 0.0)), 0.0)
            qpow = (ri + 1.0) if d == 0 else (c - ri)
            kpow = (c - 1.0 - ri) if d == 0 else ri
            qd_ref[d, j] = jnp.exp(lgv * qpow)
            kd_ref[d, j] = jnp.exp(lgv * kpow)
            cd_ref[d, j] = jnp.exp(jnp.broadcast_to(lgv * c, (HEAD, PAIR)))
    s_ref[...] = jnp.zeros(s_ref.shape, f32)
    acc_ref[...] = jnp.zeros(acc_ref.shape, f32)
    chains = [(d, j) for d in range(2) for j in range(NP)]

    def step(s, carry):
        rows, qp, kp, vp = {}, {}, {}, {}
        for d in range(2):
            ch = _scan_chunk(s, d == 0, n_ctx_chunks, n_chunks)
            r0 = pl.multiple_of(ch * c, c)
            rows[d] = pl.ds(r0, c)
            q = q_ref[0, rows[d], :]
            k = k_ref[0, rows[d], :]
            v = v_ref[0, rows[d], :]
            for j in range(NP):
                col = slice(j * PAIR, (j + 1) * PAIR)
                qp[d, j], kp[d, j], vp[d, j] = q[:, col], k[:, col], v[:, col]
        sc, st, vb = {}, {}, {}
        for ck in chains:
            sc[ck] = _mm(qp[ck], _bd(kp[ck], masks), NT, P_RET) * mask_ref[ck[0], ck[1]]
            st[ck] = s_ref[ck[0], ck[1]]
            vb[ck] = _bd(vp[ck], masks)
        for (d, j) in chains:
            ck = (d, j)
            o = _mm(sc[ck], vb[ck], NN, P_RET) + _mm(qp[ck] * qd_ref[d, j], _bd(st[ck], masks), NN, P_RET)
            s_ref[d, j] = st[ck] * cd_ref[d, j] + _mm(_heads_to_rows(kp[ck] * kd_ref[d, j]), vb[ck], TN, P_RET)
            acc_ref[rows[d], j * PAIR:(j + 1) * PAIR] += o
        return carry

    lax.fori_loop(0, n_chunks, step, 0)

    gmat = _group_mat()
    nw = nw_ref[...]

    n_tiles = (n_chunks * c) // ROW_TILE
    tiles_per_step = 3 if n_tiles % 3 == 0 else 1

    def norm_tiles(i, carry):
        sls = [pl.ds(pl.multiple_of((i * tiles_per_step + u) * ROW_TILE, ROW_TILE), ROW_TILE)
               for u in range(tiles_per_step)]
        os_ = [acc_ref[sl, :] for sl in sls]
        mss = [_group_sum(o * o, gmat) * (1.0 / HEAD) for o in os_]
        for sl, o, ms in zip(sls, os_, mss):
            g = gate_ref[0, sl, :]
            o_ref[0, sl, :] = ((o * lax.rsqrt(ms + NORM_EPS) * nw) * (g * jax.nn.sigmoid(g))).astype(bf16)
        return carry

    lax.fori_loop(0, n_tiles // tiles_per_step, norm_tiles, 0)


def _retention(qkv, gates, log_gamma, norm_w, n_ctx):
    b, t, w3 = qkv.shape
    d_ret = w3 // 3
    nhalf = d_ret // RET_W
    kern = functools.partial(_ret_kernel, n_ctx_chunks=n_ctx // RET_C, n_chunks=t // RET_C)
    blk = lambda off: pl.BlockSpec((1, t, RET_W), lambda i, j, off=off: (i, 0, off * nhalf + j))
    return pl.pallas_call(
        kern,
        grid=(b, nhalf),
        in_specs=[pl.BlockSpec(memory_space=pltpu.SMEM),
                  blk(0), blk(1), blk(2),
                  pl.BlockSpec((1, t, RET_W), lambda i, j: (i, 0, j)),
                  pl.BlockSpec((1, RET_W), lambda i, j: (0, j))],
        out_specs=pl.BlockSpec((1, t, RET_W), lambda i, j: (i, 0, j)),
        out_shape=jax.ShapeDtypeStruct((b, t, d_ret), bf16),
        scratch_shapes=[pltpu.VMEM((t, RET_W), f32),
                        pltpu.VMEM((2, NP, HEAD, PAIR), f32),
                        pltpu.VMEM((2, NP, RET_C, 2 * RET_C), f32),
                        pltpu.VMEM((2, NP, RET_C, PAIR), f32),
                        pltpu.VMEM((2, NP, RET_C, PAIR), f32),
                        pltpu.VMEM((2, NP, HEAD, PAIR), f32)],
        compiler_params=pltpu.CompilerParams(vmem_limit_bytes=VMEM_LIMIT),
        name="retention",
    )(log_gamma, qkv, qkv, qkv, gates, norm_w)


class _Pair:
    def __init__(self, x, masks):
        self.x = x
        self.masks = masks
        self._parts = None
        self._lhs = None
        self._bd = None

    def parts(self):
        if self._parts is None:
            hi = self.x.astype(bf16).astype(f32)
            self._parts = (hi, self.x - hi)
        return self._parts

    def lhs(self):
        if self._lhs is None:
            if P_RW == 1:
                self._lhs = self.x.astype(bf16)
            else:
                hi, lo = self.parts()
                self._lhs = jnp.concatenate([hi, hi, lo], axis=1).astype(bf16)
        return self._lhs

    def _bd_parts(self):
        if self._bd is None:
            if P_RW == 1:
                self._bd = (_bd(self.x, self.masks).astype(bf16),)
            else:
                hi, lo = self.parts()
                bh = _bd(hi, self.masks).astype(bf16)
                self._bd = (bh, _bd(lo, self.masks).astype(bf16), bh)
        return self._bd

    def rhs(self):
        return jnp.concatenate(self._bd_parts(), axis=0)

    def rhs_nt(self):
        return jnp.concatenate(self._bd_parts(), axis=1)

    def lhs_rows(self):
        if P_RW == 1:
            return _heads_to_rows(self.x).astype(bf16)
        hi, lo = self.parts()
        th = _heads_to_rows(hi)
        return jnp.concatenate([th, th, _heads_to_rows(lo)], axis=0).astype(bf16)


def _lhs_rows(x):
    if P_RW == 1:
        return x.astype(bf16)
    hi = x.astype(bf16).astype(f32)
    return jnp.concatenate([hi, hi, x - hi], axis=0).astype(bf16)


def _mmp(a, b):
    return _dg(a.lhs(), b.rhs(), NN)


def _rwkv_kernel(r_ref, k_ref, v_ref, xwa_ref, swr_ref, swk_ref, swv_ref, swx_ref,
                 w0_ref, w2_ref, a0_ref, a2_ref, kk_ref, ka_ref, rk_ref, lnw_ref, lnb_ref, gate_ref,
                 o_ref, acc_ref, bon_ref,
                 g_ref, h_ref, q_ref, y_ref, pc_ref,
                 *, n_ctx_chunks, n_chunks):
    c = RW_C
    t_rows = n_chunks * c
    gmat = _group_mat()
    rows = lax.broadcasted_iota(jnp.int32, (c, 1), 0)

    def conv(ref, sw_ref, ch, r0):
        x = ref[0, pl.ds(r0, c), :]
        p0 = pl.multiple_of(jnp.maximum(r0 - SUBLANES, 0), SUBLANES)
        n0 = pl.multiple_of(jnp.minimum(r0 + c, t_rows - SUBLANES), SUBLANES)
        prev8 = ref[0, pl.ds(p0, SUBLANES), :]
        next8 = ref[0, pl.ds(n0, SUBLANES), :]
        has_prev = jnp.logical_and(ch != 0, ch != n_ctx_chunks).astype(f32)
        has_next = jnp.logical_and(ch != n_ctx_chunks - 1, ch != n_chunks - 1).astype(f32)
        xd = jnp.where(rows == 0, prev8[SUBLANES - 1:SUBLANES, :] * has_prev, pltpu.roll(x, 1, 0))
        xu = jnp.where(rows == c - 1, next8[0:1, :] * has_next, pltpu.roll(x, c - 1, 0))
        sw = sw_ref[...]
        return sw[0:1] * xd + sw[1:2] * x + sw[2:3] * xu

    masks = _head_masks(RW_W)
    ii = lax.broadcasted_iota(jnp.int32, (c, RW_W), 0)
    jj = lax.broadcasted_iota(jnp.int32, (c, RW_W), 1) % HEAD
    eye = jnp.where(ii == jj, 1.0, 0.0)
    bd16 = (ii // 16) == (jj // 16)
    same32 = (ii // 32) == (jj // 32)
    off32 = jnp.logical_and(same32, jnp.logical_not(bd16))
    off64 = jnp.logical_not(same32)
    strict = (jj < ii, jj > ii)
    incl = (jj <= ii, jj >= ii)
    both = tuple(jnp.concatenate([strict[d], incl[d]], axis=0) for d in range(2))
    lora = xwa_ref.shape[2] // 2
    wscale = -math.exp(-0.5)
    mk = lambda x: _Pair(x, masks)
    col = lambda j: slice(j * RW_W, (j + 1) * RW_W)

    n_groups = n_chunks // RW_UNROLL

    def features(grp):
        out = []
        rs, ks, vs, xs = [], [], [], []
        for u in range(RW_UNROLL):
            ch = grp * RW_UNROLL + u
            r0 = pl.multiple_of(ch * c, c)
            rs.append(conv(r_ref, swr_ref, ch, r0))
            ks.append(conv(k_ref, swk_ref, ch, r0))
            vs.append(conv(v_ref, swv_ref, ch, r0))
            xs.append(conv(xwa_ref, swx_ref, ch, r0))
        kks = [k * kk_ref[...] for k in ks]
        sums = _group_sum(jnp.concatenate([kk * kk for kk in kks]
                                          + [r * k * rk_ref[...] for r, k in zip(rs, ks)], axis=0), gmat)
        xall = jnp.concatenate(xs, axis=0)
        txw = jnp.tanh(xall[:, :lora])
        kaps = []
        for u in range(RW_UNROLL):
            ch = grp * RW_UNROLL + u
            kaps.append(kks[u] / jnp.maximum(jnp.sqrt(sums[u * c:(u + 1) * c]), 1e-12))
            bon_ref[pl.ds(pl.multiple_of(ch * c, c), c), :] = sums[(RW_UNROLL + u) * c:(RW_UNROLL + u + 1) * c] * vs[u]
            out.append(vs[u])
        for d in range(2):
            lw_all = wscale * jax.nn.sigmoid(w0_ref[d:d + 1, :] + _mm(txw, w2_ref[d], NN, P_RW))
            a_all = jax.nn.sigmoid(a0_ref[d:d + 1, :] + _mm(xall[:, lora:], a2_ref[d], NN, P_RW))
            for u in range(RW_UNROLL):
                ch = grp * RW_UNROLL + u
                lw = lw_all[u * c:(u + 1) * c]
                a = a_all[u * c:(u + 1) * c]
                kmod = ks[u] * (1.0 + (a - 1.0) * ka_ref[...])
                b = kaps[u] * a
                lp = _cumsum_rows(lw, reverse=(d == 1))
                lpc = lp[c - 1:c, :] if d == 0 else lp[0:1, :]
                pinv = jnp.exp(-lp)
                ptail = jnp.exp(lpc - lp)
                pc_ref[d, pl.ds(pl.multiple_of(ch * SUBLANES, SUBLANES), SUBLANES), :] = jnp.broadcast_to(
                    jnp.exp(lpc), (SUBLANES, lp.shape[1]))
                out += [jnp.concatenate([kaps[u] * jnp.exp(lp - lw), rs[u] * jnp.exp(lp)], axis=0),
                        kmod * pinv, b * pinv, kmod * ptail, b * ptail]
        return out

    def prepare(it, carry):
        cur = features(it)
        feat = {}
        for u in range(RW_UNROLL):
            sl = pl.ds(pl.multiple_of((it * RW_UNROLL + u) * c, c), c)
            for d in range(2):
                kr, kd, bd, kh, bh = cur[RW_UNROLL + 5 * (d * RW_UNROLL + u):RW_UNROLL + 5 * (d * RW_UNROLL + u + 1)]
                feat[u, d] = dict(sl=sl, v=cur[u], kr=kr, rq=kr[c:], kd=kd, bd=bd, kh=kh, bh=bh)

        chains = [(u, d, j) for u in range(RW_UNROLL) for d in range(2) for j in range(RW_NP)]
        mkv, m_b, n_b, vp = {}, {}, {}, {}
        for ck in chains:
            u, d, j = ck
            f = feat[u, d]
            kr = mk(f["kr"][:, col(j)])
            mk_ = _dg(kr.lhs(), mk(f["kd"][:, col(j)]).rhs_nt(), NT)
            mb_ = _dg(kr.lhs(), mk(f["bd"][:, col(j)]).rhs_nt(), NT)
            vp[ck] = mk(f["v"][:, col(j)])
            mkv[ck] = _mmp(mk(jnp.where(both[d], mk_, 0.0)), vp[ck])
            m_b[ck] = jnp.where(strict[d], mb_[:c], 0.0)
            n_b[ck] = mk(jnp.where(incl[d], mb_[c:], 0.0))
        p, npow = {}, {}
        for ck in chains:
            nd = mk(jnp.where(bd16, -m_b[ck], 0.0))
            p[ck] = eye + nd.x
            npow[ck] = mk(_mmp(nd, nd))
        for level in range(2):
            for ck in chains:
                both_ = _mmp(mk(jnp.concatenate([p[ck], npow[ck].x], axis=0)), npow[ck])
                p[ck] = p[ck] + both_[:c]
                npow[ck] = mk(both_[c:])
        for ck in chains:
            p[ck] = p[ck] + _mmp(mk(p[ck]), npow[ck])
        for off in (off32, off64):
            tmp, pm = {}, {}
            for ck in chains:
                pm[ck] = mk(p[ck])
                tmp[ck] = mk(_mmp(pm[ck], mk(jnp.where(off, m_b[ck], 0.0))))
            for ck in chains:
                p[ck] = p[ck] - _mmp(tmp[ck], pm[ck])
        ta = {}
        for ck in chains:
            u, d, j = ck
            kq = feat[u, d]["kr"][:c, col(j)]
            ta[ck] = _dg(mk(p[ck]).lhs(), jnp.concatenate([mk(kq).rhs(), mk(mkv[ck][:c]).rhs()], axis=1), NN)
        for ck in chains:
            u, d, j = ck
            f = feat[u, d]
            a1, z0 = ta[ck][:, :RW_W], ta[ck][:, RW_W:]
            nb = _dg(n_b[ck].lhs(), jnp.concatenate([mk(a1).rhs(), mk(z0).rhs()], axis=1), NN)
            az = jnp.concatenate([_heads_to_rows(a1), _heads_to_rows(z0)], axis=1)
            bh = mk(f["bh"][:, col(j)])
            ab = _dg(_lhs_rows(az), bh.rhs(), TN)
            vk = _dg(vp[ck].lhs_rows(), mk(f["kh"][:, col(j)]).rhs(), TN)
            g_ref[d, f["sl"], col(j)] = -ab[:HEAD]
            h_ref[d, f["sl"], col(j)] = vk - ab[HEAD:]
            q_ref[d, f["sl"], col(j)] = f["rq"][:, col(j)] - nb[:, :RW_W]
            y_ref[d, f["sl"], col(j)] = mkv[ck][c:] - nb[:, RW_W:]
        return carry

    lax.fori_loop(0, n_groups, prepare, 0)

    acc_ref[...] = jnp.zeros(acc_ref.shape, f32)

    def scan(it, st):
        st = list(st)
        for u in range(SCAN_UNROLL):
            for d in range(2):
                ch = _scan_chunk(it * SCAN_UNROLL + u, d == 0, n_ctx_chunks, n_chunks)
                sl = pl.ds(pl.multiple_of(ch * c, c), c)
                pc = pc_ref[d, pl.ds(pl.multiple_of(ch * SUBLANES, SUBLANES), SUBLANES), :][0:1]
                g, h, q, y0 = g_ref[d, sl, :], h_ref[d, sl, :], q_ref[d, sl, :], y_ref[d, sl, :]
                for j in range(RW_NP):
                    s0 = st[d * RW_NP + j]
                    sp = mk(s0)
                    y = _dg(mk(q[:, col(j)]).lhs(), sp.rhs_nt(), NT) + y0[:, col(j)]
                    st[d * RW_NP + j] = (s0 * pc[:, col(j)] + _mmp(sp, mk(g[:, col(j)]))) + h[:, col(j)]
                    acc_ref[sl, col(j)] += y
        return tuple(st)

    lax.fori_loop(0, n_chunks // SCAN_UNROLL, scan,
                  tuple(jnp.zeros((HEAD, RW_W), f32) for _ in range(2 * RW_NP)))

    lnw = lnw_ref[...]
    lnb = lnb_ref[...]

    n_tiles = t_rows // ROW_TILE
    tiles_per_step = 3 if n_tiles % 3 == 0 else 1

    def readout(i, carry):
        sls = [pl.ds(pl.multiple_of((i * tiles_per_step + u) * ROW_TILE, ROW_TILE), ROW_TILE)
               for u in range(tiles_per_step)]
        ys = [acc_ref[sl, :] for sl in sls]
        ycs = [y - _group_sum(y, gmat) * (1.0 / HEAD) for y in ys]
        vrs = [_group_sum(yc * yc, gmat) * (1.0 / HEAD) for yc in ycs]
        for sl, yc, var in zip(sls, ycs, vrs):
            g = gate_ref[0, sl, :]
            o_ref[0, sl, :] = ((yc * lax.rsqrt(var + GN_EPS) * lnw + lnb + bon_ref[sl, :])
                               * (g * jax.nn.sigmoid(g))).astype(bf16)
        return carry

    lax.fori_loop(0, n_tiles // tiles_per_step, readout, 0)


def _rwkv(rw, gates, shift_w, w0, w2, a0, a2, k_k, k_a, r_k, ln_w, ln_b, n_ctx):
    b, t, d_shift = rw.shape
    d_rwkv = w0.shape[-1]
    lora2 = d_shift - 3 * d_rwkv
    nhalf = d_rwkv // HALF
    xcol = (3 * d_rwkv) // lora2
    assert (t // RW_C) % RW_UNROLL == 0 and (t // RW_C) % SCAN_UNROLL == 0
    kern = functools.partial(_rwkv_kernel, n_ctx_chunks=n_ctx // RW_C, n_chunks=t // RW_C)
    seq = lambda off: pl.BlockSpec((1, t, HALF), lambda i, j, off=off: (i, 0, off * nhalf + j))
    swb = lambda off: pl.BlockSpec((3, HALF), lambda i, j, off=off: (0, off * nhalf + j))
    vec = lambda rows: pl.BlockSpec((rows, HALF), lambda i, j: (0, j))
    lor = pl.BlockSpec((2, lora2 // 2, HALF), lambda i, j: (0, 0, j))
    return pl.pallas_call(
        kern,
        grid=(b, nhalf),
        in_specs=[seq(0), seq(1), seq(2),
                  pl.BlockSpec((1, t, lora2), lambda i, j: (i, 0, xcol)),
                  swb(0), swb(1), swb(2),
                  pl.BlockSpec((3, lora2), lambda i, j: (0, xcol)),
                  vec(2), lor, vec(2), lor, vec(1), vec(1), vec(1), vec(1), vec(1),
                  pl.BlockSpec((1, t, HALF), lambda i, j: (i, 0, gates.shape[-1] // HALF - nhalf + j))],
        out_specs=pl.BlockSpec((1, t, HALF), lambda i, j: (i, 0, j)),
        out_shape=jax.ShapeDtypeStruct((b, t, d_rwkv), bf16),
        scratch_shapes=[pltpu.VMEM((t, HALF), f32)] * 2 + [pltpu.VMEM((2, t, HALF), f32)] * 4
        + [pltpu.VMEM((2, (t // RW_C) * SUBLANES, HALF), f32)],
        compiler_params=pltpu.CompilerParams(vmem_limit_bytes=VMEM_LIMIT),
        name="rwkv7",
    )(rw, rw, rw, rw, shift_w, shift_w, shift_w, shift_w,
      w0, w2, a0, a2, k_k, k_a, r_k, ln_w, ln_b, gates)


def _outproj_kernel(*refs, n_x, n_out, skip, final):
    g = OUT_TILES
    ret_refs, rwo_refs, refs = refs[:g], refs[g:2 * g], refs[2 * g:]
    xs_refs, refs = refs[:g * n_x], refs[g * n_x:]
    mod_refs, (w_ref, fw_ref, o_ref) = refs[:g], refs[g:]
    act = jnp.concatenate([jnp.concatenate([ret_refs[t][...], rwo_refs[t][...]], axis=1) for t in range(g)], axis=0)
    mix = _dg(act, w_ref[...], NN)
    xs, gates = [], []
    for t in range(g):
        if n_x == 1:
            xs.append(xs_refs[t][...])
        else:
            tile = lax.rem(pl.program_id(0) * g + t, n_out) + skip
            xs.append(jnp.where(tile == 0, xs_refs[2 * t][...], xs_refs[2 * t + 1][...]))
        gates.append(jnp.broadcast_to(mod_refs[t][0, 0][2:3], xs[-1].shape))
    xn = jnp.concatenate(xs, axis=0) + jnp.concatenate(gates, axis=0) * mix
    if final:
        ms = jnp.mean(xn * xn, axis=-1, keepdims=True)
        xn = xn * lax.rsqrt(ms + NORM_EPS) * fw_ref[...]
    o_ref[...] = xn


def _outproj(ret, rwo, xs, modv, w, fw, n_ctx, final):
    b, t, d_ret = ret.shape
    d = xs[0].shape[-1]
    d_rw = rwo.shape[-1]
    skip = (n_ctx // ROW_TILE) if final else 0
    n_tiles = t // ROW_TILE
    n_out = n_tiles - skip
    g = OUT_TILES
    assert (b * n_out) % g == 0
    batch = lambda s, k: lax.div(s * g + k, n_out)
    tile = lambda s, k: lax.rem(s * g + k, n_out) + skip
    row_of = lambda s, k: batch(s, k) * n_tiles + tile(s, k)
    per_tile = lambda shape, imap: [pl.BlockSpec(shape, functools.partial(imap, k=k)) for k in range(g)]
    x_specs = []
    for k in range(g):
        if len(xs) == 1:
            x_specs.append(pl.BlockSpec((ROW_TILE, d), functools.partial(lambda s, k: (row_of(s, k), 0), k=k)))
        else:
            assert skip == 0 and xs[0].shape[1] == ROW_TILE
            x_specs += [pl.BlockSpec((ROW_TILE, d), lambda s, k=k: (batch(s, k), 0)),
                        pl.BlockSpec((ROW_TILE, d), lambda s, k=k: (
                            batch(s, k) * (n_tiles - 1) + jnp.maximum(tile(s, k) - 1, 0), 0))]
    flat = [a.reshape(-1, d) for a in xs]
    kern = functools.partial(_outproj_kernel, n_x=len(xs), n_out=n_out, skip=skip, final=final)
    out = pl.pallas_call(
        kern,
        grid=(b * n_out // g,),
        in_specs=per_tile((ROW_TILE, d_ret), lambda s, k: (row_of(s, k), 0))
        + per_tile((ROW_TILE, d_rw), lambda s, k: (row_of(s, k), 0))
        + x_specs
        + per_tile((1, 1, SUBLANES, d), lambda s, k: (batch(s, k), jnp.minimum(tile(s, k), 1), 0, 0))
        + [pl.BlockSpec(w.shape, lambda s: (0, 0), pipeline_mode=pl.Buffered(1)),
           pl.BlockSpec((1, d), lambda s: (0, 0))],
        out_specs=pl.BlockSpec((g * ROW_TILE, d), lambda s: (s, 0)),
        out_shape=jax.ShapeDtypeStruct((b * n_out * ROW_TILE, d), f32),
        compiler_params=pltpu.CompilerParams(vmem_limit_bytes=VMEM_LIMIT),
        name="out_proj",
    )(*([ret.reshape(-1, d_ret)] * g), *([rwo.reshape(-1, d_rw)] * g),
      *[flat[i] for _ in range(g) for i in range(len(xs))], *([modv] * g), w, fw)
    return out.reshape(b, n_out * ROW_TILE, d)


def _rope_tables(n_ctx, seq, d_ret):
    nf = ROPE_NF
    inv = ROPE_BASE ** (-jnp.arange(nf, dtype=f32) / nf)
    pos = jnp.arange(seq)
    row_pos = (pos // GRID_W).astype(f32)
    col_pos = (pos % GRID_W).astype(f32)
    ang_r = row_pos[:, None] * inv[None, :]
    ang_c = col_pos[:, None] * inv[None, :]
    cos_h = jnp.concatenate([jnp.cos(ang_r)] * 2 + [jnp.cos(ang_c)] * 2, axis=-1)
    sin_h = jnp.concatenate([-jnp.sin(ang_r), jnp.sin(ang_r), -jnp.sin(ang_c), jnp.sin(ang_c)], axis=-1)
    reps = d_ret // HEAD
    cos_l = jnp.tile(cos_h, (1, reps))
    sin_l = jnp.tile(sin_h, (1, reps))
    cos_t = jnp.concatenate([jnp.ones((n_ctx, d_ret), f32), cos_l], axis=0)
    sin_t = jnp.concatenate([jnp.zeros((n_ctx, d_ret), f32), sin_l], axis=0)
    return cos_t, sin_t


def kernel(x, c, ctx, c_ctx, norm_w, w_mod, b_mod, w_in, ret_log_gamma, ret_norm_w, rwkv_shift_w, rwkv_w0, rwkv_w2, rwkv_a0, rwkv_a2, rwkv_k_k, rwkv_k_a, rwkv_r_k, rwkv_ln_w, rwkv_ln_b, w_out, final_norm_w):
    b, seq, d = x.shape
    n_ctx = ctx.shape[1]
    depth = w_in.shape[0]
    d_ret = ret_norm_w.shape[-1]
    d_rwkv = rwkv_w0.shape[-1]
    d_shift = rwkv_shift_w.shape[-1]

    assert b < COND_ROWS
    cond = jnp.zeros((COND_ROWS, d), f32).at[:b].set(c.astype(f32)).at[b].set(c_ctx.astype(f32))
    mods = _modulation(cond, w_mod, b_mod)
    cos_t, sin_t = _rope_tables(n_ctx, seq, d_ret)
    if depth > 1 and n_ctx == ROW_TILE:
        xs = (ctx.astype(f32), x.astype(f32))
    else:
        xs = (jnp.concatenate([ctx.astype(f32), x.astype(f32)], axis=1),)

    out = None
    for layer in range(depth):
        m = mods[layer].reshape(COND_ROWS, 3, d)
        m = jnp.stack([m[:, 1], m[:, 0], m[:, 2]], axis=1)
        lat = m[:b]
        cx = jnp.broadcast_to(m[b][None], (b, 3, d))
        modv = jnp.pad(jnp.stack([cx, lat], axis=1), ((0, 0), (0, 0), (0, SUBLANES - 3), (0, 0)))

        qkv, rw, gates = _inproj(xs, modv, norm_w[layer][None], w_in[layer].astype(bf16), cos_t, sin_t,
                                 d_ret, d_shift, d_rwkv)
        ret = _retention(qkv, gates, ret_log_gamma[layer], ret_norm_w[layer][None], n_ctx)
        rwo = _rwkv(rw, gates, rwkv_shift_w[layer], rwkv_w0[layer], rwkv_w2[layer], rwkv_a0[layer],
                    rwkv_a2[layer], rwkv_k_k[layer][None], rwkv_k_a[layer][None],
                    rwkv_r_k[layer].reshape(1, d_rwkv), rwkv_ln_w[layer][None],
                    rwkv_ln_b[layer][None], n_ctx)
        final = layer == depth - 1
        res = _outproj(ret, rwo, xs, modv, w_out[layer].astype(bf16), final_norm_w[None], n_ctx, final)
        if final:
            out = res
        else:
            xs = (res,)
    return out
```

```python
import functools
import math

import jax
import jax.numpy as jnp
from jax import lax
from jax.experimental import pallas as pl
from jax.experimental.pallas import tpu as pltpu

f32 = jnp.float32
bf16 = jnp.bfloat16

HEAD = 64
PAIR = 2 * HEAD
RET_C = 128
RW_C = 64
INV_BLOCK = 32
RW_UNROLL = 4
SCAN_UNROLL = 12
GRID_W = 64
ROPE_BASE = 10000.0
NORM_EPS = 1e-6
GN_EPS = 64e-5
ROW_TILE = 256
IN_TILES = 2
OUT_TILES = 4
HALF = 256
RET_W = 512
NP = RET_W // PAIR
RW_W = 128
RW_NP = HALF // RW_W
VMEM_LIMIT = 56 * 1024 * 1024
SUBLANES = 8
COND_ROWS = 16
MOD_TILE = 1024
ROPE_NF = HEAD // 4

NN = ((1,), (0,))
NT = ((1,), (1,))
TN = ((0,), (0,))

P_MOD = 3
P_RET = 1
P_RW = 1
P_SUM = 1


def _dg(a, b, dims):
    return lax.dot_general(a, b, (dims, ((), ())), preferred_element_type=f32)


def _split2(a):
    hi = a.astype(bf16)
    lo = (a - hi.astype(f32)).astype(bf16)
    return hi, lo


def _mm(a, b, dims=NN, passes=3):
    if passes == 1:
        return _dg(a.astype(bf16), b.astype(bf16), dims)
    ah, al = _split2(a)
    bh, bl = _split2(b)
    return _dg(ah, bh, dims) + (_dg(ah, bl, dims) + _dg(al, bh, dims))


def _cumsum_rows(x, reverse):
    n = x.shape[0]
    row = lax.broadcasted_iota(jnp.int32, (n, 1), 0)
    s = 1
    while s < n:
        if reverse:
            x = x + jnp.where(row < n - s, pltpu.roll(x, n - s, 0), 0.0)
        else:
            x = x + jnp.where(row >= s, pltpu.roll(x, s, 0), 0.0)
        s *= 2
    return x


def _group_sum(x, gmat, pieces=P_SUM):
    rows, width = x.shape
    gw = gmat.shape[0]
    cols = width // gw
    xs = jnp.concatenate([x[:, c * gw:(c + 1) * gw] for c in range(cols)], axis=0)
    if pieces == 1:
        s = _dg(xs.astype(bf16), gmat, NN)
    else:
        h, l = _split2(xs)
        s = _dg(h, gmat, NN) + _dg(l, gmat, NN)
    return jnp.concatenate([s[c * rows:(c + 1) * rows] for c in range(cols)], axis=1)


def _group_mat():
    ii = lax.broadcasted_iota(jnp.int32, (PAIR, PAIR), 0) // HEAD
    jj = lax.broadcasted_iota(jnp.int32, (PAIR, PAIR), 1) // HEAD
    return jnp.where(ii == jj, 1.0, 0.0).astype(bf16)


def _head_masks(width):
    blk = lax.broadcasted_iota(jnp.int32, (1, width), 1) // HEAD
    return [blk == h for h in range(width // HEAD)]


def _bd(x, masks):
    return jnp.concatenate([jnp.where(m, x, 0.0) for m in masks], axis=0)


def _heads_to_rows(x):
    w = x.shape[1]
    return jnp.concatenate([x[:, :HEAD]] + [pltpu.roll(x, w - h * HEAD, 1)[:, :HEAD]
                                            for h in range(1, w // HEAD)], axis=0)


def _mod_kernel(c_ref, w_ref, b_ref, o_ref):
    cnd = c_ref[...]
    s = cnd * jax.nn.sigmoid(cnd)
    o_ref[0] = _mm(s, w_ref[0], NN, P_MOD) + b_ref[0]


def _modulation(cond, w_mod, b_mod):
    depth, d, d3 = w_mod.shape
    rows = cond.shape[0]
    tn = MOD_TILE
    return pl.pallas_call(
        _mod_kernel,
        grid=(depth, d3 // tn),
        in_specs=[pl.BlockSpec((rows, d), lambda l, j: (0, 0)),
                  pl.BlockSpec((1, d, tn), lambda l, j: (l, 0, j)),
                  pl.BlockSpec((1, 1, tn), lambda l, j: (l, 0, j))],
        out_specs=pl.BlockSpec((1, rows, tn), lambda l, j: (l, 0, j)),
        out_shape=jax.ShapeDtypeStruct((depth, rows, d3), f32),
        compiler_params=pltpu.CompilerParams(vmem_limit_bytes=VMEM_LIMIT),
        name="adaln_mod",
    )(cond, w_mod, b_mod.reshape(depth, 1, d3))


def _inproj_kernel(*refs, n_x, n_tiles, d_ret, d_shift):
    g = IN_TILES
    xs_refs, refs = refs[:g * n_x], refs[g * n_x:]
    mod_refs, (nw_ref, w_ref), refs = refs[:g], refs[g:g + 2], refs[g + 2:]
    cos_refs, sin_refs, (qkv_ref, rw_ref, g_ref) = refs[:g], refs[g:2 * g], refs[2 * g:]
    hs = []
    for t in range(g):
        if n_x == 1:
            x = xs_refs[t][...]
        else:
            tile = lax.rem(pl.program_id(0) * g + t, n_tiles)
            x = jnp.where(tile == 0, xs_refs[2 * t][...], xs_refs[2 * t + 1][...])
        ms = jnp.mean(x * x, axis=-1, keepdims=True)
        m = mod_refs[t][0, 0]
        h = x * lax.rsqrt(ms + NORM_EPS) * nw_ref[...]
        hs.append((h * (1.0 + m[0:1]) + m[1:2]).astype(bf16))
    def proj(lo, hi):
        return jnp.concatenate([_dg(h, w_ref[:, lo:hi], NN) for h in hs], axis=0)

    cos = jnp.concatenate([r[...] for r in cos_refs], axis=0)
    sin = jnp.concatenate([r[...] for r in sin_refs], axis=0)
    lane = lax.broadcasted_iota(jnp.int32, cos.shape, 1)
    first = (lane % (2 * ROPE_NF)) < ROPE_NF

    def rope(t):
        nl = t.shape[-1]
        swapped = jnp.where(first, pltpu.roll(t, nl - ROPE_NF, 1), pltpu.roll(t, ROPE_NF, 1))
        return t * cos + swapped * sin

    qkv_ref[:, 0:d_ret] = rope(proj(0, d_ret))
    qkv_ref[:, d_ret:2 * d_ret] = rope(proj(d_ret, 2 * d_ret)) * (HEAD ** -0.5)
    qkv_ref[:, 2 * d_ret:3 * d_ret] = proj(2 * d_ret, 3 * d_ret)
    g_ref[:, 0:d_ret] = proj(3 * d_ret, 4 * d_ret)
    o = 4 * d_ret
    rw_ref[...] = proj(o, o + d_shift)
    g_ref[:, d_ret:] = proj(o + d_shift, w_ref.shape[1])


def _inproj(xs, modv, nw, w, cos_t, sin_t, d_ret, d_shift, d_rwkv):
    b, d = xs[0].shape[0], xs[0].shape[-1]
    t = sum(a.shape[1] for a in xs)
    n_tiles = t // ROW_TILE
    g = IN_TILES
    assert (b * n_tiles) % g == 0
    batch = lambda s, k: lax.div(s * g + k, n_tiles)
    tile = lambda s, k: lax.rem(s * g + k, n_tiles)
    flat = [a.reshape(-1, d) for a in xs]
    x_specs = []
    for k in range(g):
        if len(xs) == 1:
            x_specs.append(pl.BlockSpec((ROW_TILE, d), lambda s, k=k: (s * g + k, 0)))
        else:
            assert xs[0].shape[1] == ROW_TILE
            x_specs += [pl.BlockSpec((ROW_TILE, d), lambda s, k=k: (batch(s, k), 0)),
                        pl.BlockSpec((ROW_TILE, d), lambda s, k=k: (
                            batch(s, k) * (n_tiles - 1) + jnp.maximum(tile(s, k) - 1, 0), 0))]
    per_tile = lambda shape, imap: [pl.BlockSpec(shape, functools.partial(imap, k=k)) for k in range(g)]
    kern = functools.partial(_inproj_kernel, n_x=len(xs), n_tiles=n_tiles, d_ret=d_ret, d_shift=d_shift)
    widths = (3 * d_ret, d_shift, d_ret + d_rwkv)
    outs = pl.pallas_call(
        kern,
        grid=(b * n_tiles // g,),
        in_specs=x_specs
        + per_tile((1, 1, SUBLANES, d), lambda s, k: (batch(s, k), jnp.minimum(tile(s, k), 1), 0, 0))
        + [pl.BlockSpec((1, d), lambda s: (0, 0)),
           pl.BlockSpec(w.shape, lambda s: (0, 0), pipeline_mode=pl.Buffered(1))]
        + per_tile((ROW_TILE, d_ret), lambda s, k: (tile(s, k), 0))
        + per_tile((ROW_TILE, d_ret), lambda s, k: (tile(s, k), 0)),
        out_specs=[pl.BlockSpec((g * ROW_TILE, n), lambda s: (s, 0)) for n in widths],
        out_shape=[jax.ShapeDtypeStruct((b * t, n), f32) for n in widths],
        compiler_params=pltpu.CompilerParams(vmem_limit_bytes=VMEM_LIMIT),
        name="in_proj",
    )(*[flat[i] for _ in range(g) for i in range(len(xs))], *([modv] * g), nw, w, *([cos_t] * g), *([sin_t] * g))
    return [o.reshape(b, t, n) for o, n in zip(outs, widths)]


def _scan_chunk(idx, forward, n_ctx_chunks, n_chunks):
    if forward:
        return idx
    return jnp.where(idx < n_ctx_chunks, n_ctx_chunks - 1 - idx, n_chunks + n_ctx_chunks - 1 - idx)


def _ret_kernel(lg_ref, q_ref, k_ref, v_ref, gate_ref, nw_ref, o_ref,
                acc_ref, s_ref, mask_ref, qd_ref, kd_ref, cd_ref, *, n_ctx_chunks, n_chunks):
    hh = pl.program_id(1)
    c = RET_C
    masks = _head_masks(PAIR)
    first = masks[0]
    ii = lax.broadcasted_iota(jnp.int32, (c, 2 * c), 0)
    nn = lax.broadcasted_iota(jnp.int32, (c, 2 * c), 1)
    ri = lax.broadcasted_iota(jnp.int32, (c, PAIR), 0).astype(f32)
    for d in range(2):
        rel = ((ii - nn % c) if d == 0 else (nn % c - ii)).astype(f32)
        for j in range(NP):
            lg0 = lg_ref[d, hh * 2 * NP + 2 * j]
            lg1 = lg_ref[d, hh * 2 * NP + 2 * j + 1]
            lgm = jnp.where(nn < c, lg0, lg1)
            lgv = jnp.where(first, lg0, lg1)
            mask_ref[d, j] = jnp.where(rel >= 0.0, jnp.exp(lgm * jnp.maximum(rel, 0.0)), 0.0)
            qpow = (ri + 1.0) if d == 0 else (c - ri)
            kpow = (c - 1.0 - ri) if d == 0 else ri
            qd_ref[d, j] = jnp.exp(lgv * qpow)
            kd_ref[d, j] = jnp.exp(lgv * kpow)
            cd_ref[d, j] = jnp.exp(jnp.broadcast_to(lgv * c, (HEAD, PAIR)))
    s_ref[...] = jnp.zeros(s_ref.shape, f32)
    acc_ref[...] = jnp.zeros(acc_ref.shape, f32)
    chains = [(d, j) for d in range(2) for j in range(NP)]

    def step(s, carry):
        rows, qp, kp, vp = {}, {}, {}, {}
        for d in range(2):
            ch = _scan_chunk(s, d == 0, n_ctx_chunks, n_chunks)
            r0 = pl.multiple_of(ch * c, c)
            rows[d] = pl.ds(r0, c)
            q = q_ref[0, rows[d], :]
            k = k_ref[0, rows[d], :]
            v = v_ref[0, rows[d], :]
            for j in range(NP):
                col = slice(j * PAIR, (j + 1) * PAIR)
                qp[d, j], kp[d, j], vp[d, j] = q[:, col], k[:, col], v[:, col]
        sc, st, vb = {}, {}, {}
        for ck in chains:
            sc[ck] = _mm(qp[ck], _bd(kp[ck], masks), NT, P_RET) * mask_ref[ck[0], ck[1]]
            st[ck] = s_ref[ck[0], ck[1]]
            vb[ck] = _bd(vp[ck], masks)
        for (d, j) in chains:
            ck = (d, j)
            o = _mm(sc[ck], vb[ck], NN, P_RET) + _mm(qp[ck] * qd_ref[d, j], _bd(st[ck], masks), NN, P_RET)
            s_ref[d, j] = st[ck] * cd_ref[d, j] + _mm(_heads_to_rows(kp[ck] * kd_ref[d, j]), vb[ck], TN, P_RET)
            acc_ref[rows[d], j * PAIR:(j + 1) * PAIR] += o
        return carry

    lax.fori_loop(0, n_chunks, step, 0)

    gmat = _group_mat()
    nw = nw_ref[...]

    n_tiles = (n_chunks * c) // ROW_TILE
    tiles_per_step = 3 if n_tiles % 3 == 0 else 1

    def norm_tiles(i, carry):
        sls = [pl.ds(pl.multiple_of((i * tiles_per_step + u) * ROW_TILE, ROW_TILE), ROW_TILE)
               for u in range(tiles_per_step)]
        os_ = [acc_ref[sl, :] for sl in sls]
        mss = [_group_sum(o * o, gmat) * (1.0 / HEAD) for o in os_]
        for sl, o, ms in zip(sls, os_, mss):
            g = gate_ref[0, sl, :]
            o_ref[0, sl, :] = ((o * lax.rsqrt(ms + NORM_EPS) * nw) * (g * jax.nn.sigmoid(g))).astype(bf16)
        return carry

    lax.fori_loop(0, n_tiles // tiles_per_step, norm_tiles, 0)


def _retention(qkv, gates, log_gamma, norm_w, n_ctx):
    b, t, w3 = qkv.shape
    d_ret = w3 // 3
    nhalf = d_ret // RET_W
    kern = functools.partial(_ret_kernel, n_ctx_chunks=n_ctx // RET_C, n_chunks=t // RET_C)
    blk = lambda off: pl.BlockSpec((1, t, RET_W), lambda i, j, off=off: (i, 0, off * nhalf + j))
    return pl.pallas_call(
        kern,
        grid=(b, nhalf),
        in_specs=[pl.BlockSpec(memory_space=pltpu.SMEM),
                  blk(0), blk(1), blk(2),
                  pl.BlockSpec((1, t, RET_W), lambda i, j: (i, 0, j)),
                  pl.BlockSpec((1, RET_W), lambda i, j: (0, j))],
        out_specs=pl.BlockSpec((1, t, RET_W), lambda i, j: (i, 0, j)),
        out_shape=jax.ShapeDtypeStruct((b, t, d_ret), bf16),
        scratch_shapes=[pltpu.VMEM((t, RET_W), f32),
                        pltpu.VMEM((2, NP, HEAD, PAIR), f32),
                        pltpu.VMEM((2, NP, RET_C, 2 * RET_C), f32),
                        pltpu.VMEM((2, NP, RET_C, PAIR), f32),
                        pltpu.VMEM((2, NP, RET_C, PAIR), f32),
                        pltpu.VMEM((2, NP, HEAD, PAIR), f32)],
        compiler_params=pltpu.CompilerParams(vmem_limit_bytes=VMEM_LIMIT),
        name="retention",
    )(log_gamma, qkv, qkv, qkv, gates, norm_w)


class _Pair:
    def __init__(self, x, masks):
        self.x = x
        self.masks = masks
        self._parts = None
        self._lhs = None
        self._bd = None

    def parts(self):
        if self._parts is None:
            hi = self.x.astype(bf16).astype(f32)
            self._parts = (hi, self.x - hi)
        return self._parts

    def lhs(self):
        if self._lhs is None:
            if P_RW == 1:
                self._lhs = self.x.astype(bf16)
            else:
                hi, lo = self.parts()
                self._lhs = jnp.concatenate([hi, hi, lo], axis=1).astype(bf16)
        return self._lhs

    def _bd_parts(self):
        if self._bd is None:
            if P_RW == 1:
                self._bd = (_bd(self.x, self.masks).astype(bf16),)
            else:
                hi, lo = self.parts()
                bh = _bd(hi, self.masks).astype(bf16)
                self._bd = (bh, _bd(lo, self.masks).astype(bf16), bh)
        return self._bd

    def rhs(self):
        return jnp.concatenate(self._bd_parts(), axis=0)

    def rhs_nt(self):
        return jnp.concatenate(self._bd_parts(), axis=1)

    def lhs_rows(self):
        if P_RW == 1:
            return _heads_to_rows(self.x).astype(bf16)
        hi, lo = self.parts()
        th = _heads_to_rows(hi)
        return jnp.concatenate([th, th, _heads_to_rows(lo)], axis=0).astype(bf16)


def _lhs_rows(x):
    if P_RW == 1:
        return x.astype(bf16)
    hi = x.astype(bf16).astype(f32)
    return jnp.concatenate([hi, hi, x - hi], axis=0).astype(bf16)


def _mmp(a, b):
    return _dg(a.lhs(), b.rhs(), NN)


def _rwkv_kernel(r_ref, k_ref, v_ref, xwa_ref, swr_ref, swk_ref, swv_ref, swx_ref,
                 w0_ref, w2_ref, a0_ref, a2_ref, kk_ref, ka_ref, rk_ref, lnw_ref, lnb_ref, gate_ref,
                 o_ref, acc_ref, bon_ref,
                 g_ref, h_ref, q_ref, y_ref, pc_ref,
                 *, n_ctx_chunks, n_chunks):
    c = RW_C
    t_rows = n_chunks * c
    gmat = _group_mat()
    rows = lax.broadcasted_iota(jnp.int32, (c, 1), 0)

    def conv(ref, sw_ref, ch, r0):
        x = ref[0, pl.ds(r0, c), :]
        p0 = pl.multiple_of(jnp.maximum(r0 - SUBLANES, 0), SUBLANES)
        n0 = pl.multiple_of(jnp.minimum(r0 + c, t_rows - SUBLANES), SUBLANES)
        prev8 = ref[0, pl.ds(p0, SUBLANES), :]
        next8 = ref[0, pl.ds(n0, SUBLANES), :]
        has_prev = jnp.logical_and(ch != 0, ch != n_ctx_chunks).astype(f32)
        has_next = jnp.logical_and(ch != n_ctx_chunks - 1, ch != n_chunks - 1).astype(f32)
        xd = jnp.where(rows == 0, prev8[SUBLANES - 1:SUBLANES, :] * has_prev, pltpu.roll(x, 1, 0))
        xu = jnp.where(rows == c - 1, next8[0:1, :] * has_next, pltpu.roll(x, c - 1, 0))
        sw = sw_ref[...]
        return sw[0:1] * xd + sw[1:2] * x + sw[2:3] * xu

    masks = _head_masks(RW_W)
    ii = lax.broadcasted_iota(jnp.int32, (c, RW_W), 0)
    jj = lax.broadcasted_iota(jnp.int32, (c, RW_W), 1) % HEAD
    eye = jnp.where(ii == jj, 1.0, 0.0)
    same = lambda n: (ii // n) == (jj // n)
    diag_blk = same(INV_BLOCK)
    merges = []
    n = INV_BLOCK
    while n < c:
        merges.append(jnp.logical_and(same(2 * n), jnp.logical_not(same(n))))
        n *= 2
    strict = (jj < ii, jj > ii)
    incl = (jj <= ii, jj >= ii)
    both = tuple(jnp.concatenate([strict[d], incl[d]], axis=0) for d in range(2))
    lora = xwa_ref.shape[2] // 2
    wscale = -math.exp(-0.5)
    mk = lambda x: _Pair(x, masks)
    col = lambda j: slice(j * RW_W, (j + 1) * RW_W)

    n_groups = n_chunks // RW_UNROLL

    def features(grp):
        out = []
        rs, ks, vs, xs = [], [], [], []
        for u in range(RW_UNROLL):
            ch = grp * RW_UNROLL + u
            r0 = pl.multiple_of(ch * c, c)
            rs.append(conv(r_ref, swr_ref, ch, r0))
            ks.append(conv(k_ref, swk_ref, ch, r0))
            vs.append(conv(v_ref, swv_ref, ch, r0))
            xs.append(conv(xwa_ref, swx_ref, ch, r0))
        kks = [k * kk_ref[...] for k in ks]
        sums = _group_sum(jnp.concatenate([kk * kk for kk in kks]
                                          + [r * k * rk_ref[...] for r, k in zip(rs, ks)], axis=0), gmat)
        xall = jnp.concatenate(xs, axis=0)
        txw = jnp.tanh(xall[:, :lora])
        kaps = []
        for u in range(RW_UNROLL):
            ch = grp * RW_UNROLL + u
            kaps.append(kks[u] / jnp.maximum(jnp.sqrt(sums[u * c:(u + 1) * c]), 1e-12))
            bon_ref[pl.ds(pl.multiple_of(ch * c, c), c), :] = sums[(RW_UNROLL + u) * c:(RW_UNROLL + u + 1) * c] * vs[u]
            out.append(vs[u])
        for d in range(2):
            lw_all = wscale * jax.nn.sigmoid(w0_ref[d:d + 1, :] + _mm(txw, w2_ref[d], NN, P_RW))
            a_all = jax.nn.sigmoid(a0_ref[d:d + 1, :] + _mm(xall[:, lora:], a2_ref[d], NN, P_RW))
            for u in range(RW_UNROLL):
                ch = grp * RW_UNROLL + u
                lw = lw_all[u * c:(u + 1) * c]
                a = a_all[u * c:(u + 1) * c]
                kmod = ks[u] * (1.0 + (a - 1.0) * ka_ref[...])
                b = kaps[u] * a
                lp = _cumsum_rows(lw, reverse=(d == 1))
                lpc = lp[c - 1:c, :] if d == 0 else lp[0:1, :]
                pinv = jnp.exp(-lp)
                ptail = jnp.exp(lpc - lp)
                pc_ref[d, pl.ds(pl.multiple_of(ch * SUBLANES, SUBLANES), SUBLANES), :] = jnp.broadcast_to(
                    jnp.exp(lpc), (SUBLANES, lp.shape[1]))
                out += [jnp.concatenate([kaps[u] * jnp.exp(lp - lw), rs[u] * jnp.exp(lp)], axis=0),
                        kmod * pinv, b * pinv, kmod * ptail, b * ptail]
        return out

    def prepare(it, carry):
        cur = features(it)
        feat = {}
        for u in range(RW_UNROLL):
            sl = pl.ds(pl.multiple_of((it * RW_UNROLL + u) * c, c), c)
            for d in range(2):
                kr, kd, bd, kh, bh = cur[RW_UNROLL + 5 * (d * RW_UNROLL + u):RW_UNROLL + 5 * (d * RW_UNROLL + u + 1)]
                feat[u, d] = dict(sl=sl, v=cur[u], kr=kr, rq=kr[c:], kd=kd, bd=bd, kh=kh, bh=bh)

        chains = [(u, d, j) for u in range(RW_UNROLL) for d in range(2) for j in range(RW_NP)]
        mkv, m_b, n_b, vp = {}, {}, {}, {}
        for ck in chains:
            u, d, j = ck
            f = feat[u, d]
            kr = mk(f["kr"][:, col(j)])
            mk_ = _dg(kr.lhs(), mk(f["kd"][:, col(j)]).rhs_nt(), NT)
            mb_ = _dg(kr.lhs(), mk(f["bd"][:, col(j)]).rhs_nt(), NT)
            vp[ck] = mk(f["v"][:, col(j)])
            mkv[ck] = _mmp(mk(jnp.where(both[d], mk_, 0.0)), vp[ck])
            m_b[ck] = jnp.where(strict[d], mb_[:c], 0.0)
            n_b[ck] = mk(jnp.where(incl[d], mb_[c:], 0.0))
        p, npow = {}, {}
        for ck in chains:
            nd = mk(jnp.where(diag_blk, -m_b[ck], 0.0))
            p[ck] = eye + nd.x
            npow[ck] = mk(_mmp(nd, nd))
        for level in range(int(math.log2(INV_BLOCK)) - 2):
            for ck in chains:
                both_ = _mmp(mk(jnp.concatenate([p[ck], npow[ck].x], axis=0)), npow[ck])
                p[ck] = p[ck] + both_[:c]
                npow[ck] = mk(both_[c:])
        for ck in chains:
            p[ck] = p[ck] + _mmp(mk(p[ck]), npow[ck])
        for off in merges:
            tmp, pm = {}, {}
            for ck in chains:
                pm[ck] = mk(p[ck])
                tmp[ck] = mk(_mmp(pm[ck], mk(jnp.where(off, m_b[ck], 0.0))))
            for ck in chains:
                p[ck] = p[ck] - _mmp(tmp[ck], pm[ck])
        ta = {}
        for ck in chains:
            u, d, j = ck
            kq = feat[u, d]["kr"][:c, col(j)]
            ta[ck] = _dg(mk(p[ck]).lhs(), jnp.concatenate([mk(kq).rhs(), mk(mkv[ck][:c]).rhs()], axis=1), NN)
        for ck in chains:
            u, d, j = ck
            f = feat[u, d]
            a1, z0 = ta[ck][:, :RW_W], ta[ck][:, RW_W:]
            nb = _dg(n_b[ck].lhs(), jnp.concatenate([mk(a1).rhs(), mk(z0).rhs()], axis=1), NN)
            az = jnp.concatenate([_heads_to_rows(a1), _heads_to_rows(z0)], axis=1)
            bh = mk(f["bh"][:, col(j)])
            ab = _dg(_lhs_rows(az), bh.rhs(), TN)
            vk = _dg(vp[ck].lhs_rows(), mk(f["kh"][:, col(j)]).rhs(), TN)
            g_ref[d, f["sl"], col(j)] = -ab[:HEAD]
            h_ref[d, f["sl"], col(j)] = vk - ab[HEAD:]
            q_ref[d, f["sl"], col(j)] = f["rq"][:, col(j)] - nb[:, :RW_W]
            y_ref[d, f["sl"], col(j)] = mkv[ck][c:] - nb[:, RW_W:]
        return carry

    lax.fori_loop(0, n_groups, prepare, 0)

    acc_ref[...] = jnp.zeros(acc_ref.shape, f32)

    def scan(it, st):
        st = list(st)
        for u in range(SCAN_UNROLL):
            for d in range(2):
                ch = _scan_chunk(it * SCAN_UNROLL + u, d == 0, n_ctx_chunks, n_chunks)
                sl = pl.ds(pl.multiple_of(ch * c, c), c)
                pc = pc_ref[d, pl.ds(pl.multiple_of(ch * SUBLANES, SUBLANES), SUBLANES), :][0:1]
                g, h, q, y0 = g_ref[d, sl, :], h_ref[d, sl, :], q_ref[d, sl, :], y_ref[d, sl, :]
                for j in range(RW_NP):
                    s0 = st[d * RW_NP + j]
                    sp = mk(s0)
                    y = _dg(mk(q[:, col(j)]).lhs(), sp.rhs_nt(), NT) + y0[:, col(j)]
                    st[d * RW_NP + j] = (s0 * pc[:, col(j)] + _mmp(sp, mk(g[:, col(j)]))) + h[:, col(j)]
                    acc_ref[sl, col(j)] += y
        return tuple(st)

    lax.fori_loop(0, n_chunks // SCAN_UNROLL, scan,
                  tuple(jnp.zeros((HEAD, RW_W), f32) for _ in range(2 * RW_NP)))

    lnw = lnw_ref[...]
    lnb = lnb_ref[...]

    n_tiles = t_rows // ROW_TILE
    tiles_per_step = 3 if n_tiles % 3 == 0 else 1

    def readout(i, carry):
        sls = [pl.ds(pl.multiple_of((i * tiles_per_step + u) * ROW_TILE, ROW_TILE), ROW_TILE)
               for u in range(tiles_per_step)]
        ys = [acc_ref[sl, :] for sl in sls]
        ycs = [y - _group_sum(y, gmat) * (1.0 / HEAD) for y in ys]
        vrs = [_group_sum(yc * yc, gmat) * (1.0 / HEAD) for yc in ycs]
        for sl, yc, var in zip(sls, ycs, vrs):
            g = gate_ref[0, sl, :]
            o_ref[0, sl, :] = ((yc * lax.rsqrt(var + GN_EPS) * lnw + lnb + bon_ref[sl, :])
                               * (g * jax.nn.sigmoid(g))).astype(bf16)
        return carry

    lax.fori_loop(0, n_tiles // tiles_per_step, readout, 0)


def _rwkv(rw, gates, shift_w, w0, w2, a0, a2, k_k, k_a, r_k, ln_w, ln_b, n_ctx):
    b, t, d_shift = rw.shape
    d_rwkv = w0.shape[-1]
    lora2 = d_shift - 3 * d_rwkv
    nhalf = d_rwkv // HALF
    xcol = (3 * d_rwkv) // lora2
    assert (t // RW_C) % RW_UNROLL == 0 and (t // RW_C) % SCAN_UNROLL == 0
    kern = functools.partial(_rwkv_kernel, n_ctx_chunks=n_ctx // RW_C, n_chunks=t // RW_C)
    seq = lambda off: pl.BlockSpec((1, t, HALF), lambda i, j, off=off: (i, 0, off * nhalf + j))
    swb = lambda off: pl.BlockSpec((3, HALF), lambda i, j, off=off: (0, off * nhalf + j))
    vec = lambda rows: pl.BlockSpec((rows, HALF), lambda i, j: (0, j))
    lor = pl.BlockSpec((2, lora2 // 2, HALF), lambda i, j: (0, 0, j))
    return pl.pallas_call(
        kern,
        grid=(b, nhalf),
        in_specs=[seq(0), seq(1), seq(2),
                  pl.BlockSpec((1, t, lora2), lambda i, j: (i, 0, xcol)),
                  swb(0), swb(1), swb(2),
                  pl.BlockSpec((3, lora2), lambda i, j: (0, xcol)),
                  vec(2), lor, vec(2), lor, vec(1), vec(1), vec(1), vec(1), vec(1),
                  pl.BlockSpec((1, t, HALF), lambda i, j: (i, 0, gates.shape[-1] // HALF - nhalf + j))],
        out_specs=pl.BlockSpec((1, t, HALF), lambda i, j: (i, 0, j)),
        out_shape=jax.ShapeDtypeStruct((b, t, d_rwkv), bf16),
        scratch_shapes=[pltpu.VMEM((t, HALF), f32)] * 2 + [pltpu.VMEM((2, t, HALF), f32)] * 4
        + [pltpu.VMEM((2, (t // RW_C) * SUBLANES, HALF), f32)],
        compiler_params=pltpu.CompilerParams(vmem_limit_bytes=VMEM_LIMIT),
        name="rwkv7",
    )(rw, rw, rw, rw, shift_w, shift_w, shift_w, shift_w,
      w0, w2, a0, a2, k_k, k_a, r_k, ln_w, ln_b, gates)


def _outproj_kernel(*refs, n_x, n_out, skip, final):
    g = OUT_TILES
    ret_refs, rwo_refs, refs = refs[:g], refs[g:2 * g], refs[2 * g:]
    xs_refs, refs = refs[:g * n_x], refs[g * n_x:]
    mod_refs, (w_ref, fw_ref, o_ref) = refs[:g], refs[g:]
    act = jnp.concatenate([jnp.concatenate([ret_refs[t][...], rwo_refs[t][...]], axis=1) for t in range(g)], axis=0)
    mix = _dg(act, w_ref[...], NN)
    xs, gates = [], []
    for t in range(g):
        if n_x == 1:
            xs.append(xs_refs[t][...])
        else:
            tile = lax.rem(pl.program_id(0) * g + t, n_out) + skip
            xs.append(jnp.where(tile == 0, xs_refs[2 * t][...], xs_refs[2 * t + 1][...]))
        gates.append(jnp.broadcast_to(mod_refs[t][0, 0][2:3], xs[-1].shape))
    xn = jnp.concatenate(xs, axis=0) + jnp.concatenate(gates, axis=0) * mix
    if final:
        ms = jnp.mean(xn * xn, axis=-1, keepdims=True)
        xn = xn * lax.rsqrt(ms + NORM_EPS) * fw_ref[...]
    o_ref[...] = xn


def _outproj(ret, rwo, xs, modv, w, fw, n_ctx, final):
    b, t, d_ret = ret.shape
    d = xs[0].shape[-1]
    d_rw = rwo.shape[-1]
    skip = (n_ctx // ROW_TILE) if final else 0
    n_tiles = t // ROW_TILE
    n_out = n_tiles - skip
    g = OUT_TILES
    assert (b * n_out) % g == 0
    batch = lambda s, k: lax.div(s * g + k, n_out)
    tile = lambda s, k: lax.rem(s * g + k, n_out) + skip
    row_of = lambda s, k: batch(s, k) * n_tiles + tile(s, k)
    per_tile = lambda shape, imap: [pl.BlockSpec(shape, functools.partial(imap, k=k)) for k in range(g)]
    x_specs = []
    for k in range(g):
        if len(xs) == 1:
            x_specs.append(pl.BlockSpec((ROW_TILE, d), functools.partial(lambda s, k: (row_of(s, k), 0), k=k)))
        else:
            assert skip == 0 and xs[0].shape[1] == ROW_TILE
            x_specs += [pl.BlockSpec((ROW_TILE, d), lambda s, k=k: (batch(s, k), 0)),
                        pl.BlockSpec((ROW_TILE, d), lambda s, k=k: (
                            batch(s, k) * (n_tiles - 1) + jnp.maximum(tile(s, k) - 1, 0), 0))]
    flat = [a.reshape(-1, d) for a in xs]
    kern = functools.partial(_outproj_kernel, n_x=len(xs), n_out=n_out, skip=skip, final=final)
    out = pl.pallas_call(
        kern,
        grid=(b * n_out // g,),
        in_specs=per_tile((ROW_TILE, d_ret), lambda s, k: (row_of(s, k), 0))
        + per_tile((ROW_TILE, d_rw), lambda s, k: (row_of(s, k), 0))
        + x_specs
        + per_tile((1, 1, SUBLANES, d), lambda s, k: (batch(s, k), jnp.minimum(tile(s, k), 1), 0, 0))
        + [pl.BlockSpec(w.shape, lambda s: (0, 0), pipeline_mode=pl.Buffered(1)),
           pl.BlockSpec((1, d), lambda s: (0, 0))],
        out_specs=pl.BlockSpec((g * ROW_TILE, d), lambda s: (s, 0)),
        out_shape=jax.ShapeDtypeStruct((b * n_out * ROW_TILE, d), f32),
        compiler_params=pltpu.CompilerParams(vmem_limit_bytes=VMEM_LIMIT),
        name="out_proj",
    )(*([ret.reshape(-1, d_ret)] * g), *([rwo.reshape(-1, d_rw)] * g),
      *[flat[i] for _ in range(g) for i in range(len(xs))], *([modv] * g), w, fw)
    return out.reshape(b, n_out * ROW_TILE, d)


def _rope_tables(n_ctx, seq, d_ret):
    nf = ROPE_NF
    inv = ROPE_BASE ** (-jnp.arange(nf, dtype=f32) / nf)
    pos = jnp.arange(seq)
    row_pos = (pos // GRID_W).astype(f32)
    col_pos = (pos % GRID_W).astype(f32)
    ang_r = row_pos[:, None] * inv[None, :]
    ang_c = col_pos[:, None] * inv[None, :]
    cos_h = jnp.concatenate([jnp.cos(ang_r)] * 2 + [jnp.cos(ang_c)] * 2, axis=-1)
    sin_h = jnp.concatenate([-jnp.sin(ang_r), jnp.sin(ang_r), -jnp.sin(ang_c), jnp.sin(ang_c)], axis=-1)
    reps = d_ret // HEAD
    cos_l = jnp.tile(cos_h, (1, reps))
    sin_l = jnp.tile(sin_h, (1, reps))
    cos_t = jnp.concatenate([jnp.ones((n_ctx, d_ret), f32), cos_l], axis=0)
    sin_t = jnp.concatenate([jnp.zeros((n_ctx, d_ret), f32), sin_l], axis=0)
    return cos_t, sin_t


def kernel(x, c, ctx, c_ctx, norm_w, w_mod, b_mod, w_in, ret_log_gamma, ret_norm_w, rwkv_shift_w, rwkv_w0, rwkv_w2, rwkv_a0, rwkv_a2, rwkv_k_k, rwkv_k_a, rwkv_r_k, rwkv_ln_w, rwkv_ln_b, w_out, final_norm_w):
    b, seq, d = x.shape
    n_ctx = ctx.shape[1]
    depth = w_in.shape[0]
    d_ret = ret_norm_w.shape[-1]
    d_rwkv = rwkv_w0.shape[-1]
    d_shift = rwkv_shift_w.shape[-1]

    assert b < COND_ROWS
    cond = jnp.zeros((COND_ROWS, d), f32).at[:b].set(c.astype(f32)).at[b].set(c_ctx.astype(f32))
    mods = _modulation(cond, w_mod, b_mod)
    cos_t, sin_t = _rope_tables(n_ctx, seq, d_ret)
    if depth > 1 and n_ctx == ROW_TILE:
        xs = (ctx.astype(f32), x.astype(f32))
    else:
        xs = (jnp.concatenate([ctx.astype(f32), x.astype(f32)], axis=1),)

    out = None
    for layer in range(depth):
        m = mods[layer].reshape(COND_ROWS, 3, d)
        m = jnp.stack([m[:, 1], m[:, 0], m[:, 2]], axis=1)
        lat = m[:b]
        cx = jnp.broadcast_to(m[b][None], (b, 3, d))
        modv = jnp.pad(jnp.stack([cx, lat], axis=1), ((0, 0), (0, 0), (0, SUBLANES - 3), (0, 0)))

        qkv, rw, gates = _inproj(xs, modv, norm_w[layer][None], w_in[layer].astype(bf16), cos_t, sin_t,
                                 d_ret, d_shift, d_rwkv)
        ret = _retention(qkv, gates, ret_log_gamma[layer], ret_norm_w[layer][None], n_ctx)
        rwo = _rwkv(rw, gates, rwkv_shift_w[layer], rwkv_w0[layer], rwkv_w2[layer], rwkv_a0[layer],
                    rwkv_a2[layer], rwkv_k_k[layer][None], rwkv_k_a[layer][None],
                    rwkv_r_k[layer].reshape(1, d_rwkv), rwkv_ln_w[layer][None],
                    rwkv_ln_b[layer][None], n_ctx)
        final = layer == depth - 1
        res = _outproj(ret, rwo, xs, modv, w_out[layer].astype(bf16), final_norm_w[None], n_ctx, final)
        if final:
            out = res
        else:
            xs = (res,)
    return out
```

```python
import functools
import math

import jax
import jax.numpy as jnp
from jax import lax
from jax.experimental import pallas as pl
from jax.experimental.pallas import tpu as pltpu

f32 = jnp.float32
bf16 = jnp.bfloat16

HEAD = 64
PAIR = 2 * HEAD
RET_C = 128
RW_C = 64
INV_BLOCK = 32
RW_UNROLL = 6
SCAN_UNROLL = 12
GRID_W = 64
ROPE_BASE = 10000.0
NORM_EPS = 1e-6
GN_EPS = 64e-5
ROW_TILE = 256
IN_TILES = 2
OUT_TILES = 4
HALF = 256
RET_W = 512
NP = RET_W // PAIR
RW_W = 128
RW_NP = HALF // RW_W
VMEM_LIMIT = 56 * 1024 * 1024
SUBLANES = 8
COND_ROWS = 16
MOD_TILE = 1024
ROPE_NF = HEAD // 4

NN = ((1,), (0,))
NT = ((1,), (1,))
TN = ((0,), (0,))

P_MOD = 3
P_RET = 1
P_RW = 1
P_SUM = 1


def _dg(a, b, dims):
    return lax.dot_general(a, b, (dims, ((), ())), preferred_element_type=f32)


def _split2(a):
    hi = a.astype(bf16)
    lo = (a - hi.astype(f32)).astype(bf16)
    return hi, lo


def _mm(a, b, dims=NN, passes=3):
    if passes == 1:
        return _dg(a.astype(bf16), b.astype(bf16), dims)
    ah, al = _split2(a)
    bh, bl = _split2(b)
    return _dg(ah, bh, dims) + (_dg(ah, bl, dims) + _dg(al, bh, dims))


def _cumsum_rows(x, reverse):
    n = x.shape[0]
    row = lax.broadcasted_iota(jnp.int32, (n, 1), 0)
    s = 1
    while s < n:
        if reverse:
            x = x + jnp.where(row < n - s, pltpu.roll(x, n - s, 0), 0.0)
        else:
            x = x + jnp.where(row >= s, pltpu.roll(x, s, 0), 0.0)
        s *= 2
    return x


def _group_sum(x, gmat, pieces=P_SUM):
    rows, width = x.shape
    gw = gmat.shape[0]
    cols = width // gw
    xs = jnp.concatenate([x[:, c * gw:(c + 1) * gw] for c in range(cols)], axis=0)
    if pieces == 1:
        s = _dg(xs.astype(bf16), gmat, NN)
    else:
        h, l = _split2(xs)
        s = _dg(h, gmat, NN) + _dg(l, gmat, NN)
    return jnp.concatenate([s[c * rows:(c + 1) * rows] for c in range(cols)], axis=1)


def _group_mat():
    ii = lax.broadcasted_iota(jnp.int32, (PAIR, PAIR), 0) // HEAD
    jj = lax.broadcasted_iota(jnp.int32, (PAIR, PAIR), 1) // HEAD
    return jnp.where(ii == jj, 1.0, 0.0).astype(bf16)


def _head_masks(width):
    blk = lax.broadcasted_iota(jnp.int32, (1, width), 1) // HEAD
    return [blk == h for h in range(width // HEAD)]


def _bd(x, masks):
    return jnp.concatenate([jnp.where(m, x, 0.0) for m in masks], axis=0)


def _heads_to_rows(x):
    w = x.shape[1]
    return jnp.concatenate([x[:, :HEAD]] + [pltpu.roll(x, w - h * HEAD, 1)[:, :HEAD]
                                            for h in range(1, w // HEAD)], axis=0)


def _mod_kernel(c_ref, w_ref, b_ref, o_ref):
    cnd = c_ref[...]
    s = cnd * jax.nn.sigmoid(cnd)
    o_ref[0] = _mm(s, w_ref[0], NN, P_MOD) + b_ref[0]


def _modulation(cond, w_mod, b_mod):
    depth, d, d3 = w_mod.shape
    rows = cond.shape[0]
    tn = MOD_TILE
    return pl.pallas_call(
        _mod_kernel,
        grid=(depth, d3 // tn),
        in_specs=[pl.BlockSpec((rows, d), lambda l, j: (0, 0)),
                  pl.BlockSpec((1, d, tn), lambda l, j: (l, 0, j)),
                  pl.BlockSpec((1, 1, tn), lambda l, j: (l, 0, j))],
        out_specs=pl.BlockSpec((1, rows, tn), lambda l, j: (l, 0, j)),
        out_shape=jax.ShapeDtypeStruct((depth, rows, d3), f32),
        compiler_params=pltpu.CompilerParams(vmem_limit_bytes=VMEM_LIMIT),
        name="adaln_mod",
    )(cond, w_mod, b_mod.reshape(depth, 1, d3))


def _inproj_kernel(*refs, n_x, n_tiles, d_ret, d_shift):
    g = IN_TILES
    xs_refs, refs = refs[:g * n_x], refs[g * n_x:]
    mod_refs, (nw_ref, w_ref), refs = refs[:g], refs[g:g + 2], refs[g + 2:]
    cos_refs, sin_refs, (qkv_ref, rw_ref, g_ref) = refs[:g], refs[g:2 * g], refs[2 * g:]
    hs = []
    for t in range(g):
        if n_x == 1:
            x = xs_refs[t][...]
        else:
            tile = lax.rem(pl.program_id(0) * g + t, n_tiles)
            x = jnp.where(tile == 0, xs_refs[2 * t][...], xs_refs[2 * t + 1][...])
        ms = jnp.mean(x * x, axis=-1, keepdims=True)
        m = mod_refs[t][0, 0]
        h = x * lax.rsqrt(ms + NORM_EPS) * nw_ref[...]
        hs.append((h * (1.0 + m[0:1]) + m[1:2]).astype(bf16))
    def proj(lo, hi):
        return jnp.concatenate([_dg(h, w_ref[:, lo:hi], NN) for h in hs], axis=0)

    cos = jnp.concatenate([r[...] for r in cos_refs], axis=0)
    sin = jnp.concatenate([r[...] for r in sin_refs], axis=0)
    lane = lax.broadcasted_iota(jnp.int32, cos.shape, 1)
    first = (lane % (2 * ROPE_NF)) < ROPE_NF

    def rope(t):
        nl = t.shape[-1]
        swapped = jnp.where(first, pltpu.roll(t, nl - ROPE_NF, 1), pltpu.roll(t, ROPE_NF, 1))
        return t * cos + swapped * sin

    qkv_ref[:, 0:d_ret] = rope(proj(0, d_ret))
    qkv_ref[:, d_ret:2 * d_ret] = rope(proj(d_ret, 2 * d_ret)) * (HEAD ** -0.5)
    qkv_ref[:, 2 * d_ret:3 * d_ret] = proj(2 * d_ret, 3 * d_ret)
    g_ref[:, 0:d_ret] = proj(3 * d_ret, 4 * d_ret)
    o = 4 * d_ret
    rw_ref[...] = proj(o, o + d_shift)
    g_ref[:, d_ret:] = proj(o + d_shift, w_ref.shape[1])


def _inproj(xs, modv, nw, w, cos_t, sin_t, d_ret, d_shift, d_rwkv):
    b, d = xs[0].shape[0], xs[0].shape[-1]
    t = sum(a.shape[1] for a in xs)
    n_tiles = t // ROW_TILE
    g = IN_TILES
    assert (b * n_tiles) % g == 0
    batch = lambda s, k: lax.div(s * g + k, n_tiles)
    tile = lambda s, k: lax.rem(s * g + k, n_tiles)
    flat = [a.reshape(-1, d) for a in xs]
    x_specs = []
    for k in range(g):
        if len(xs) == 1:
            x_specs.append(pl.BlockSpec((ROW_TILE, d), lambda s, k=k: (s * g + k, 0)))
        else:
            assert xs[0].shape[1] == ROW_TILE
            x_specs += [pl.BlockSpec((ROW_TILE, d), lambda s, k=k: (batch(s, k), 0)),
                        pl.BlockSpec((ROW_TILE, d), lambda s, k=k: (
                            batch(s, k) * (n_tiles - 1) + jnp.maximum(tile(s, k) - 1, 0), 0))]
    per_tile = lambda shape, imap: [pl.BlockSpec(shape, functools.partial(imap, k=k)) for k in range(g)]
    kern = functools.partial(_inproj_kernel, n_x=len(xs), n_tiles=n_tiles, d_ret=d_ret, d_shift=d_shift)
    widths = (3 * d_ret, d_shift, d_ret + d_rwkv)
    outs = pl.pallas_call(
        kern,
        grid=(b * n_tiles // g,),
        in_specs=x_specs
        + per_tile((1, 1, SUBLANES, d), lambda s, k: (batch(s, k), jnp.minimum(tile(s, k), 1), 0, 0))
        + [pl.BlockSpec((1, d), lambda s: (0, 0)),
           pl.BlockSpec(w.shape, lambda s: (0, 0), pipeline_mode=pl.Buffered(1))]
        + per_tile((ROW_TILE, d_ret), lambda s, k: (tile(s, k), 0))
        + per_tile((ROW_TILE, d_ret), lambda s, k: (tile(s, k), 0)),
        out_specs=[pl.BlockSpec((g * ROW_TILE, n), lambda s: (s, 0)) for n in widths],
        out_shape=[jax.ShapeDtypeStruct((b * t, n), f32) for n in widths],
        compiler_params=pltpu.CompilerParams(vmem_limit_bytes=VMEM_LIMIT),
        name="in_proj",
    )(*[flat[i] for _ in range(g) for i in range(len(xs))], *([modv] * g), nw, w, *([cos_t] * g), *([sin_t] * g))
    return [o.reshape(b, t, n) for o, n in zip(outs, widths)]


def _scan_chunk(idx, forward, n_ctx_chunks, n_chunks):
    if forward:
        return idx
    return jnp.where(idx < n_ctx_chunks, n_ctx_chunks - 1 - idx, n_chunks + n_ctx_chunks - 1 - idx)


def _ret_kernel(lg_ref, q_ref, k_ref, v_ref, gate_ref, nw_ref, o_ref,
                acc_ref, s_ref, mask_ref, qd_ref, kd_ref, cd_ref, *, n_ctx_chunks, n_chunks):
    hh = pl.program_id(1)
    c = RET_C
    masks = _head_masks(PAIR)
    first = masks[0]
    ii = lax.broadcasted_iota(jnp.int32, (c, 2 * c), 0)
    nn = lax.broadcasted_iota(jnp.int32, (c, 2 * c), 1)
    ri = lax.broadcasted_iota(jnp.int32, (c, PAIR), 0).astype(f32)
    for d in range(2):
        rel = ((ii - nn % c) if d == 0 else (nn % c - ii)).astype(f32)
        for j in range(NP):
            lg0 = lg_ref[d, hh * 2 * NP + 2 * j]
            lg1 = lg_ref[d, hh * 2 * NP + 2 * j + 1]
            lgm = jnp.where(nn < c, lg0, lg1)
            lgv = jnp.where(first, lg0, lg1)
            mask_ref[d, j] = jnp.where(rel >= 0.0, jnp.exp(lgm * jnp.maximum(rel, 0.0)), 0.0)
            qpow = (ri + 1.0) if d == 0 else (c - ri)
            kpow = (c - 1.0 - ri) if d == 0 else ri
            qd_ref[d, j] = jnp.exp(lgv * qpow)
            kd_ref[d, j] = jnp.exp(lgv * kpow)
            cd_ref[d, j] = jnp.exp(jnp.broadcast_to(lgv * c, (HEAD, PAIR)))
    s_ref[...] = jnp.zeros(s_ref.shape, f32)
    acc_ref[...] = jnp.zeros(acc_ref.shape, f32)
    chains = [(d, j) for d in range(2) for j in range(NP)]

    def step(s, carry):
        rows, qp, kp, vp = {}, {}, {}, {}
        for d in range(2):
            ch = _scan_chunk(s, d == 0, n_ctx_chunks, n_chunks)
            r0 = pl.multiple_of(ch * c, c)
            rows[d] = pl.ds(r0, c)
            q = q_ref[0, rows[d], :]
            k = k_ref[0, rows[d], :]
            v = v_ref[0, rows[d], :]
            for j in range(NP):
                col = slice(j * PAIR, (j + 1) * PAIR)
                qp[d, j], kp[d, j], vp[d, j] = q[:, col], k[:, col], v[:, col]
        sc, st, vb = {}, {}, {}
        for ck in chains:
            sc[ck] = _mm(qp[ck], _bd(kp[ck], masks), NT, P_RET) * mask_ref[ck[0], ck[1]]
            st[ck] = s_ref[ck[0], ck[1]]
            vb[ck] = _bd(vp[ck], masks)
        for (d, j) in chains:
            ck = (d, j)
            o = _mm(sc[ck], vb[ck], NN, P_RET) + _mm(qp[ck] * qd_ref[d, j], _bd(st[ck], masks), NN, P_RET)
            s_ref[d, j] = st[ck] * cd_ref[d, j] + _mm(_heads_to_rows(kp[ck] * kd_ref[d, j]), vb[ck], TN, P_RET)
            acc_ref[rows[d], j * PAIR:(j + 1) * PAIR] += o
        return carry

    lax.fori_loop(0, n_chunks, step, 0)

    gmat = _group_mat()
    nw = nw_ref[...]

    n_tiles = (n_chunks * c) // ROW_TILE
    tiles_per_step = 3 if n_tiles % 3 == 0 else 1

    def norm_tiles(i, carry):
        sls = [pl.ds(pl.multiple_of((i * tiles_per_step + u) * ROW_TILE, ROW_TILE), ROW_TILE)
               for u in range(tiles_per_step)]
        os_ = [acc_ref[sl, :] for sl in sls]
        mss = [_group_sum(o * o, gmat) * (1.0 / HEAD) for o in os_]
        for sl, o, ms in zip(sls, os_, mss):
            g = gate_ref[0, sl, :]
            o_ref[0, sl, :] = ((o * lax.rsqrt(ms + NORM_EPS) * nw) * (g * jax.nn.sigmoid(g))).astype(bf16)
        return carry

    lax.fori_loop(0, n_tiles // tiles_per_step, norm_tiles, 0)


def _retention(qkv, gates, log_gamma, norm_w, n_ctx):
    b, t, w3 = qkv.shape
    d_ret = w3 // 3
    nhalf = d_ret // RET_W
    kern = functools.partial(_ret_kernel, n_ctx_chunks=n_ctx // RET_C, n_chunks=t // RET_C)
    blk = lambda off: pl.BlockSpec((1, t, RET_W), lambda i, j, off=off: (i, 0, off * nhalf + j))
    return pl.pallas_call(
        kern,
        grid=(b, nhalf),
        in_specs=[pl.BlockSpec(memory_space=pltpu.SMEM),
                  blk(0), blk(1), blk(2),
                  pl.BlockSpec((1, t, RET_W), lambda i, j: (i, 0, j)),
                  pl.BlockSpec((1, RET_W), lambda i, j: (0, j))],
        out_specs=pl.BlockSpec((1, t, RET_W), lambda i, j: (i, 0, j)),
        out_shape=jax.ShapeDtypeStruct((b, t, d_ret), bf16),
        scratch_shapes=[pltpu.VMEM((t, RET_W), f32),
                        pltpu.VMEM((2, NP, HEAD, PAIR), f32),
                        pltpu.VMEM((2, NP, RET_C, 2 * RET_C), f32),
                        pltpu.VMEM((2, NP, RET_C, PAIR), f32),
                        pltpu.VMEM((2, NP, RET_C, PAIR), f32),
                        pltpu.VMEM((2, NP, HEAD, PAIR), f32)],
        compiler_params=pltpu.CompilerParams(vmem_limit_bytes=VMEM_LIMIT),
        name="retention",
    )(log_gamma, qkv, qkv, qkv, gates, norm_w)


class _Pair:
    def __init__(self, x, masks):
        self.x = x
        self.masks = masks
        self._parts = None
        self._lhs = None
        self._bd = None

    def parts(self):
        if self._parts is None:
            hi = self.x.astype(bf16).astype(f32)
            self._parts = (hi, self.x - hi)
        return self._parts

    def lhs(self):
        if self._lhs is None:
            if P_RW == 1:
                self._lhs = self.x.astype(bf16)
            else:
                hi, lo = self.parts()
                self._lhs = jnp.concatenate([hi, hi, lo], axis=1).astype(bf16)
        return self._lhs

    def _bd_parts(self):
        if self._bd is None:
            if P_RW == 1:
                self._bd = (_bd(self.x, self.masks).astype(bf16),)
            else:
                hi, lo = self.parts()
                bh = _bd(hi, self.masks).astype(bf16)
                self._bd = (bh, _bd(lo, self.masks).astype(bf16), bh)
        return self._bd

    def rhs(self):
        return jnp.concatenate(self._bd_parts(), axis=0)

    def rhs_nt(self):
        return jnp.concatenate(self._bd_parts(), axis=1)

    def lhs_rows(self):
        if P_RW == 1:
            return _heads_to_rows(self.x).astype(bf16)
        hi, lo = self.parts()
        th = _heads_to_rows(hi)
        return jnp.concatenate([th, th, _heads_to_rows(lo)], axis=0).astype(bf16)


def _lhs_rows(x):
    if P_RW == 1:
        return x.astype(bf16)
    hi = x.astype(bf16).astype(f32)
    return jnp.concatenate([hi, hi, x - hi], axis=0).astype(bf16)


def _mmp(a, b):
    return _dg(a.lhs(), b.rhs(), NN)


def _rwkv_kernel(r_ref, k_ref, v_ref, xwa_ref, swr_ref, swk_ref, swv_ref, swx_ref,
                 w0_ref, w2_ref, a0_ref, a2_ref, kk_ref, ka_ref, rk_ref, lnw_ref, lnb_ref, gate_ref,
                 o_ref, acc_ref, bon_ref,
                 g_ref, h_ref, q_ref, y_ref, pc_ref,
                 *, n_ctx_chunks, n_chunks):
    c = RW_C
    t_rows = n_chunks * c
    gmat = _group_mat()
    rows = lax.broadcasted_iota(jnp.int32, (c, 1), 0)

    def conv(ref, sw_ref, ch, r0):
        x = ref[0, pl.ds(r0, c), :]
        p0 = pl.multiple_of(jnp.maximum(r0 - SUBLANES, 0), SUBLANES)
        n0 = pl.multiple_of(jnp.minimum(r0 + c, t_rows - SUBLANES), SUBLANES)
        prev8 = ref[0, pl.ds(p0, SUBLANES), :]
        next8 = ref[0, pl.ds(n0, SUBLANES), :]
        has_prev = jnp.logical_and(ch != 0, ch != n_ctx_chunks).astype(f32)
        has_next = jnp.logical_and(ch != n_ctx_chunks - 1, ch != n_chunks - 1).astype(f32)
        xd = jnp.where(rows == 0, prev8[SUBLANES - 1:SUBLANES, :] * has_prev, pltpu.roll(x, 1, 0))
        xu = jnp.where(rows == c - 1, next8[0:1, :] * has_next, pltpu.roll(x, c - 1, 0))
        sw = sw_ref[...]
        return sw[0:1] * xd + sw[1:2] * x + sw[2:3] * xu

    masks = _head_masks(RW_W)
    ii = lax.broadcasted_iota(jnp.int32, (c, RW_W), 0)
    jj = lax.broadcasted_iota(jnp.int32, (c, RW_W), 1) % HEAD
    eye = jnp.where(ii == jj, 1.0, 0.0)
    same = lambda n: (ii // n) == (jj // n)
    diag_blk = same(INV_BLOCK)
    merges = []
    n = INV_BLOCK
    while n < c:
        merges.append(jnp.logical_and(same(2 * n), jnp.logical_not(same(n))))
        n *= 2
    strict = (jj < ii, jj > ii)
    incl = (jj <= ii, jj >= ii)
    both = tuple(jnp.concatenate([strict[d], incl[d]], axis=0) for d in range(2))
    lora = xwa_ref.shape[2] // 2
    wscale = -math.exp(-0.5)
    mk = lambda x: _Pair(x, masks)
    col = lambda j: slice(j * RW_W, (j + 1) * RW_W)

    n_groups = n_chunks // RW_UNROLL

    def features(grp):
        out = []
        rs, ks, vs, xs = [], [], [], []
        for u in range(RW_UNROLL):
            ch = grp * RW_UNROLL + u
            r0 = pl.multiple_of(ch * c, c)
            rs.append(conv(r_ref, swr_ref, ch, r0))
            ks.append(conv(k_ref, swk_ref, ch, r0))
            vs.append(conv(v_ref, swv_ref, ch, r0))
            xs.append(conv(xwa_ref, swx_ref, ch, r0))
        kks = [k * kk_ref[...] for k in ks]
        sums = _group_sum(jnp.concatenate([kk * kk for kk in kks]
                                          + [r * k * rk_ref[...] for r, k in zip(rs, ks)], axis=0), gmat)
        xall = jnp.concatenate(xs, axis=0)
        txw = jnp.tanh(xall[:, :lora])
        kaps = []
        for u in range(RW_UNROLL):
            ch = grp * RW_UNROLL + u
            kaps.append(kks[u] / jnp.maximum(jnp.sqrt(sums[u * c:(u + 1) * c]), 1e-12))
            bon_ref[pl.ds(pl.multiple_of(ch * c, c), c), :] = sums[(RW_UNROLL + u) * c:(RW_UNROLL + u + 1) * c] * vs[u]
            out.append(vs[u])
        for d in range(2):
            lw_all = wscale * jax.nn.sigmoid(w0_ref[d:d + 1, :] + _mm(txw, w2_ref[d], NN, P_RW))
            a_all = jax.nn.sigmoid(a0_ref[d:d + 1, :] + _mm(xall[:, lora:], a2_ref[d], NN, P_RW))
            for u in range(RW_UNROLL):
                ch = grp * RW_UNROLL + u
                lw = lw_all[u * c:(u + 1) * c]
                a = a_all[u * c:(u + 1) * c]
                kmod = ks[u] * (1.0 + (a - 1.0) * ka_ref[...])
                b = kaps[u] * a
                lp = _cumsum_rows(lw, reverse=(d == 1))
                lpc = lp[c - 1:c, :] if d == 0 else lp[0:1, :]
                pinv = jnp.exp(-lp)
                ptail = jnp.exp(lpc - lp)
                pc_ref[d, pl.ds(pl.multiple_of(ch * SUBLANES, SUBLANES), SUBLANES), :] = jnp.broadcast_to(
                    jnp.exp(lpc), (SUBLANES, lp.shape[1]))
                out += [jnp.concatenate([kaps[u] * jnp.exp(lp - lw), rs[u] * jnp.exp(lp)], axis=0),
                        kmod * pinv, b * pinv, kmod * ptail, b * ptail]
        return out

    def prepare(it, carry):
        cur = features(it)
        feat = {}
        for u in range(RW_UNROLL):
            sl = pl.ds(pl.multiple_of((it * RW_UNROLL + u) * c, c), c)
            for d in range(2):
                kr, kd, bd, kh, bh = cur[RW_UNROLL + 5 * (d * RW_UNROLL + u):RW_UNROLL + 5 * (d * RW_UNROLL + u + 1)]
                feat[u, d] = dict(sl=sl, v=cur[u], kr=kr, rq=kr[c:], kd=kd, bd=bd, kh=kh, bh=bh)

        chains = [(u, d, j) for u in range(RW_UNROLL) for d in range(2) for j in range(RW_NP)]
        mkv, m_b, n_b, vp = {}, {}, {}, {}
        for ck in chains:
            u, d, j = ck
            f = feat[u, d]
            kr = mk(f["kr"][:, col(j)])
            mk_ = _dg(kr.lhs(), mk(f["kd"][:, col(j)]).rhs_nt(), NT)
            mb_ = _dg(kr.lhs(), mk(f["bd"][:, col(j)]).rhs_nt(), NT)
            vp[ck] = mk(f["v"][:, col(j)])
            mkv[ck] = _mmp(mk(jnp.where(both[d], mk_, 0.0)), vp[ck])
            m_b[ck] = jnp.where(strict[d], mb_[:c], 0.0)
            n_b[ck] = mk(jnp.where(incl[d], mb_[c:], 0.0))
        p, npow = {}, {}
        for ck in chains:
            nd = mk(jnp.where(diag_blk, -m_b[ck], 0.0))
            p[ck] = eye + nd.x
            npow[ck] = mk(_mmp(nd, nd))
        for level in range(int(math.log2(INV_BLOCK)) - 2):
            for ck in chains:
                both_ = _mmp(mk(jnp.concatenate([p[ck], npow[ck].x], axis=0)), npow[ck])
                p[ck] = p[ck] + both_[:c]
                npow[ck] = mk(both_[c:])
        for ck in chains:
            p[ck] = p[ck] + _mmp(mk(p[ck]), npow[ck])
        for off in merges:
            tmp, pm = {}, {}
            for ck in chains:
                pm[ck] = mk(p[ck])
                tmp[ck] = mk(_mmp(pm[ck], mk(jnp.where(off, m_b[ck], 0.0))))
            for ck in chains:
                p[ck] = p[ck] - _mmp(tmp[ck], pm[ck])
        ta = {}
        for ck in chains:
            u, d, j = ck
            kq = feat[u, d]["kr"][:c, col(j)]
            ta[ck] = _dg(mk(p[ck]).lhs(), jnp.concatenate([mk(kq).rhs(), mk(mkv[ck][:c]).rhs()], axis=1), NN)
        for ck in chains:
            u, d, j = ck
            f = feat[u, d]
            a1, z0 = ta[ck][:, :RW_W], ta[ck][:, RW_W:]
            nb = _dg(n_b[ck].lhs(), jnp.concatenate([mk(a1).rhs(), mk(z0).rhs()], axis=1), NN)
            az = jnp.concatenate([_heads_to_rows(a1), _heads_to_rows(z0)], axis=1)
            bh = mk(f["bh"][:, col(j)])
            ab = _dg(_lhs_rows(az), bh.rhs(), TN)
            vk = _dg(vp[ck].lhs_rows(), mk(f["kh"][:, col(j)]).rhs(), TN)
            g_ref[d, f["sl"], col(j)] = -ab[:HEAD]
            h_ref[d, f["sl"], col(j)] = vk - ab[HEAD:]
            q_ref[d, f["sl"], col(j)] = f["rq"][:, col(j)] - nb[:, :RW_W]
            y_ref[d, f["sl"], col(j)] = mkv[ck][c:] - nb[:, RW_W:]
        return carry

    lax.fori_loop(0, n_groups, prepare, 0)

    acc_ref[...] = jnp.zeros(acc_ref.shape, f32)

    def scan(it, st):
        st = list(st)
        for u in range(SCAN_UNROLL):
            for d in range(2):
                ch = _scan_chunk(it * SCAN_UNROLL + u, d == 0, n_ctx_chunks, n_chunks)
                sl = pl.ds(pl.multiple_of(ch * c, c), c)
                pc = pc_ref[d, pl.ds(pl.multiple_of(ch * SUBLANES, SUBLANES), SUBLANES), :][0:1]
                g, h, q, y0 = g_ref[d, sl, :], h_ref[d, sl, :], q_ref[d, sl, :], y_ref[d, sl, :]
                for j in range(RW_NP):
                    s0 = st[d * RW_NP + j]
                    sp = mk(s0)
                    y = _dg(mk(q[:, col(j)]).lhs(), sp.rhs_nt(), NT) + y0[:, col(j)]
                    st[d * RW_NP + j] = (s0 * pc[:, col(j)] + _mmp(sp, mk(g[:, col(j)]))) + h[:, col(j)]
                    acc_ref[sl, col(j)] += y
        return tuple(st)

    lax.fori_loop(0, n_chunks // SCAN_UNROLL, scan,
                  tuple(jnp.zeros((HEAD, RW_W), f32) for _ in range(2 * RW_NP)))

    lnw = lnw_ref[...]
    lnb = lnb_ref[...]

    n_tiles = t_rows // ROW_TILE
    tiles_per_step = 3 if n_tiles % 3 == 0 else 1

    def readout(i, carry):
        sls = [pl.ds(pl.multiple_of((i * tiles_per_step + u) * ROW_TILE, ROW_TILE), ROW_TILE)
               for u in range(tiles_per_step)]
        ys = [acc_ref[sl, :] for sl in sls]
        ycs = [y - _group_sum(y, gmat) * (1.0 / HEAD) for y in ys]
        vrs = [_group_sum(yc * yc, gmat) * (1.0 / HEAD) for yc in ycs]
        for sl, yc, var in zip(sls, ycs, vrs):
            g = gate_ref[0, sl, :]
            o_ref[0, sl, :] = ((yc * lax.rsqrt(var + GN_EPS) * lnw + lnb + bon_ref[sl, :])
                               * (g * jax.nn.sigmoid(g))).astype(bf16)
        return carry

    lax.fori_loop(0, n_tiles // tiles_per_step, readout, 0)


def _rwkv(rw, gates, shift_w, w0, w2, a0, a2, k_k, k_a, r_k, ln_w, ln_b, n_ctx):
    b, t, d_shift = rw.shape
    d_rwkv = w0.shape[-1]
    lora2 = d_shift - 3 * d_rwkv
    nhalf = d_rwkv // HALF
    xcol = (3 * d_rwkv) // lora2
    assert (t // RW_C) % RW_UNROLL == 0 and (t // RW_C) % SCAN_UNROLL == 0
    kern = functools.partial(_rwkv_kernel, n_ctx_chunks=n_ctx // RW_C, n_chunks=t // RW_C)
    seq = lambda off: pl.BlockSpec((1, t, HALF), lambda i, j, off=off: (i, 0, off * nhalf + j))
    swb = lambda off: pl.BlockSpec((3, HALF), lambda i, j, off=off: (0, off * nhalf + j))
    vec = lambda rows: pl.BlockSpec((rows, HALF), lambda i, j: (0, j))
    lor = pl.BlockSpec((2, lora2 // 2, HALF), lambda i, j: (0, 0, j))
    return pl.pallas_call(
        kern,
        grid=(b, nhalf),
        in_specs=[seq(0), seq(1), seq(2),
                  pl.BlockSpec((1, t, lora2), lambda i, j: (i, 0, xcol)),
                  swb(0), swb(1), swb(2),
                  pl.BlockSpec((3, lora2), lambda i, j: (0, xcol)),
                  vec(2), lor, vec(2), lor, vec(1), vec(1), vec(1), vec(1), vec(1),
                  pl.BlockSpec((1, t, HALF), lambda i, j: (i, 0, gates.shape[-1] // HALF - nhalf + j))],
        out_specs=pl.BlockSpec((1, t, HALF), lambda i, j: (i, 0, j)),
        out_shape=jax.ShapeDtypeStruct((b, t, d_rwkv), bf16),
        scratch_shapes=[pltpu.VMEM((t, HALF), f32)] * 2 + [pltpu.VMEM((2, t, HALF), f32)] * 4
        + [pltpu.VMEM((2, (t // RW_C) * SUBLANES, HALF), f32)],
        compiler_params=pltpu.CompilerParams(vmem_limit_bytes=VMEM_LIMIT),
        name="rwkv7",
    )(rw, rw, rw, rw, shift_w, shift_w, shift_w, shift_w,
      w0, w2, a0, a2, k_k, k_a, r_k, ln_w, ln_b, gates)


def _outproj_kernel(*refs, n_x, n_out, skip, final):
    g = OUT_TILES
    ret_refs, rwo_refs, refs = refs[:g], refs[g:2 * g], refs[2 * g:]
    xs_refs, refs = refs[:g * n_x], refs[g * n_x:]
    mod_refs, (w_ref, fw_ref, o_ref) = refs[:g], refs[g:]
    act = jnp.concatenate([jnp.concatenate([ret_refs[t][...], rwo_refs[t][...]], axis=1) for t in range(g)], axis=0)
    mix = _dg(act, w_ref[...], NN)
    xs, gates = [], []
    for t in range(g):
        if n_x == 1:
            xs.append(xs_refs[t][...])
        else:
            tile = lax.rem(pl.program_id(0) * g + t, n_out) + skip
            xs.append(jnp.where(tile == 0, xs_refs[2 * t][...], xs_refs[2 * t + 1][...]))
        gates.append(jnp.broadcast_to(mod_refs[t][0, 0][2:3], xs[-1].shape))
    xn = jnp.concatenate(xs, axis=0) + jnp.concatenate(gates, axis=0) * mix
    if final:
        ms = jnp.mean(xn * xn, axis=-1, keepdims=True)
        xn = xn * lax.rsqrt(ms + NORM_EPS) * fw_ref[...]
    o_ref[...] = xn


def _outproj(ret, rwo, xs, modv, w, fw, n_ctx, final):
    b, t, d_ret = ret.shape
    d = xs[0].shape[-1]
    d_rw = rwo.shape[-1]
    skip = (n_ctx // ROW_TILE) if final else 0
    n_tiles = t // ROW_TILE
    n_out = n_tiles - skip
    g = OUT_TILES
    assert (b * n_out) % g == 0
    batch = lambda s, k: lax.div(s * g + k, n_out)
    tile = lambda s, k: lax.rem(s * g + k, n_out) + skip
    row_of = lambda s, k: batch(s, k) * n_tiles + tile(s, k)
    per_tile = lambda shape, imap: [pl.BlockSpec(shape, functools.partial(imap, k=k)) for k in range(g)]
    x_specs = []
    for k in range(g):
        if len(xs) == 1:
            x_specs.append(pl.BlockSpec((ROW_TILE, d), functools.partial(lambda s, k: (row_of(s, k), 0), k=k)))
        else:
            assert skip == 0 and xs[0].shape[1] == ROW_TILE
            x_specs += [pl.BlockSpec((ROW_TILE, d), lambda s, k=k: (batch(s, k), 0)),
                        pl.BlockSpec((ROW_TILE, d), lambda s, k=k: (
                            batch(s, k) * (n_tiles - 1) + jnp.maximum(tile(s, k) - 1, 0), 0))]
    flat = [a.reshape(-1, d) for a in xs]
    kern = functools.partial(_outproj_kernel, n_x=len(xs), n_out=n_out, skip=skip, final=final)
    out = pl.pallas_call(
        kern,
        grid=(b * n_out // g,),
        in_specs=per_tile((ROW_TILE, d_ret), lambda s, k: (row_of(s, k), 0))
        + per_tile((ROW_TILE, d_rw), lambda s, k: (row_of(s, k), 0))
        + x_specs
        + per_tile((1, 1, SUBLANES, d), lambda s, k: (batch(s, k), jnp.minimum(tile(s, k), 1), 0, 0))
        + [pl.BlockSpec(w.shape, lambda s: (0, 0), pipeline_mode=pl.Buffered(1)),
           pl.BlockSpec((1, d), lambda s: (0, 0))],
        out_specs=pl.BlockSpec((g * ROW_TILE, d), lambda s: (s, 0)),
        out_shape=jax.ShapeDtypeStruct((b * n_out * ROW_TILE, d), f32),
        compiler_params=pltpu.CompilerParams(vmem_limit_bytes=VMEM_LIMIT),
        name="out_proj",
    )(*([ret.reshape(-1, d_ret)] * g), *([rwo.reshape(-1, d_rw)] * g),
      *[flat[i] for _ in range(g) for i in range(len(xs))], *([modv] * g), w, fw)
    return out.reshape(b, n_out * ROW_TILE, d)


def _rope_tables(n_ctx, seq, d_ret):
    nf = ROPE_NF
    inv = ROPE_BASE ** (-jnp.arange(nf, dtype=f32) / nf)
    pos = jnp.arange(seq)
    row_pos = (pos // GRID_W).astype(f32)
    col_pos = (pos % GRID_W).astype(f32)
    ang_r = row_pos[:, None] * inv[None, :]
    ang_c = col_pos[:, None] * inv[None, :]
    cos_h = jnp.concatenate([jnp.cos(ang_r)] * 2 + [jnp.cos(ang_c)] * 2, axis=-1)
    sin_h = jnp.concatenate([-jnp.sin(ang_r), jnp.sin(ang_r), -jnp.sin(ang_c), jnp.sin(ang_c)], axis=-1)
    reps = d_ret // HEAD
    cos_l = jnp.tile(cos_h, (1, reps))
    sin_l = jnp.tile(sin_h, (1, reps))
    cos_t = jnp.concatenate([jnp.ones((n_ctx, d_ret), f32), cos_l], axis=0)
    sin_t = jnp.concatenate([jnp.zeros((n_ctx, d_ret), f32), sin_l], axis=0)
    return cos_t, sin_t


def kernel(x, c, ctx, c_ctx, norm_w, w_mod, b_mod, w_in, ret_log_gamma, ret_norm_w, rwkv_shift_w, rwkv_w0, rwkv_w2, rwkv_a0, rwkv_a2, rwkv_k_k, rwkv_k_a, rwkv_r_k, rwkv_ln_w, rwkv_ln_b, w_out, final_norm_w):
    b, seq, d = x.shape
    n_ctx = ctx.shape[1]
    depth = w_in.shape[0]
    d_ret = ret_norm_w.shape[-1]
    d_rwkv = rwkv_w0.shape[-1]
    d_shift = rwkv_shift_w.shape[-1]

    assert b < COND_ROWS
    cond = jnp.zeros((COND_ROWS, d), f32).at[:b].set(c.astype(f32)).at[b].set(c_ctx.astype(f32))
    mods = _modulation(cond, w_mod, b_mod)
    cos_t, sin_t = _rope_tables(n_ctx, seq, d_ret)
    if depth > 1 and n_ctx == ROW_TILE:
        xs = (ctx.astype(f32), x.astype(f32))
    else:
        xs = (jnp.concatenate([ctx.astype(f32), x.astype(f32)], axis=1),)

    out = None
    for layer in range(depth):
        m = mods[layer].reshape(COND_ROWS, 3, d)
        m = jnp.stack([m[:, 1], m[:, 0], m[:, 2]], axis=1)
        lat = m[:b]
        cx = jnp.broadcast_to(m[b][None], (b, 3, d))
        modv = jnp.pad(jnp.stack([cx, lat], axis=1), ((0, 0), (0, 0), (0, SUBLANES - 3), (0, 0)))

        qkv, rw, gates = _inproj(xs, modv, norm_w[layer][None], w_in[layer].astype(bf16), cos_t, sin_t,
                                 d_ret, d_shift, d_rwkv)
        ret = _retention(qkv, gates, ret_log_gamma[layer], ret_norm_w[layer][None], n_ctx)
        rwo = _rwkv(rw, gates, rwkv_shift_w[layer], rwkv_w0[layer], rwkv_w2[layer], rwkv_a0[layer],
                    rwkv_a2[layer], rwkv_k_k[layer][None], rwkv_k_a[layer][None],
                    rwkv_r_k[layer].reshape(1, d_rwkv), rwkv_ln_w[layer][None],
                    rwkv_ln_b[layer][None], n_ctx)
        final = layer == depth - 1
        res = _outproj(ret, rwo, xs, modv, w_out[layer].astype(bf16), final_norm_w[None], n_ctx, final)
        if final:
            out = res
        else:
            xs = (res,)
    return out
```

```python
import functools
import math

import jax
import jax.numpy as jnp
from jax import lax
from jax.experimental import pallas as pl
from jax.experimental.pallas import tpu as pltpu

f32 = jnp.float32
bf16 = jnp.bfloat16

HEAD = 64
PAIR = 2 * HEAD
RET_C = 128
RW_C = 64
INV_BLOCK = 32
RW_UNROLL = 6
SCAN_UNROLL = 12
GRID_W = 64
ROPE_BASE = 10000.0
NORM_EPS = 1e-6
GN_EPS = 64e-5
ROW_TILE = 256
IN_TILES = 2
OUT_TILES = 4
HALF = 256
RET_W = 512
NP = RET_W // PAIR
RW_W = 128
RW_NP = HALF // RW_W
VMEM_LIMIT = 56 * 1024 * 1024
SUBLANES = 8
COND_ROWS = 16
MOD_TILE = 1024
ROPE_NF = HEAD // 4

NN = ((1,), (0,))
NT = ((1,), (1,))
TN = ((0,), (0,))

P_MOD = 3
P_RET = 1
P_RW = 1
P_SUM = 1


def _dg(a, b, dims):
    return lax.dot_general(a, b, (dims, ((), ())), preferred_element_type=f32)


def _split2(a):
    hi = a.astype(bf16)
    lo = (a - hi.astype(f32)).astype(bf16)
    return hi, lo


def _mm(a, b, dims=NN, passes=3):
    if passes == 1:
        return _dg(a.astype(bf16), b.astype(bf16), dims)
    ah, al = _split2(a)
    bh, bl = _split2(b)
    return _dg(ah, bh, dims) + (_dg(ah, bl, dims) + _dg(al, bh, dims))


def _cumsum_rows(x, reverse):
    n = x.shape[0]
    row = lax.broadcasted_iota(jnp.int32, (n, 1), 0)
    s = 1
    while s < n:
        if reverse:
            x = x + jnp.where(row < n - s, pltpu.roll(x, n - s, 0), 0.0)
        else:
            x = x + jnp.where(row >= s, pltpu.roll(x, s, 0), 0.0)
        s *= 2
    return x


def _group_sum(x, gmat, pieces=P_SUM):
    rows, width = x.shape
    gw = gmat.shape[0]
    cols = width // gw
    xs = jnp.concatenate([x[:, c * gw:(c + 1) * gw] for c in range(cols)], axis=0)
    if pieces == 1:
        s = _dg(xs.astype(bf16), gmat, NN)
    else:
        h, l = _split2(xs)
        s = _dg(h, gmat, NN) + _dg(l, gmat, NN)
    return jnp.concatenate([s[c * rows:(c + 1) * rows] for c in range(cols)], axis=1)


def _group_mat():
    ii = lax.broadcasted_iota(jnp.int32, (PAIR, PAIR), 0) // HEAD
    jj = lax.broadcasted_iota(jnp.int32, (PAIR, PAIR), 1) // HEAD
    return jnp.where(ii == jj, 1.0, 0.0).astype(bf16)


def _head_masks(width):
    blk = lax.broadcasted_iota(jnp.int32, (1, width), 1) // HEAD
    return [blk == h for h in range(width // HEAD)]


def _bd(x, masks):
    return jnp.concatenate([jnp.where(m, x, 0.0) for m in masks], axis=0)


def _heads_to_rows(x):
    w = x.shape[1]
    return jnp.concatenate([x[:, :HEAD]] + [pltpu.roll(x, w - h * HEAD, 1)[:, :HEAD]
                                            for h in range(1, w // HEAD)], axis=0)


def _mod_kernel(c_ref, w_ref, b_ref, o_ref):
    cnd = c_ref[...]
    s = cnd * jax.nn.sigmoid(cnd)
    o_ref[0] = _mm(s, w_ref[0], NN, P_MOD) + b_ref[0]


def _modulation(cond, w_mod, b_mod):
    depth, d, d3 = w_mod.shape
    rows = cond.shape[0]
    tn = MOD_TILE
    return pl.pallas_call(
        _mod_kernel,
        grid=(depth, d3 // tn),
        in_specs=[pl.BlockSpec((rows, d), lambda l, j: (0, 0)),
                  pl.BlockSpec((1, d, tn), lambda l, j: (l, 0, j)),
                  pl.BlockSpec((1, 1, tn), lambda l, j: (l, 0, j))],
        out_specs=pl.BlockSpec((1, rows, tn), lambda l, j: (l, 0, j)),
        out_shape=jax.ShapeDtypeStruct((depth, rows, d3), f32),
        compiler_params=pltpu.CompilerParams(vmem_limit_bytes=VMEM_LIMIT),
        name="adaln_mod",
    )(cond, w_mod, b_mod.reshape(depth, 1, d3))


def _inproj_kernel(*refs, n_x, n_tiles, d_ret, d_shift):
    g = IN_TILES
    xs_refs, refs = refs[:g * n_x], refs[g * n_x:]
    mod_refs, (nw_ref, w_ref), refs = refs[:g], refs[g:g + 2], refs[g + 2:]
    cos_refs, sin_refs, (qkv_ref, rw_ref, g_ref) = refs[:g], refs[g:2 * g], refs[2 * g:]
    hs = []
    for t in range(g):
        if n_x == 1:
            x = xs_refs[t][...]
        else:
            tile = lax.rem(pl.program_id(0) * g + t, n_tiles)
            x = jnp.where(tile == 0, xs_refs[2 * t][...], xs_refs[2 * t + 1][...])
        ms = jnp.mean(x * x, axis=-1, keepdims=True)
        m = mod_refs[t][0, 0]
        h = x * lax.rsqrt(ms + NORM_EPS) * nw_ref[...]
        hs.append((h * (1.0 + m[0:1]) + m[1:2]).astype(bf16))
    def proj(lo, hi):
        return jnp.concatenate([_dg(h, w_ref[:, lo:hi], NN) for h in hs], axis=0)

    cos = jnp.concatenate([r[...] for r in cos_refs], axis=0)
    sin = jnp.concatenate([r[...] for r in sin_refs], axis=0)
    lane = lax.broadcasted_iota(jnp.int32, cos.shape, 1)
    first = (lane % (2 * ROPE_NF)) < ROPE_NF

    def rope(t):
        nl = t.shape[-1]
        swapped = jnp.where(first, pltpu.roll(t, nl - ROPE_NF, 1), pltpu.roll(t, ROPE_NF, 1))
        return t * cos + swapped * sin

    qkv_ref[:, 0:d_ret] = rope(proj(0, d_ret)).astype(bf16)
    qkv_ref[:, d_ret:2 * d_ret] = (rope(proj(d_ret, 2 * d_ret)) * (HEAD ** -0.5)).astype(bf16)
    qkv_ref[:, 2 * d_ret:3 * d_ret] = proj(2 * d_ret, 3 * d_ret).astype(bf16)
    g_ref[:, 0:d_ret] = proj(3 * d_ret, 4 * d_ret)
    o = 4 * d_ret
    rw_ref[...] = proj(o, o + d_shift)
    g_ref[:, d_ret:] = proj(o + d_shift, w_ref.shape[1])


def _inproj(xs, modv, nw, w, cos_t, sin_t, d_ret, d_shift, d_rwkv):
    b, d = xs[0].shape[0], xs[0].shape[-1]
    t = sum(a.shape[1] for a in xs)
    n_tiles = t // ROW_TILE
    g = IN_TILES
    assert (b * n_tiles) % g == 0
    batch = lambda s, k: lax.div(s * g + k, n_tiles)
    tile = lambda s, k: lax.rem(s * g + k, n_tiles)
    flat = [a.reshape(-1, d) for a in xs]
    x_specs = []
    for k in range(g):
        if len(xs) == 1:
            x_specs.append(pl.BlockSpec((ROW_TILE, d), lambda s, k=k: (s * g + k, 0)))
        else:
            assert xs[0].shape[1] == ROW_TILE
            x_specs += [pl.BlockSpec((ROW_TILE, d), lambda s, k=k: (batch(s, k), 0)),
                        pl.BlockSpec((ROW_TILE, d), lambda s, k=k: (
                            batch(s, k) * (n_tiles - 1) + jnp.maximum(tile(s, k) - 1, 0), 0))]
    per_tile = lambda shape, imap: [pl.BlockSpec(shape, functools.partial(imap, k=k)) for k in range(g)]
    kern = functools.partial(_inproj_kernel, n_x=len(xs), n_tiles=n_tiles, d_ret=d_ret, d_shift=d_shift)
    widths = (3 * d_ret, d_shift, d_ret + d_rwkv)
    outs = pl.pallas_call(
        kern,
        grid=(b * n_tiles // g,),
        in_specs=x_specs
        + per_tile((1, 1, SUBLANES, d), lambda s, k: (batch(s, k), jnp.minimum(tile(s, k), 1), 0, 0))
        + [pl.BlockSpec((1, d), lambda s: (0, 0)),
           pl.BlockSpec(w.shape, lambda s: (0, 0), pipeline_mode=pl.Buffered(1))]
        + per_tile((ROW_TILE, d_ret), lambda s, k: (tile(s, k), 0))
        + per_tile((ROW_TILE, d_ret), lambda s, k: (tile(s, k), 0)),
        out_specs=[pl.BlockSpec((g * ROW_TILE, n), lambda s: (s, 0)) for n in widths],
        out_shape=[jax.ShapeDtypeStruct((b * t, n), dt) for n, dt in zip(widths, (bf16, f32, f32))],
        compiler_params=pltpu.CompilerParams(vmem_limit_bytes=VMEM_LIMIT),
        name="in_proj",
    )(*[flat[i] for _ in range(g) for i in range(len(xs))], *([modv] * g), nw, w, *([cos_t] * g), *([sin_t] * g))
    return [o.reshape(b, t, n) for o, n in zip(outs, widths)]


def _scan_chunk(idx, forward, n_ctx_chunks, n_chunks):
    if forward:
        return idx
    return jnp.where(idx < n_ctx_chunks, n_ctx_chunks - 1 - idx, n_chunks + n_ctx_chunks - 1 - idx)


def _ret_kernel(lg_ref, q_ref, k_ref, v_ref, gate_ref, nw_ref, o_ref,
                acc_ref, s_ref, mask_ref, qd_ref, kd_ref, cd_ref, *, n_ctx_chunks, n_chunks):
    hh = pl.program_id(1)
    c = RET_C
    masks = _head_masks(PAIR)
    first = masks[0]
    ii = lax.broadcasted_iota(jnp.int32, (c, 2 * c), 0)
    nn = lax.broadcasted_iota(jnp.int32, (c, 2 * c), 1)
    ri = lax.broadcasted_iota(jnp.int32, (c, PAIR), 0).astype(f32)
    for d in range(2):
        rel = ((ii - nn % c) if d == 0 else (nn % c - ii)).astype(f32)
        for j in range(NP):
            lg0 = lg_ref[d, hh * 2 * NP + 2 * j]
            lg1 = lg_ref[d, hh * 2 * NP + 2 * j + 1]
            lgm = jnp.where(nn < c, lg0, lg1)
            lgv = jnp.where(first, lg0, lg1)
            mask_ref[d, j] = jnp.where(rel >= 0.0, jnp.exp(lgm * jnp.maximum(rel, 0.0)), 0.0)
            qpow = (ri + 1.0) if d == 0 else (c - ri)
            kpow = (c - 1.0 - ri) if d == 0 else ri
            qd_ref[d, j] = jnp.exp(lgv * qpow)
            kd_ref[d, j] = jnp.exp(lgv * kpow)
            cd_ref[d, j] = jnp.exp(jnp.broadcast_to(lgv * c, (HEAD, PAIR)))
    s_ref[...] = jnp.zeros(s_ref.shape, f32)
    acc_ref[...] = jnp.zeros(acc_ref.shape, f32)
    chains = [(d, j) for d in range(2) for j in range(NP)]

    def step(s, carry):
        rows, qp, kp, vp = {}, {}, {}, {}
        for d in range(2):
            ch = _scan_chunk(s, d == 0, n_ctx_chunks, n_chunks)
            r0 = pl.multiple_of(ch * c, c)
            rows[d] = pl.ds(r0, c)
            q = q_ref[0, rows[d], :]
            k = k_ref[0, rows[d], :]
            v = v_ref[0, rows[d], :]
            for j in range(NP):
                col = slice(j * PAIR, (j + 1) * PAIR)
                qp[d, j], kp[d, j], vp[d, j] = q[:, col], k[:, col], v[:, col]
        sc, st, vb = {}, {}, {}
        for ck in chains:
            sc[ck] = _mm(qp[ck], _bd(kp[ck], masks), NT, P_RET) * mask_ref[ck[0], ck[1]]
            st[ck] = s_ref[ck[0], ck[1]]
            vb[ck] = _bd(vp[ck], masks)
        for (d, j) in chains:
            ck = (d, j)
            o = _mm(sc[ck], vb[ck], NN, P_RET) + _mm(qp[ck] * qd_ref[d, j], _bd(st[ck], masks), NN, P_RET)
            s_ref[d, j] = st[ck] * cd_ref[d, j] + _mm(_heads_to_rows(kp[ck] * kd_ref[d, j]), vb[ck], TN, P_RET)
            acc_ref[rows[d], j * PAIR:(j + 1) * PAIR] += o
        return carry

    lax.fori_loop(0, n_chunks, step, 0)

    gmat = _group_mat()
    nw = nw_ref[...]

    n_tiles = (n_chunks * c) // ROW_TILE
    tiles_per_step = 3 if n_tiles % 3 == 0 else 1

    def norm_tiles(i, carry):
        sls = [pl.ds(pl.multiple_of((i * tiles_per_step + u) * ROW_TILE, ROW_TILE), ROW_TILE)
               for u in range(tiles_per_step)]
        os_ = [acc_ref[sl, :] for sl in sls]
        mss = [_group_sum(o * o, gmat) * (1.0 / HEAD) for o in os_]
        for sl, o, ms in zip(sls, os_, mss):
            g = gate_ref[0, sl, :]
            o_ref[0, sl, :] = ((o * lax.rsqrt(ms + NORM_EPS) * nw) * (g * jax.nn.sigmoid(g))).astype(bf16)
        return carry

    lax.fori_loop(0, n_tiles // tiles_per_step, norm_tiles, 0)


def _retention(qkv, gates, log_gamma, norm_w, n_ctx):
    b, t, w3 = qkv.shape
    d_ret = w3 // 3
    nhalf = d_ret // RET_W
    kern = functools.partial(_ret_kernel, n_ctx_chunks=n_ctx // RET_C, n_chunks=t // RET_C)
    blk = lambda off: pl.BlockSpec((1, t, RET_W), lambda i, j, off=off: (i, 0, off * nhalf + j))
    return pl.pallas_call(
        kern,
        grid=(b, nhalf),
        in_specs=[pl.BlockSpec(memory_space=pltpu.SMEM),
                  blk(0), blk(1), blk(2),
                  pl.BlockSpec((1, t, RET_W), lambda i, j: (i, 0, j)),
                  pl.BlockSpec((1, RET_W), lambda i, j: (0, j))],
        out_specs=pl.BlockSpec((1, t, RET_W), lambda i, j: (i, 0, j)),
        out_shape=jax.ShapeDtypeStruct((b, t, d_ret), bf16),
        scratch_shapes=[pltpu.VMEM((t, RET_W), f32),
                        pltpu.VMEM((2, NP, HEAD, PAIR), f32),
                        pltpu.VMEM((2, NP, RET_C, 2 * RET_C), f32),
                        pltpu.VMEM((2, NP, RET_C, PAIR), f32),
                        pltpu.VMEM((2, NP, RET_C, PAIR), f32),
                        pltpu.VMEM((2, NP, HEAD, PAIR), f32)],
        compiler_params=pltpu.CompilerParams(vmem_limit_bytes=VMEM_LIMIT),
        name="retention",
    )(log_gamma, qkv, qkv, qkv, gates, norm_w)


class _Pair:
    def __init__(self, x, masks):
        self.x = x
        self.masks = masks
        self._parts = None
        self._lhs = None
        self._bd = None

    def parts(self):
        if self._parts is None:
            hi = self.x.astype(bf16).astype(f32)
            self._parts = (hi, self.x - hi)
        return self._parts

    def lhs(self):
        if self._lhs is None:
            if P_RW == 1:
                self._lhs = self.x.astype(bf16)
            else:
                hi, lo = self.parts()
                self._lhs = jnp.concatenate([hi, hi, lo], axis=1).astype(bf16)
        return self._lhs

    def _bd_parts(self):
        if self._bd is None:
            if P_RW == 1:
                self._bd = (_bd(self.x, self.masks).astype(bf16),)
            else:
                hi, lo = self.parts()
                bh = _bd(hi, self.masks).astype(bf16)
                self._bd = (bh, _bd(lo, self.masks).astype(bf16), bh)
        return self._bd

    def rhs(self):
        return jnp.concatenate(self._bd_parts(), axis=0)

    def rhs_nt(self):
        return jnp.concatenate(self._bd_parts(), axis=1)

    def lhs_rows(self):
        if P_RW == 1:
            return _heads_to_rows(self.x).astype(bf16)
        hi, lo = self.parts()
        th = _heads_to_rows(hi)
        return jnp.concatenate([th, th, _heads_to_rows(lo)], axis=0).astype(bf16)


def _lhs_rows(x):
    if P_RW == 1:
        return x.astype(bf16)
    hi = x.astype(bf16).astype(f32)
    return jnp.concatenate([hi, hi, x - hi], axis=0).astype(bf16)


def _mmp(a, b):
    return _dg(a.lhs(), b.rhs(), NN)


def _rwkv_kernel(r_ref, k_ref, v_ref, xwa_ref, swr_ref, swk_ref, swv_ref, swx_ref,
                 w0_ref, w2_ref, a0_ref, a2_ref, kk_ref, ka_ref, rk_ref, lnw_ref, lnb_ref, gate_ref,
                 o_ref, acc_ref, bon_ref,
                 g_ref, h_ref, q_ref, y_ref, pc_ref,
                 *, n_ctx_chunks, n_chunks):
    c = RW_C
    t_rows = n_chunks * c
    gmat = _group_mat()
    rows = lax.broadcasted_iota(jnp.int32, (c, 1), 0)

    def conv(ref, sw_ref, ch, r0):
        x = ref[0, pl.ds(r0, c), :]
        p0 = pl.multiple_of(jnp.maximum(r0 - SUBLANES, 0), SUBLANES)
        n0 = pl.multiple_of(jnp.minimum(r0 + c, t_rows - SUBLANES), SUBLANES)
        prev8 = ref[0, pl.ds(p0, SUBLANES), :]
        next8 = ref[0, pl.ds(n0, SUBLANES), :]
        has_prev = jnp.logical_and(ch != 0, ch != n_ctx_chunks).astype(f32)
        has_next = jnp.logical_and(ch != n_ctx_chunks - 1, ch != n_chunks - 1).astype(f32)
        xd = jnp.where(rows == 0, prev8[SUBLANES - 1:SUBLANES, :] * has_prev, pltpu.roll(x, 1, 0))
        xu = jnp.where(rows == c - 1, next8[0:1, :] * has_next, pltpu.roll(x, c - 1, 0))
        sw = sw_ref[...]
        return sw[0:1] * xd + sw[1:2] * x + sw[2:3] * xu

    masks = _head_masks(RW_W)
    ii = lax.broadcasted_iota(jnp.int32, (c, RW_W), 0)
    jj = lax.broadcasted_iota(jnp.int32, (c, RW_W), 1) % HEAD
    eye = jnp.where(ii == jj, 1.0, 0.0)
    same = lambda n: (ii // n) == (jj // n)
    diag_blk = same(INV_BLOCK)
    merges = []
    n = INV_BLOCK
    while n < c:
        merges.append(jnp.logical_and(same(2 * n), jnp.logical_not(same(n))))
        n *= 2
    strict = (jj < ii, jj > ii)
    incl = (jj <= ii, jj >= ii)
    both = tuple(jnp.concatenate([strict[d], incl[d]], axis=0) for d in range(2))
    lora = xwa_ref.shape[2] // 2
    wscale = -math.exp(-0.5)
    mk = lambda x: _Pair(x, masks)
    col = lambda j: slice(j * RW_W, (j + 1) * RW_W)

    n_groups = n_chunks // RW_UNROLL

    def features(grp):
        out = []
        rs, ks, vs, xs = [], [], [], []
        for u in range(RW_UNROLL):
            ch = grp * RW_UNROLL + u
            r0 = pl.multiple_of(ch * c, c)
            rs.append(conv(r_ref, swr_ref, ch, r0))
            ks.append(conv(k_ref, swk_ref, ch, r0))
            vs.append(conv(v_ref, swv_ref, ch, r0))
            xs.append(conv(xwa_ref, swx_ref, ch, r0))
        kks = [k * kk_ref[...] for k in ks]
        sums = _group_sum(jnp.concatenate([kk * kk for kk in kks]
                                          + [r * k * rk_ref[...] for r, k in zip(rs, ks)], axis=0), gmat)
        xall = jnp.concatenate(xs, axis=0)
        txw = jnp.tanh(xall[:, :lora])
        kaps = []
        for u in range(RW_UNROLL):
            ch = grp * RW_UNROLL + u
            kaps.append(kks[u] / jnp.maximum(jnp.sqrt(sums[u * c:(u + 1) * c]), 1e-12))
            bon_ref[pl.ds(pl.multiple_of(ch * c, c), c), :] = sums[(RW_UNROLL + u) * c:(RW_UNROLL + u + 1) * c] * vs[u]
            out.append(vs[u])
        for d in range(2):
            lw_all = wscale * jax.nn.sigmoid(w0_ref[d:d + 1, :] + _mm(txw, w2_ref[d], NN, P_RW))
            a_all = jax.nn.sigmoid(a0_ref[d:d + 1, :] + _mm(xall[:, lora:], a2_ref[d], NN, P_RW))
            for u in range(RW_UNROLL):
                ch = grp * RW_UNROLL + u
                lw = lw_all[u * c:(u + 1) * c]
                a = a_all[u * c:(u + 1) * c]
                kmod = ks[u] * (1.0 + (a - 1.0) * ka_ref[...])
                b = kaps[u] * a
                lp = _cumsum_rows(lw, reverse=(d == 1))
                lpc = lp[c - 1:c, :] if d == 0 else lp[0:1, :]
                pinv = jnp.exp(-lp)
                ptail = jnp.exp(lpc - lp)
                pc_ref[d, pl.ds(pl.multiple_of(ch * SUBLANES, SUBLANES), SUBLANES), :] = jnp.broadcast_to(
                    jnp.exp(lpc), (SUBLANES, lp.shape[1]))
                out += [jnp.concatenate([kaps[u] * jnp.exp(lp - lw), rs[u] * jnp.exp(lp)], axis=0),
                        kmod * pinv, b * pinv, kmod * ptail, b * ptail]
        return out

    def prepare(it, carry):
        cur = features(it)
        feat = {}
        for u in range(RW_UNROLL):
            sl = pl.ds(pl.multiple_of((it * RW_UNROLL + u) * c, c), c)
            for d in range(2):
                kr, kd, bd, kh, bh = cur[RW_UNROLL + 5 * (d * RW_UNROLL + u):RW_UNROLL + 5 * (d * RW_UNROLL + u + 1)]
                feat[u, d] = dict(sl=sl, v=cur[u], kr=kr, rq=kr[c:], kd=kd, bd=bd, kh=kh, bh=bh)

        chains = [(u, d, j) for u in range(RW_UNROLL) for d in range(2) for j in range(RW_NP)]
        mkv, m_b, n_b, vp = {}, {}, {}, {}
        for ck in chains:
            u, d, j = ck
            f = feat[u, d]
            kr = mk(f["kr"][:, col(j)])
            mk_ = _dg(kr.lhs(), mk(f["kd"][:, col(j)]).rhs_nt(), NT)
            mb_ = _dg(kr.lhs(), mk(f["bd"][:, col(j)]).rhs_nt(), NT)
            vp[ck] = mk(f["v"][:, col(j)])
            mkv[ck] = _mmp(mk(jnp.where(both[d], mk_, 0.0)), vp[ck])
            m_b[ck] = jnp.where(strict[d], mb_[:c], 0.0)
            n_b[ck] = mk(jnp.where(incl[d], mb_[c:], 0.0))
        p, npow = {}, {}
        for ck in chains:
            nd = mk(jnp.where(diag_blk, -m_b[ck], 0.0))
            p[ck] = eye + nd.x
            npow[ck] = mk(_mmp(nd, nd))
        for level in range(int(math.log2(INV_BLOCK)) - 2):
            for ck in chains:
                both_ = _mmp(mk(jnp.concatenate([p[ck], npow[ck].x], axis=0)), npow[ck])
                p[ck] = p[ck] + both_[:c]
                npow[ck] = mk(both_[c:])
        for ck in chains:
            p[ck] = p[ck] + _mmp(mk(p[ck]), npow[ck])
        for off in merges:
            tmp, pm = {}, {}
            for ck in chains:
                pm[ck] = mk(p[ck])
                tmp[ck] = mk(_mmp(pm[ck], mk(jnp.where(off, m_b[ck], 0.0))))
            for ck in chains:
                p[ck] = p[ck] - _mmp(tmp[ck], pm[ck])
        ta = {}
        for ck in chains:
            u, d, j = ck
            kq = feat[u, d]["kr"][:c, col(j)]
            ta[ck] = _dg(mk(p[ck]).lhs(), jnp.concatenate([mk(kq).rhs(), mk(mkv[ck][:c]).rhs()], axis=1), NN)
        for ck in chains:
            u, d, j = ck
            f = feat[u, d]
            a1, z0 = ta[ck][:, :RW_W], ta[ck][:, RW_W:]
            nb = _dg(n_b[ck].lhs(), jnp.concatenate([mk(a1).rhs(), mk(z0).rhs()], axis=1), NN)
            az = jnp.concatenate([_heads_to_rows(a1), _heads_to_rows(z0)], axis=1)
            bh = mk(f["bh"][:, col(j)])
            ab = _dg(_lhs_rows(az), bh.rhs(), TN)
            vk = _dg(vp[ck].lhs_rows(), mk(f["kh"][:, col(j)]).rhs(), TN)
            g_ref[d, f["sl"], col(j)] = -ab[:HEAD]
            h_ref[d, f["sl"], col(j)] = vk - ab[HEAD:]
            q_ref[d, f["sl"], col(j)] = f["rq"][:, col(j)] - nb[:, :RW_W]
            y_ref[d, f["sl"], col(j)] = mkv[ck][c:] - nb[:, RW_W:]
        return carry

    lax.fori_loop(0, n_groups, prepare, 0)

    acc_ref[...] = jnp.zeros(acc_ref.shape, f32)

    def scan(it, st):
        st = list(st)
        for u in range(SCAN_UNROLL):
            for d in range(2):
                ch = _scan_chunk(it * SCAN_UNROLL + u, d == 0, n_ctx_chunks, n_chunks)
                sl = pl.ds(pl.multiple_of(ch * c, c), c)
                pc = pc_ref[d, pl.ds(pl.multiple_of(ch * SUBLANES, SUBLANES), SUBLANES), :][0:1]
                g, h, q, y0 = g_ref[d, sl, :], h_ref[d, sl, :], q_ref[d, sl, :], y_ref[d, sl, :]
                for j in range(RW_NP):
                    s0 = st[d * RW_NP + j]
                    sp = mk(s0)
                    y = _dg(mk(q[:, col(j)]).lhs(), sp.rhs_nt(), NT) + y0[:, col(j)]
                    st[d * RW_NP + j] = (s0 * pc[:, col(j)] + _mmp(sp, mk(g[:, col(j)]))) + h[:, col(j)]
                    acc_ref[sl, col(j)] += y
        return tuple(st)

    lax.fori_loop(0, n_chunks // SCAN_UNROLL, scan,
                  tuple(jnp.zeros((HEAD, RW_W), f32) for _ in range(2 * RW_NP)))

    lnw = lnw_ref[...]
    lnb = lnb_ref[...]

    n_tiles = t_rows // ROW_TILE
    tiles_per_step = 3 if n_tiles % 3 == 0 else 1

    def readout(i, carry):
        sls = [pl.ds(pl.multiple_of((i * tiles_per_step + u) * ROW_TILE, ROW_TILE), ROW_TILE)
               for u in range(tiles_per_step)]
        ys = [acc_ref[sl, :] for sl in sls]
        ycs = [y - _group_sum(y, gmat) * (1.0 / HEAD) for y in ys]
        vrs = [_group_sum(yc * yc, gmat) * (1.0 / HEAD) for yc in ycs]
        for sl, yc, var in zip(sls, ycs, vrs):
            g = gate_ref[0, sl, :]
            o_ref[0, sl, :] = ((yc * lax.rsqrt(var + GN_EPS) * lnw + lnb + bon_ref[sl, :])
                               * (g * jax.nn.sigmoid(g))).astype(bf16)
        return carry

    lax.fori_loop(0, n_tiles // tiles_per_step, readout, 0)


def _rwkv(rw, gates, shift_w, w0, w2, a0, a2, k_k, k_a, r_k, ln_w, ln_b, n_ctx):
    b, t, d_shift = rw.shape
    d_rwkv = w0.shape[-1]
    lora2 = d_shift - 3 * d_rwkv
    nhalf = d_rwkv // HALF
    xcol = (3 * d_rwkv) // lora2
    assert (t // RW_C) % RW_UNROLL == 0 and (t // RW_C) % SCAN_UNROLL == 0
    kern = functools.partial(_rwkv_kernel, n_ctx_chunks=n_ctx // RW_C, n_chunks=t // RW_C)
    seq = lambda off: pl.BlockSpec((1, t, HALF), lambda i, j, off=off: (i, 0, off * nhalf + j))
    swb = lambda off: pl.BlockSpec((3, HALF), lambda i, j, off=off: (0, off * nhalf + j))
    vec = lambda rows: pl.BlockSpec((rows, HALF), lambda i, j: (0, j))
    lor = pl.BlockSpec((2, lora2 // 2, HALF), lambda i, j: (0, 0, j))
    return pl.pallas_call(
        kern,
        grid=(b, nhalf),
        in_specs=[seq(0), seq(1), seq(2),
                  pl.BlockSpec((1, t, lora2), lambda i, j: (i, 0, xcol)),
                  swb(0), swb(1), swb(2),
                  pl.BlockSpec((3, lora2), lambda i, j: (0, xcol)),
                  vec(2), lor, vec(2), lor, vec(1), vec(1), vec(1), vec(1), vec(1),
                  pl.BlockSpec((1, t, HALF), lambda i, j: (i, 0, gates.shape[-1] // HALF - nhalf + j))],
        out_specs=pl.BlockSpec((1, t, HALF), lambda i, j: (i, 0, j)),
        out_shape=jax.ShapeDtypeStruct((b, t, d_rwkv), bf16),
        scratch_shapes=[pltpu.VMEM((t, HALF), f32)] * 2 + [pltpu.VMEM((2, t, HALF), f32)] * 4
        + [pltpu.VMEM((2, (t // RW_C) * SUBLANES, HALF), f32)],
        compiler_params=pltpu.CompilerParams(vmem_limit_bytes=VMEM_LIMIT),
        name="rwkv7",
    )(rw, rw, rw, rw, shift_w, shift_w, shift_w, shift_w,
      w0, w2, a0, a2, k_k, k_a, r_k, ln_w, ln_b, gates)


def _outproj_kernel(*refs, n_x, n_out, skip, final):
    g = OUT_TILES
    ret_refs, rwo_refs, refs = refs[:g], refs[g:2 * g], refs[2 * g:]
    xs_refs, refs = refs[:g * n_x], refs[g * n_x:]
    mod_refs, (w_ref, fw_ref, o_ref) = refs[:g], refs[g:]
    act = jnp.concatenate([jnp.concatenate([ret_refs[t][...], rwo_refs[t][...]], axis=1) for t in range(g)], axis=0)
    mix = _dg(act, w_ref[...], NN)
    xs, gates = [], []
    for t in range(g):
        if n_x == 1:
            xs.append(xs_refs[t][...])
        else:
            tile = lax.rem(pl.program_id(0) * g + t, n_out) + skip
            xs.append(jnp.where(tile == 0, xs_refs[2 * t][...], xs_refs[2 * t + 1][...]))
        gates.append(jnp.broadcast_to(mod_refs[t][0, 0][2:3], xs[-1].shape))
    xn = jnp.concatenate(xs, axis=0) + jnp.concatenate(gates, axis=0) * mix
    if final:
        ms = jnp.mean(xn * xn, axis=-1, keepdims=True)
        xn = xn * lax.rsqrt(ms + NORM_EPS) * fw_ref[...]
    o_ref[...] = xn


def _outproj(ret, rwo, xs, modv, w, fw, n_ctx, final):
    b, t, d_ret = ret.shape
    d = xs[0].shape[-1]
    d_rw = rwo.shape[-1]
    skip = (n_ctx // ROW_TILE) if final else 0
    n_tiles = t // ROW_TILE
    n_out = n_tiles - skip
    g = OUT_TILES
    assert (b * n_out) % g == 0
    batch = lambda s, k: lax.div(s * g + k, n_out)
    tile = lambda s, k: lax.rem(s * g + k, n_out) + skip
    row_of = lambda s, k: batch(s, k) * n_tiles + tile(s, k)
    per_tile = lambda shape, imap: [pl.BlockSpec(shape, functools.partial(imap, k=k)) for k in range(g)]
    x_specs = []
    for k in range(g):
        if len(xs) == 1:
            x_specs.append(pl.BlockSpec((ROW_TILE, d), functools.partial(lambda s, k: (row_of(s, k), 0), k=k)))
        else:
            assert skip == 0 and xs[0].shape[1] == ROW_TILE
            x_specs += [pl.BlockSpec((ROW_TILE, d), lambda s, k=k: (batch(s, k), 0)),
                        pl.BlockSpec((ROW_TILE, d), lambda s, k=k: (
                            batch(s, k) * (n_tiles - 1) + jnp.maximum(tile(s, k) - 1, 0), 0))]
    flat = [a.reshape(-1, d) for a in xs]
    kern = functools.partial(_outproj_kernel, n_x=len(xs), n_out=n_out, skip=skip, final=final)
    out = pl.pallas_call(
        kern,
        grid=(b * n_out // g,),
        in_specs=per_tile((ROW_TILE, d_ret), lambda s, k: (row_of(s, k), 0))
        + per_tile((ROW_TILE, d_rw), lambda s, k: (row_of(s, k), 0))
        + x_specs
        + per_tile((1, 1, SUBLANES, d), lambda s, k: (batch(s, k), jnp.minimum(tile(s, k), 1), 0, 0))
        + [pl.BlockSpec(w.shape, lambda s: (0, 0), pipeline_mode=pl.Buffered(1)),
           pl.BlockSpec((1, d), lambda s: (0, 0))],
        out_specs=pl.BlockSpec((g * ROW_TILE, d), lambda s: (s, 0)),
        out_shape=jax.ShapeDtypeStruct((b * n_out * ROW_TILE, d), f32),
        compiler_params=pltpu.CompilerParams(vmem_limit_bytes=VMEM_LIMIT),
        name="out_proj",
    )(*([ret.reshape(-1, d_ret)] * g), *([rwo.reshape(-1, d_rw)] * g),
      *[flat[i] for _ in range(g) for i in range(len(xs))], *([modv] * g), w, fw)
    return out.reshape(b, n_out * ROW_TILE, d)


def _rope_tables(n_ctx, seq, d_ret):
    nf = ROPE_NF
    inv = ROPE_BASE ** (-jnp.arange(nf, dtype=f32) / nf)
    pos = jnp.arange(seq)
    row_pos = (pos // GRID_W).astype(f32)
    col_pos = (pos % GRID_W).astype(f32)
    ang_r = row_pos[:, None] * inv[None, :]
    ang_c = col_pos[:, None] * inv[None, :]
    cos_h = jnp.concatenate([jnp.cos(ang_r)] * 2 + [jnp.cos(ang_c)] * 2, axis=-1)
    sin_h = jnp.concatenate([-jnp.sin(ang_r), jnp.sin(ang_r), -jnp.sin(ang_c), jnp.sin(ang_c)], axis=-1)
    reps = d_ret // HEAD
    cos_l = jnp.tile(cos_h, (1, reps))
    sin_l = jnp.tile(sin_h, (1, reps))
    cos_t = jnp.concatenate([jnp.ones((n_ctx, d_ret), f32), cos_l], axis=0)
    sin_t = jnp.concatenate([jnp.zeros((n_ctx, d_ret), f32), sin_l], axis=0)
    return cos_t, sin_t


def kernel(x, c, ctx, c_ctx, norm_w, w_mod, b_mod, w_in, ret_log_gamma, ret_norm_w, rwkv_shift_w, rwkv_w0, rwkv_w2, rwkv_a0, rwkv_a2, rwkv_k_k, rwkv_k_a, rwkv_r_k, rwkv_ln_w, rwkv_ln_b, w_out, final_norm_w):
    b, seq, d = x.shape
    n_ctx = ctx.shape[1]
    depth = w_in.shape[0]
    d_ret = ret_norm_w.shape[-1]
    d_rwkv = rwkv_w0.shape[-1]
    d_shift = rwkv_shift_w.shape[-1]

    assert b < COND_ROWS
    cond = jnp.zeros((COND_ROWS, d), f32).at[:b].set(c.astype(f32)).at[b].set(c_ctx.astype(f32))
    mods = _modulation(cond, w_mod, b_mod)
    cos_t, sin_t = _rope_tables(n_ctx, seq, d_ret)
    if depth > 1 and n_ctx == ROW_TILE:
        xs = (ctx.astype(f32), x.astype(f32))
    else:
        xs = (jnp.concatenate([ctx.astype(f32), x.astype(f32)], axis=1),)

    out = None
    for layer in range(depth):
        m = mods[layer].reshape(COND_ROWS, 3, d)
        m = jnp.stack([m[:, 1], m[:, 0], m[:, 2]], axis=1)
        lat = m[:b]
        cx = jnp.broadcast_to(m[b][None], (b, 3, d))
        modv = jnp.pad(jnp.stack([cx, lat], axis=1), ((0, 0), (0, 0), (0, SUBLANES - 3), (0, 0)))

        qkv, rw, gates = _inproj(xs, modv, norm_w[layer][None], w_in[layer].astype(bf16), cos_t, sin_t,
                                 d_ret, d_shift, d_rwkv)
        ret = _retention(qkv, gates, ret_log_gamma[layer], ret_norm_w[layer][None], n_ctx)
        rwo = _rwkv(rw, gates, rwkv_shift_w[layer], rwkv_w0[layer], rwkv_w2[layer], rwkv_a0[layer],
                    rwkv_a2[layer], rwkv_k_k[layer][None], rwkv_k_a[layer][None],
                    rwkv_r_k[layer].reshape(1, d_rwkv), rwkv_ln_w[layer][None],
                    rwkv_ln_b[layer][None], n_ctx)
        final = layer == depth - 1
        res = _outproj(ret, rwo, xs, modv, w_out[layer].astype(bf16), final_norm_w[None], n_ctx, final)
        if final:
            out = res
        else:
            xs = (res,)
    return out
```
